```python
import math
import jax, jax.numpy as jnp
from jax import lax
import numpy as np

D_MODEL = 1024
BATCH = 4
SEQ = 4096
DEPTH = 1

HEAD_DIM_A = 64
N_HEADS_A = 8
WIDTH_A = N_HEADS_A * HEAD_DIM_A
DILATED_BRANCHES = ((128, 1), (512, 4), (2048, 16))

N_HEADS_B = 8
QK_NOPE_DIM = 64
QK_ROPE_DIM = 32
V_HEAD_DIM = 64
QK_HEAD_DIM_B = QK_NOPE_DIM + QK_ROPE_DIM
Q_LORA_RANK = 768
KV_LORA_RANK = 256
WIDTH_B = N_HEADS_B * V_HEAD_DIM
ROPE_THETA = 10000.0

MIX_WIDTH = WIDTH_A + WIDTH_B
D_FF = 4 * D_MODEL

REL_BUCKETS = 32
REL_MAX_DIST = 2048
Q_BLOCK = 128
EPS = 1e-6

IN_SPLITS = (WIDTH_A, WIDTH_A, WIDTH_A, Q_LORA_RANK, KV_LORA_RANK, QK_ROPE_DIM)
IN_WIDTH = sum(IN_SPLITS)

kernel_name = "hybrid_dilated_mla_sqrelu_layer"


def rms_norm(x, g):
    xf = x.astype(jnp.float32)
    y = xf * lax.rsqrt(jnp.mean(xf * xf, axis=-1, keepdims=True) + EPS)
    return (y * g.astype(jnp.float32)).astype(x.dtype)


def t5_causal_bucket(dist):
    dist = np.asarray(dist, dtype=np.int64)
    max_exact = REL_BUCKETS // 2
    safe = np.maximum(dist, 1).astype(np.float32)
    large = max_exact + (np.log(safe / max_exact) / math.log(REL_MAX_DIST / max_exact)
                         * (REL_BUCKETS - max_exact)).astype(np.int64)
    large = np.minimum(large, REL_BUCKETS - 1)
    return np.where(dist < max_exact, dist, large).astype(np.int32)


def apply_rope(x, positions):
    r = x.shape[-1]
    inv_freq = 1.0 / (ROPE_THETA ** (jnp.arange(0, r, 2, dtype=jnp.float32) / r))
    ang = positions.astype(jnp.float32)[..., None] * inv_freq
    cos = jnp.cos(ang)[:, :, None, :]
    sin = jnp.sin(ang)[:, :, None, :]
    xf = x.astype(jnp.float32)
    x1, x2 = xf[..., : r // 2], xf[..., r // 2:]
    out = jnp.concatenate([x1 * cos - x2 * sin, x2 * cos + x1 * sin], axis=-1)
    return out.astype(x.dtype)


def to_query_blocks(t):
    b, s = t.shape[:2]
    nblk = s // Q_BLOCK
    return jnp.moveaxis(t.reshape((b, nblk, Q_BLOCK) + t.shape[2:]), 1, 0)


def from_query_blocks(t):
    t = jnp.moveaxis(t, 0, 1)
    return t.reshape((t.shape[0], t.shape[1] * t.shape[2]) + t.shape[3:])


def dilated_attention(q, k, v, rel_bias):
    b, s, h, dh = q.shape
    scale = 1.0 / math.sqrt(dh)
    branches = []
    for window, dil in DILATED_BRANCHES:
        offs_np = dil * np.arange(window // dil + 1, dtype=np.int32)
        bias = rel_bias[jnp.asarray(t5_causal_bucket(offs_np))].T.astype(jnp.float32)
        branches.append((jnp.asarray(offs_np), bias))
    nblk = s // Q_BLOCK

    def block_fn(args):
        i, qi = args
        tq = i * Q_BLOCK + jnp.arange(Q_BLOCK, dtype=jnp.int32)
        ms, ss, os_ = [], [], []
        for offs, bias in branches:
            idx = tq[:, None] - offs[None, :]
            valid = idx >= 0
            idxc = jnp.maximum(idx, 0)
            kg = k[:, idxc]
            vg = v[:, idxc]
            logits = jnp.einsum('bqhd,bqkhd->bhqk', qi, kg).astype(jnp.float32) * scale
            logits = logits + bias[None, :, None, :]
            logits = jnp.where(valid[None, None], logits, -jnp.inf)
            m = jnp.max(logits, axis=-1, keepdims=True)
            p = jnp.exp(logits - m)
            den = jnp.sum(p, axis=-1, keepdims=True)
            o = jnp.einsum('bhqk,bqkhd->bhqd', p, vg.astype(jnp.float32)) / den
            ms.append(m); ss.append(den); os_.append(o)
        m_all = jnp.stack(ms)
        w = jnp.stack(ss) * jnp.exp(m_all - jnp.max(m_all, axis=0, keepdims=True))
        out = jnp.sum(w * jnp.stack(os_), axis=0) / jnp.sum(w, axis=0)
        return jnp.transpose(out, (0, 2, 1, 3)).astype(q.dtype)

    out = lax.map(block_fn, (jnp.arange(nblk, dtype=jnp.int32), to_query_blocks(q)))
    return from_query_blocks(out)


def causal_block_attention(q, k, v):
    b, s, h, dqk = q.shape
    scale = 1.0 / math.sqrt(dqk)
    kpos = jnp.arange(s, dtype=jnp.int32)
    nblk = s // Q_BLOCK

    def block_fn(args):
        i, qi = args
        tq = i * Q_BLOCK + jnp.arange(Q_BLOCK, dtype=jnp.int32)
        logits = jnp.einsum('bqhd,bkhd->bhqk', qi, k).astype(jnp.float32) * scale
        logits = jnp.where((kpos[None, :] <= tq[:, None])[None, None], logits, -jnp.inf)
        p = jax.nn.softmax(logits, axis=-1)
        out = jnp.einsum('bhqk,bkhd->bqhd', p, v.astype(jnp.float32))
        return out.astype(q.dtype)

    out = lax.map(block_fn, (jnp.arange(nblk, dtype=jnp.int32), to_query_blocks(q)))
    return from_query_blocks(out)


def setup_inputs(seed: int = 0) -> dict:
    key = jax.random.key(seed)
    ks = jax.random.split(key, 20)
    f32 = jnp.float32

    def nrm(k, shape, fan_in):
        return jax.random.normal(k, shape, f32) * (fan_in ** -0.5)

    def gain(k, n):
        return 1.0 + 0.05 * jax.random.normal(k, (n,), f32)

    x = jax.random.normal(ks[0], (BATCH, SEQ, D_MODEL), f32)
    offset = jax.random.randint(ks[1], (BATCH, 1), 0, 1024, dtype=jnp.int32)
    positions = jnp.arange(SEQ, dtype=jnp.int32)[None, :] + offset
    return {
        "x": x,
        "positions": positions,
        "norm_mix_g": gain(ks[2], D_MODEL),
        "w_in": nrm(ks[3], (D_MODEL, IN_WIDTH), D_MODEL),
        "qnorm_a_g": gain(ks[4], HEAD_DIM_A),
        "knorm_a_g": gain(ks[5], HEAD_DIM_A),
        "rel_bias": 0.5 * jax.random.normal(ks[6], (REL_BUCKETS, N_HEADS_A), f32),
        "cq_norm_g": gain(ks[7], Q_LORA_RANK),
        "ckv_norm_g": gain(ks[8], KV_LORA_RANK),
        "w_uq": nrm(ks[9], (Q_LORA_RANK, N_HEADS_B * QK_HEAD_DIM_B), Q_LORA_RANK),
        "w_ukv": nrm(ks[10], (KV_LORA_RANK, N_HEADS_B * (QK_NOPE_DIM + V_HEAD_DIM)), KV_LORA_RANK),
        "qnorm_b_g": gain(ks[11], QK_HEAD_DIM_B),
        "knorm_b_g": gain(ks[12], QK_HEAD_DIM_B),
        "w_o": nrm(ks[13], (MIX_WIDTH, D_MODEL), MIX_WIDTH),
        "norm_ffn_g": gain(ks[14], D_MODEL),
        "w_ff1": nrm(ks[15], (D_MODEL, D_FF), D_MODEL),
        "w_ff2": nrm(ks[16], (D_FF, D_MODEL), D_FF),
    }


def reference(x, positions, norm_mix_g, w_in, qnorm_a_g, knorm_a_g, rel_bias,
              cq_norm_g, ckv_norm_g, w_uq, w_ukv, qnorm_b_g, knorm_b_g, w_o,
              norm_ffn_g, w_ff1, w_ff2):
    b, s, _ = x.shape
    h = x
    for _layer in range(DEPTH):
        xn = rms_norm(h, norm_mix_g)
        proj = jnp.einsum('bsd,de->bse', xn, w_in)
        cuts = list(np.cumsum(IN_SPLITS)[:-1])
        qa, ka, va, c_q, c_kv, k_rope = jnp.split(proj, cuts, axis=-1)

        qa = rms_norm(qa.reshape(b, s, N_HEADS_A, HEAD_DIM_A), qnorm_a_g)
        ka = rms_norm(ka.reshape(b, s, N_HEADS_A, HEAD_DIM_A), knorm_a_g)
        va = va.reshape(b, s, N_HEADS_A, HEAD_DIM_A)
        out_a = dilated_attention(qa, ka, va, rel_bias).reshape(b, s, WIDTH_A)

        qb = jnp.einsum('bsr,re->bse', rms_norm(c_q, cq_norm_g), w_uq)
        qb = qb.reshape(b, s, N_HEADS_B, QK_HEAD_DIM_B)
        kv = jnp.einsum('bsr,re->bse', rms_norm(c_kv, ckv_norm_g), w_ukv)
        kv = kv.reshape(b, s, N_HEADS_B, QK_NOPE_DIM + V_HEAD_DIM)
        k_nope, vb = kv[..., :QK_NOPE_DIM], kv[..., QK_NOPE_DIM:]
        k_rope_h = jnp.broadcast_to(k_rope[:, :, None, :], (b, s, N_HEADS_B, QK_ROPE_DIM))
        kb = jnp.concatenate([k_nope, k_rope_h], axis=-1)
        qb = rms_norm(qb, qnorm_b_g)
        kb = rms_norm(kb, knorm_b_g)
        qb = jnp.concatenate([qb[..., :QK_NOPE_DIM],
                              apply_rope(qb[..., QK_NOPE_DIM:], positions)], axis=-1)
        kb = jnp.concatenate([kb[..., :QK_NOPE_DIM],
                              apply_rope(kb[..., QK_NOPE_DIM:], positions)], axis=-1)
        out_b = causal_block_attention(qb, kb, vb).reshape(b, s, WIDTH_B)

        mixed = jnp.concatenate([out_a, out_b], axis=-1)
        h = h + jnp.einsum('bse,ed->bsd', mixed, w_o)

        hn = rms_norm(h, norm_ffn_g)
        hid = jnp.square(jax.nn.relu(jnp.einsum('bsd,df->bsf', hn, w_ff1)))
        h = h + jnp.einsum('bsf,fd->bsd', hid, w_ff2)
    return h
```

```python
import functools
import math

import numpy as np
import jax
import jax.numpy as jnp
from jax import lax
from jax.experimental import pallas as pl
from jax.experimental.pallas import tpu as pltpu

F32 = jnp.float32
BF16 = jnp.bfloat16

D_MODEL = 1024
HEAD_DIM_A = 64
N_HEADS_A = 8
WIDTH_A = N_HEADS_A * HEAD_DIM_A
DILATED_BRANCHES = ((128, 1), (512, 4), (2048, 16))
N_HEADS_B = 8
QK_NOPE_DIM = 64
QK_ROPE_DIM = 32
V_HEAD_DIM = 64
QK_HEAD_DIM_B = QK_NOPE_DIM + QK_ROPE_DIM
Q_LORA_RANK = 768
KV_LORA_RANK = 256
WIDTH_B = N_HEADS_B * V_HEAD_DIM
ROPE_THETA = 10000.0
D_FF = 4 * D_MODEL
REL_BUCKETS = 32
REL_MAX_DIST = 2048
EPS = 1e-6

LANES = 128
HEAD_PAIRS = N_HEADS_A // 2
WIN = 128
NEG = -1e30
VMEM_LIMIT = 56 * 1024 * 1024

TM_PROJ = 512
TM_FFN = 512
TQ_B = 256
TK_B = 256
FF_CHUNK = 1024

SCALE_A = 1.0 / math.sqrt(HEAD_DIM_A)
SCALE_B = 1.0 / math.sqrt(QK_HEAD_DIM_B)


def _nt_dot(a, b):
    return lax.dot_general(a, b, (((1,), (1,)), ((), ())), preferred_element_type=F32)


def _dot(a, b):
    return jnp.dot(a, b, preferred_element_type=F32)


def _const_spec(shape):
    nd = len(shape)
    return pl.BlockSpec(shape, lambda *_: (0,) * nd, pipeline_mode=pl.Buffered(1))


def _proj_kernel(x_ref, pos_ref, gmix_ref, wqkv_ref, wcq_ref, wckv_ref, wkr_ref,
                 gqa_ref, gka_ref, gcq_ref, gckv_ref, wuq_ref, wuk_ref, wuv_ref,
                 gqb_ref, gkbn_ref, gkbr_ref, invf_ref,
                 qa_ref, ka_ref, va_ref, qb_ref, kb_ref, vb_ref):
    x = x_ref[...]
    ms = jnp.mean(x * x, axis=-1, keepdims=True)
    xn = (x * lax.rsqrt(ms + EPS) * gmix_ref[...]).astype(BF16)

    lane = lax.broadcasted_iota(jnp.int32, (1, LANES), 1)
    first_head = lane < HEAD_DIM_A

    def pair_norm(x2, g_row):
        sq = x2 * x2
        ss0 = jnp.sum(jnp.where(first_head, sq, 0.0), axis=-1, keepdims=True)
        ss1 = jnp.sum(jnp.where(first_head, 0.0, sq), axis=-1, keepdims=True)
        r = jnp.where(first_head,
                      lax.rsqrt(ss0 / HEAD_DIM_A + EPS),
                      lax.rsqrt(ss1 / HEAD_DIM_A + EPS))
        return x2 * r * g_row

    qkv = _dot(xn, wqkv_ref[...])
    for p in range(HEAD_PAIRS):
        sl = slice(p * LANES, (p + 1) * LANES)
        qa_ref[:, sl] = (pair_norm(qkv[:, sl], gqa_ref[...]) * SCALE_A).astype(BF16)
        ksl = slice(WIDTH_A + p * LANES, WIDTH_A + (p + 1) * LANES)
        ka_ref[:, sl] = pair_norm(qkv[:, ksl], gka_ref[...]).astype(BF16)
    va_ref[...] = qkv[:, 2 * WIDTH_A:].astype(BF16)

    cq = _dot(xn, wcq_ref[...])
    cqn = (cq * lax.rsqrt(jnp.mean(cq * cq, axis=-1, keepdims=True) + EPS)
           * gcq_ref[...]).astype(BF16)
    qb = _dot(cqn, wuq_ref[...])
    ckv = _dot(xn, wckv_ref[...])
    ckvn = (ckv * lax.rsqrt(jnp.mean(ckv * ckv, axis=-1, keepdims=True) + EPS)
            * gckv_ref[...]).astype(BF16)
    kn = _dot(ckvn, wuk_ref[...])
    vb_ref[...] = _dot(ckvn, wuv_ref[...]).astype(BF16)
    kr = _dot(xn, wkr_ref[...])

    ang = pos_ref[...].astype(F32) * invf_ref[...]
    cs = jnp.cos(ang)
    sn = jnp.sin(ang)
    half = QK_ROPE_DIM // 2
    lo = (lane >= QK_NOPE_DIM) & (lane < QK_NOPE_DIM + half)
    hi = (lane >= QK_NOPE_DIM + half) & (lane < QK_HEAD_DIM_B)
    s_lo = jnp.where(lo, -sn, 0.0)
    s_hi = jnp.where(hi, sn, 0.0)

    def rope(y):
        return (y * cs + pltpu.roll(y, LANES - half, 1) * s_lo
                + pltpu.roll(y, half, 1) * s_hi)

    ss_kr = jnp.sum(kr * kr, axis=-1, keepdims=True)
    kr_rot = rope(kr * gkbr_ref[...])
    for h in range(N_HEADS_B):
        sl = slice(h * LANES, (h + 1) * LANES)
        qh = qb[:, sl]
        rq = lax.rsqrt(jnp.sum(qh * qh, axis=-1, keepdims=True) / QK_HEAD_DIM_B + EPS)
        qb_ref[:, sl] = (rope(qh * rq * gqb_ref[...]) * SCALE_B).astype(BF16)
        kh = kn[:, sl]
        rk = lax.rsqrt((jnp.sum(kh * kh, axis=-1, keepdims=True) + ss_kr) / QK_HEAD_DIM_B + EPS)
        kb_ref[:, sl] = ((kh * gkbn_ref[...] + kr_rot) * rk).astype(BF16)


def _proj_call(x2, pos2, consts):
    n = x2.shape[0]
    tm = TM_PROJ
    row = lambda i: (i, 0)
    in_specs = [pl.BlockSpec((tm, D_MODEL), row), pl.BlockSpec((tm, 1), row)]
    in_specs += [_const_spec(c.shape) for c in consts]
    widths = (WIDTH_A, WIDTH_A, WIDTH_A, N_HEADS_B * LANES, N_HEADS_B * LANES, WIDTH_B)
    return pl.pallas_call(
        _proj_kernel,
        grid=(n // tm,),
        in_specs=in_specs,
        out_specs=[pl.BlockSpec((tm, w), row) for w in widths],
        out_shape=[jax.ShapeDtypeStruct((n, w), BF16) for w in widths],
        compiler_params=pltpu.CompilerParams(
            dimension_semantics=("parallel",), vmem_limit_bytes=VMEM_LIMIT),
        name="proj",
    )(x2, pos2, *consts)


def _attn_a_kernel(q_ref, k_ref, v_ref, bias_ref, o_ref,
                   qf, kf, vf, acc, m0, m1, l0, l1, *, seq, pad):
    qf[...] = q_ref[...].astype(F32)
    kf[pl.ds(0, pad), :] = jnp.zeros((pad, LANES), F32)
    vf[pl.ds(0, pad), :] = jnp.zeros((pad, LANES), F32)
    kf[pl.ds(pad, seq), :] = k_ref[...].astype(F32)
    vf[pl.ds(pad, seq), :] = v_ref[...].astype(F32)
    acc[...] = jnp.zeros_like(acc)
    for m_s, l_s in ((m0, l0), (m1, l1)):
        m_s[...] = jnp.full(m_s.shape, NEG, F32)
        l_s[...] = jnp.zeros_like(l_s)

    lane = lax.broadcasted_iota(jnp.int32, (1, LANES), 1)
    first_head = lane < HEAD_DIM_A
    stats = ((m0, l0), (m1, l1))

    for bi, (window, dil) in enumerate(DILATED_BRANCHES):
        assert window // dil == WIN
        nblk = seq // (dil * WIN)

        def block(r, j, bi=bi, dil=dil):
            q0 = r + dil * WIN * j
            k0 = pad + q0 - dil * WIN
            qrows = pl.ds(q0, WIN, stride=dil) if dil > 1 else pl.ds(q0, WIN)
            krows = pl.ds(k0, 2 * WIN, stride=dil) if dil > 1 else pl.ds(k0, 2 * WIN)
            q = qf[qrows, :]
            kk = kf[krows, :].astype(BF16)
            vv = vf[krows, :].astype(BF16)
            first_blk = jnp.where(j == 0, 1, 0)
            alphas, outs = [], []
            for hh, (m_s, l_s) in enumerate(stats):
                mine = first_head if hh == 0 else jnp.logical_not(first_head)
                qh = jnp.where(mine, q, 0.0).astype(BF16)
                s = _nt_dot(qh, kk) + bias_ref[bi, first_blk, hh]
                m_old = m_s[qrows, :]
                m_new = jnp.maximum(m_old, jnp.max(s, axis=-1, keepdims=True))
                alpha = jnp.exp(m_old - m_new)
                p = jnp.exp(s - jnp.concatenate([m_new, m_new], axis=1))
                l_s[qrows, :] = alpha * l_s[qrows, :] + jnp.sum(p, axis=-1, keepdims=True)
                m_s[qrows, :] = m_new
                alphas.append(alpha)
                outs.append(_dot(p.astype(BF16), vv))
            acc[qrows, :] = (jnp.where(first_head, alphas[0], alphas[1]) * acc[qrows, :]
                             + jnp.where(first_head, outs[0], outs[1]))

        def class_body(r, carry, nblk=nblk, block=block):
            def blk_body(j, c):
                block(r, j)
                return c
            return lax.fori_loop(0, nblk, blk_body, carry)

        lax.fori_loop(0, dil, class_body, 0)

    o_ref[...] = (acc[...] / jnp.where(first_head, l0[...], l1[...])).astype(o_ref.dtype)


def _attn_a_call(qa, ka, va, bias):
    b, s, _ = qa.shape
    pad = DILATED_BRANCHES[-1][0]
    blk = pl.BlockSpec((None, s, LANES), lambda bb, p: (bb, 0, p))
    bias_spec = pl.BlockSpec((None,) + bias.shape[1:], lambda bb, p: (p, 0, 0, 0, 0, 0))
    stat = pltpu.VMEM((s, LANES), F32)
    return pl.pallas_call(
        functools.partial(_attn_a_kernel, seq=s, pad=pad),
        grid=(b, HEAD_PAIRS),
        in_specs=[blk, blk, blk, bias_spec],
        out_specs=blk,
        out_shape=jax.ShapeDtypeStruct((b, s, WIDTH_A), BF16),
        scratch_shapes=[stat, pltpu.VMEM((pad + s, LANES), F32), pltpu.VMEM((pad + s, LANES), F32),
                        stat, stat, stat, stat, stat],
        compiler_params=pltpu.CompilerParams(
            dimension_semantics=("parallel", "parallel"), vmem_limit_bytes=VMEM_LIMIT),
        name="attn_a",
    )(qa, ka, va, bias)


def _t5_causal_bucket(dist):
    dist = np.asarray(dist, dtype=np.int64)
    max_exact = REL_BUCKETS // 2
    safe = np.maximum(dist, 1).astype(np.float32)
    large = max_exact + (np.log(safe / max_exact) / math.log(REL_MAX_DIST / max_exact)
                         * (REL_BUCKETS - max_exact)).astype(np.int64)
    large = np.minimum(large, REL_BUCKETS - 1)
    return np.where(dist < max_exact, dist, large).astype(np.int32)


def _attn_a_bias(rel_bias):
    i = np.arange(WIN)[:, None]
    c = np.arange(2 * WIN)[None, :]
    sub = WIN + i - c
    valid = (sub >= 0) & (sub <= WIN)
    tables = []
    for _, dil in DILATED_BRANCHES:
        bucket = _t5_causal_bucket(np.clip(sub, 0, WIN) * dil)
        vals = rel_bias.astype(F32)[jnp.asarray(bucket)]
        vals = jnp.moveaxis(vals, -1, 0)
        full = jnp.where(jnp.asarray(valid)[None], vals, NEG)
        first = jnp.where(jnp.asarray(valid & (c >= WIN))[None], vals, NEG)
        tables.append(jnp.stack([full, first], axis=0))
    t = jnp.stack(tables, axis=0)
    t = t.reshape(len(DILATED_BRANCHES), 2, HEAD_PAIRS, 2, WIN, 2 * WIN)
    return jnp.transpose(t, (2, 0, 1, 3, 4, 5))


def _attn_b_kernel(q_ref, k_ref, v_ref, o_ref, acc, m_s, l_s):
    qi = pl.program_id(2)
    tq, tk = TQ_B, TK_B
    lane = lax.broadcasted_iota(jnp.int32, (1, LANES), 1)
    first_head = lane < V_HEAD_DIM
    row = lax.broadcasted_iota(jnp.int32, (tq, tk), 0)
    col = lax.broadcasted_iota(jnp.int32, (tq, tk), 1)
    outs, dens = [], []
    for hh in range(2):
        q = q_ref[:, hh * LANES:(hh + 1) * LANES]
        acc[...] = jnp.zeros_like(acc)
        m_s[...] = jnp.full(m_s.shape, NEG, F32)
        l_s[...] = jnp.zeros_like(l_s)

        def step(kb, masked, q=q, hh=hh):
            k = k_ref[pl.ds(kb * tk, tk), hh * LANES:(hh + 1) * LANES]
            v = v_ref[pl.ds(kb * tk, tk), :]
            s = _nt_dot(q, k)
            if masked:
                s = jnp.where(col <= row, s, NEG)
            m_old = m_s[...]
            m_new = jnp.maximum(m_old, jnp.max(s, axis=-1, keepdims=True))
            alpha = jnp.exp(m_old - m_new)
            p = jnp.exp(s - jnp.concatenate([m_new] * (tk // LANES), axis=1))
            l_s[...] = alpha * l_s[...] + jnp.sum(p, axis=-1, keepdims=True)
            m_s[...] = m_new
            acc[...] = alpha * acc[...] + _dot(p.astype(BF16), v)

        def body(kb, c, step=step):
            step(kb, False)
            return c

        lax.fori_loop(0, qi, body, 0)
        step(qi, True)
        outs.append(acc[...])
        dens.append(l_s[...])
    o_ref[...] = (jnp.where(first_head, outs[0], outs[1])
                  / jnp.where(first_head, dens[0], dens[1])).astype(o_ref.dtype)


def _attn_b_call(qb, kb, vb):
    b, s, _ = qb.shape
    assert TQ_B == TK_B
    return pl.pallas_call(
        _attn_b_kernel,
        grid=(b, HEAD_PAIRS, s // TQ_B),
        in_specs=[pl.BlockSpec((None, TQ_B, 2 * LANES), lambda bb, p, i: (bb, i, p)),
                  pl.BlockSpec((None, s, 2 * LANES), lambda bb, p, i: (bb, 0, p)),
                  pl.BlockSpec((None, s, LANES), lambda bb, p, i: (bb, 0, p))],
        out_specs=pl.BlockSpec((None, TQ_B, LANES), lambda bb, p, i: (bb, i, p)),
        out_shape=jax.ShapeDtypeStruct((b, s, WIDTH_B), BF16),
        scratch_shapes=[pltpu.VMEM((TQ_B, LANES), F32)] * 3,
        compiler_params=pltpu.CompilerParams(
            dimension_semantics=("parallel", "parallel", "arbitrary"),
            vmem_limit_bytes=VMEM_LIMIT),
        name="attn_b",
    )(qb, kb, vb)


def _ffn_kernel(x_ref, a_ref, b_ref, woa_ref, wob_ref, g_ref, w1_ref, w2_ref, o_ref):
    h = x_ref[...] + _dot(a_ref[...], woa_ref[...]) + _dot(b_ref[...], wob_ref[...])
    hn = (h * lax.rsqrt(jnp.mean(h * h, axis=-1, keepdims=True) + EPS) * g_ref[...]).astype(BF16)
    mlp = None
    for c in range(D_FF // FF_CHUNK):
        sl = slice(c * FF_CHUNK, (c + 1) * FF_CHUNK)
        hid = jnp.square(jnp.maximum(_dot(hn, w1_ref[:, sl]), 0.0)).astype(BF16)
        d = _dot(hid, w2_ref[sl, :])
        mlp = d if mlp is None else mlp + d
    o_ref[...] = h + mlp


def _ffn_call(x2, a2, b2, woa, wob, g, w1, w2):
    n = x2.shape[0]
    tm = TM_FFN
    row = lambda i: (i, 0)
    consts = (woa, wob, g, w1, w2)
    return pl.pallas_call(
        _ffn_kernel,
        grid=(n // tm,),
        in_specs=[pl.BlockSpec((tm, D_MODEL), row), pl.BlockSpec((tm, WIDTH_A), row),
                  pl.BlockSpec((tm, WIDTH_B), row)] + [_const_spec(c.shape) for c in consts],
        out_specs=pl.BlockSpec((tm, D_MODEL), row),
        out_shape=jax.ShapeDtypeStruct((n, D_MODEL), F32),
        compiler_params=pltpu.CompilerParams(
            dimension_semantics=("parallel",), vmem_limit_bytes=VMEM_LIMIT),
        name="ffn",
    )(x2, a2, b2, *consts)


def _pad_heads(w, n_heads, width):
    k = w.shape[0]
    w = w.reshape(k, n_heads, width)
    return jnp.pad(w, ((0, 0), (0, 0), (0, LANES - width))).reshape(k, n_heads * LANES)


def _lane_row(g, offset=0):
    return jnp.pad(g.astype(F32), (offset, LANES - offset - g.shape[0]))[None, :]


def kernel(x, positions, norm_mix_g, w_in, qnorm_a_g, knorm_a_g, rel_bias, cq_norm_g, ckv_norm_g,
           w_uq, w_ukv, qnorm_b_g, knorm_b_g, w_o, norm_ffn_g, w_ff1, w_ff2):
    b, s, d = x.shape
    n = b * s
    c0 = 3 * WIDTH_A
    c1 = c0 + Q_LORA_RANK
    c2 = c1 + KV_LORA_RANK
    w_qkv = w_in[:, :c0].astype(BF16)
    w_cq = w_in[:, c0:c1].astype(BF16)
    w_ckv = w_in[:, c1:c2].astype(BF16)
    w_kr = jnp.pad(w_in[:, c2:], ((0, 0), (QK_NOPE_DIM, LANES - QK_HEAD_DIM_B))).astype(BF16)
    w_uq_p = _pad_heads(w_uq, N_HEADS_B, QK_HEAD_DIM_B).astype(BF16)
    w_ukv3 = w_ukv.reshape(KV_LORA_RANK, N_HEADS_B, QK_NOPE_DIM + V_HEAD_DIM)
    w_uk_p = _pad_heads(w_ukv3[:, :, :QK_NOPE_DIM].reshape(KV_LORA_RANK, -1),
                        N_HEADS_B, QK_NOPE_DIM).astype(BF16)
    w_uv = w_ukv3[:, :, QK_NOPE_DIM:].reshape(KV_LORA_RANK, WIDTH_B).astype(BF16)
    inv_freq = 1.0 / (ROPE_THETA ** (jnp.arange(0, QK_ROPE_DIM, 2, dtype=F32) / QK_ROPE_DIM))
    consts = (
        norm_mix_g.astype(F32)[None, :], w_qkv, w_cq, w_ckv, w_kr,
        jnp.tile(qnorm_a_g.astype(F32), 2)[None, :], jnp.tile(knorm_a_g.astype(F32), 2)[None, :],
        cq_norm_g.astype(F32)[None, :], ckv_norm_g.astype(F32)[None, :],
        w_uq_p, w_uk_p, w_uv,
        _lane_row(qnorm_b_g), _lane_row(knorm_b_g[:QK_NOPE_DIM]),
        _lane_row(knorm_b_g[QK_NOPE_DIM:], QK_NOPE_DIM),
        _lane_row(jnp.tile(inv_freq, 2), QK_NOPE_DIM),
    )
    qa, ka, va, qb, kb, vb = _proj_call(x.reshape(n, d), positions.reshape(n, 1), consts)

    out_a = _attn_a_call(qa.reshape(b, s, -1), ka.reshape(b, s, -1), va.reshape(b, s, -1),
                         _attn_a_bias(rel_bias))
    out_b = _attn_b_call(qb.reshape(b, s, -1), kb.reshape(b, s, -1), vb.reshape(b, s, -1))

    out = _ffn_call(x.reshape(n, d), out_a.reshape(n, -1), out_b.reshape(n, -1),
                    w_o[:WIDTH_A].astype(BF16), w_o[WIDTH_A:].astype(BF16),
                    norm_ffn_g.astype(F32)[None, :], w_ff1.astype(BF16), w_ff2.astype(BF16))
    return out.reshape(b, s, d)
```

```python
import functools
import math

import numpy as np
import jax
import jax.numpy as jnp
from jax import lax
from jax.experimental import pallas as pl
from jax.experimental.pallas import tpu as pltpu

F32 = jnp.float32
BF16 = jnp.bfloat16

D_MODEL = 1024
HEAD_DIM_A = 64
N_HEADS_A = 8
WIDTH_A = N_HEADS_A * HEAD_DIM_A
DILATED_BRANCHES = ((128, 1), (512, 4), (2048, 16))
N_HEADS_B = 8
QK_NOPE_DIM = 64
QK_ROPE_DIM = 32
V_HEAD_DIM = 64
QK_HEAD_DIM_B = QK_NOPE_DIM + QK_ROPE_DIM
Q_LORA_RANK = 768
KV_LORA_RANK = 256
WIDTH_B = N_HEADS_B * V_HEAD_DIM
ROPE_THETA = 10000.0
D_FF = 4 * D_MODEL
REL_BUCKETS = 32
REL_MAX_DIST = 2048
EPS = 1e-6

LANES = 128
HALF = LANES // 2
HEAD_PAIRS = N_HEADS_A // 2
WIN = 128
NEG = -1e30
VMEM_LIMIT = 56 * 1024 * 1024

TM_PROJ = 512
TM_FFN = 512
TQ_B = 512
FF_CHUNK = 1024
UNROLL_A = 4

LOG2E = math.log2(math.e)
SCALE_A = LOG2E / math.sqrt(HEAD_DIM_A)
SCALE_B = LOG2E / math.sqrt(QK_HEAD_DIM_B)

assert HEAD_DIM_A == HALF and V_HEAD_DIM == HALF


def _nt_dot(a, b):
    return lax.dot_general(a, b, (((1,), (1,)), ((), ())), preferred_element_type=F32)


def _dot(a, b):
    return jnp.dot(a, b, preferred_element_type=F32)


def _const_spec(shape):
    nd = len(shape)
    return pl.BlockSpec(shape, lambda *_: (0,) * nd, pipeline_mode=pl.Buffered(1))


def _proj_kernel(x_ref, pos_ref, gmix_ref, wqkv_ref, wcq_ref, wckv_ref, wkr_ref,
                 gqa_ref, gka_ref, gcq_ref, gckv_ref, wuq_ref, wuk_ref, wuv_ref,
                 gqb_ref, gkbn_ref, gkbr_ref, invf_ref, ones_ref,
                 qa_ref, ka_ref, va_ref, qb_ref, kb_ref, vb_ref):
    x = x_ref[...]
    ms = jnp.mean(x * x, axis=-1, keepdims=True)
    xn = (x * lax.rsqrt(ms + EPS) * gmix_ref[...]).astype(BF16)

    lane = lax.broadcasted_iota(jnp.int32, (1, LANES), 1)
    first_head = lane < HALF

    def pair_norm(x2, g_row):
        sq = x2 * x2
        ss0 = jnp.sum(jnp.where(first_head, sq, 0.0), axis=-1, keepdims=True)
        ss1 = jnp.sum(jnp.where(first_head, 0.0, sq), axis=-1, keepdims=True)
        r = jnp.where(first_head,
                      lax.rsqrt(ss0 / HEAD_DIM_A + EPS),
                      lax.rsqrt(ss1 / HEAD_DIM_A + EPS))
        return x2 * r * g_row

    qkv = _dot(xn, wqkv_ref[...])
    for p in range(HEAD_PAIRS):
        sl = slice(p * LANES, (p + 1) * LANES)
        qa_ref[:, sl] = (pair_norm(qkv[:, sl], gqa_ref[...]) * SCALE_A).astype(BF16)
        ksl = slice(WIDTH_A + p * LANES, WIDTH_A + (p + 1) * LANES)
        ka_ref[:, sl] = pair_norm(qkv[:, ksl], gka_ref[...]).astype(BF16)
    va_ref[...] = qkv[:, 2 * WIDTH_A:].astype(BF16)

    cq = _dot(xn, wcq_ref[...])
    cqn = (cq * lax.rsqrt(jnp.mean(cq * cq, axis=-1, keepdims=True) + EPS)
           * gcq_ref[...]).astype(BF16)
    qb = _dot(cqn, wuq_ref[...])
    ckv = _dot(xn, wckv_ref[...])
    ckvn = (ckv * lax.rsqrt(jnp.mean(ckv * ckv, axis=-1, keepdims=True) + EPS)
            * gckv_ref[...]).astype(BF16)
    kn = _dot(ckvn, wuk_ref[...])
    vb_ref[...] = (_dot(ckvn, wuv_ref[...]) + ones_ref[...]).astype(BF16)
    kr = _dot(xn, wkr_ref[...])

    ang = pos_ref[...].astype(F32) * invf_ref[...]
    cs = jnp.cos(ang)
    sn = jnp.sin(ang)
    half = QK_ROPE_DIM // 2
    lo = (lane >= QK_NOPE_DIM) & (lane < QK_NOPE_DIM + half)
    hi = (lane >= QK_NOPE_DIM + half) & (lane < QK_HEAD_DIM_B)
    s_lo = jnp.where(lo, -sn, 0.0)
    s_hi = jnp.where(hi, sn, 0.0)

    def rope(y):
        return (y * cs + pltpu.roll(y, LANES - half, 1) * s_lo
                + pltpu.roll(y, half, 1) * s_hi)

    ss_kr = jnp.sum(kr * kr, axis=-1, keepdims=True)
    kr_rot = rope(kr * gkbr_ref[...])
    for h in range(N_HEADS_B):
        sl = slice(h * LANES, (h + 1) * LANES)
        qh = qb[:, sl]
        rq = lax.rsqrt(jnp.sum(qh * qh, axis=-1, keepdims=True) / QK_HEAD_DIM_B + EPS)
        qb_ref[:, sl] = (rope(qh * rq * gqb_ref[...]) * SCALE_B).astype(BF16)
        kh = kn[:, sl]
        rk = lax.rsqrt((jnp.sum(kh * kh, axis=-1, keepdims=True) + ss_kr) / QK_HEAD_DIM_B + EPS)
        kb_ref[:, sl] = ((kh * gkbn_ref[...] + kr_rot) * rk).astype(BF16)


def _proj_call(x2, pos2, consts):
    n = x2.shape[0]
    tm = TM_PROJ
    row = lambda i: (i, 0)
    in_specs = [pl.BlockSpec((tm, D_MODEL), row), pl.BlockSpec((tm, 1), row)]
    in_specs += [_const_spec(c.shape) for c in consts]
    wide = N_HEADS_B * LANES
    widths = (WIDTH_A, WIDTH_A, WIDTH_A, wide, wide, wide)
    return pl.pallas_call(
        _proj_kernel,
        grid=(n // tm,),
        in_specs=in_specs,
        out_specs=[pl.BlockSpec((tm, w), row) for w in widths],
        out_shape=[jax.ShapeDtypeStruct((n, w), BF16) for w in widths],
        compiler_params=pltpu.CompilerParams(
            dimension_semantics=("parallel",), vmem_limit_bytes=VMEM_LIMIT),
        name="proj",
    )(x2, pos2, *consts)


def _attn_a_kernel(q_ref, k_ref, v_ref, bias_ref, o_ref,
                   qf, kf, vf0, vf1, acc0, acc1, m0, m1, *, seq, pad):
    lane = lax.broadcasted_iota(jnp.int32, (1, LANES), 1)
    first_head = lane < HALF
    qf[...] = q_ref[...].astype(F32)
    zeros = jnp.zeros((pad, LANES), F32)
    kf[pl.ds(0, pad), :] = zeros
    vf0[pl.ds(0, pad), :] = zeros
    vf1[pl.ds(0, pad), :] = zeros
    kf[pl.ds(pad, seq), :] = k_ref[...].astype(F32)
    v = v_ref[...].astype(F32)
    vf0[pl.ds(pad, seq), :] = jnp.where(first_head, v, 1.0)
    vf1[pl.ds(pad, seq), :] = jnp.where(first_head, 1.0, v)
    for acc, m_s in ((acc0, m0), (acc1, m1)):
        acc[...] = jnp.zeros_like(acc)
        m_s[...] = jnp.full(m_s.shape, NEG, F32)
    heads = ((first_head, vf0, acc0, m0), (jnp.logical_not(first_head), vf1, acc1, m1))

    for bi, (window, dil) in enumerate(DILATED_BRANCHES):
        assert window // dil == WIN
        nblk = seq // (dil * WIN)
        unroll = min(UNROLL_A, nblk)

        def rows(start, size, dil=dil):
            return pl.ds(start, size, stride=dil) if dil > 1 else pl.ds(start, size)

        def group(r, g, bi=bi, dil=dil, unroll=unroll, rows=rows):
            q0 = r + dil * WIN * unroll * g
            k0 = pad + q0 - dil * WIN
            krows = rows(k0, (unroll + 1) * WIN)
            kk = kf[krows, :].astype(BF16)
            vvs = [vf[krows, :].astype(BF16) for vf in (vf0, vf1)]
            for u in range(unroll):
                qrows = rows(q0 + u * dil * WIN, WIN)
                q = qf[qrows, :]
                ksl = slice(u * WIN, (u + 2) * WIN)
                first_blk = jnp.where(g == 0, 1, 0) if u == 0 else 0
                for hh, (mine, _, acc, m_s) in enumerate(heads):
                    qh = jnp.where(mine, q, 0.0).astype(BF16)
                    s = _nt_dot(qh, kk[ksl]) + bias_ref[bi, first_blk, hh]
                    m_old = m_s[qrows, :]
                    m_new = jnp.maximum(m_old, jnp.max(s, axis=-1, keepdims=True))
                    alpha = jnp.exp2(m_old - m_new)
                    p = jnp.exp2(s - jnp.concatenate([m_new, m_new], axis=1)).astype(BF16)
                    acc[qrows, :] = alpha * acc[qrows, :] + _dot(p, vvs[hh][ksl])
                    m_s[qrows, :] = m_new

        def class_body(r, carry, ngrp=nblk // unroll, group=group):
            def grp_body(g, c):
                group(r, g)
                return c
            return lax.fori_loop(0, ngrp, grp_body, carry)

        lax.fori_loop(0, dil, class_body, 0)

    a0 = acc0[...]
    a1 = acc1[...]
    o_ref[...] = jnp.where(first_head, a0 / pltpu.roll(a0, HALF, 1),
                           a1 / pltpu.roll(a1, HALF, 1)).astype(o_ref.dtype)


def _attn_a_call(qa, ka, va, bias):
    b, s, _ = qa.shape
    pad = DILATED_BRANCHES[-1][0]
    blk = pl.BlockSpec((None, s, LANES), lambda bb, p: (bb, 0, p))
    bias_spec = pl.BlockSpec((None,) + bias.shape[1:], lambda bb, p: (p, 0, 0, 0, 0, 0))
    rows = pltpu.VMEM((s, LANES), F32)
    padded = pltpu.VMEM((pad + s, LANES), F32)
    return pl.pallas_call(
        functools.partial(_attn_a_kernel, seq=s, pad=pad),
        grid=(b, HEAD_PAIRS),
        in_specs=[blk, blk, blk, bias_spec],
        out_specs=blk,
        out_shape=jax.ShapeDtypeStruct((b, s, WIDTH_A), BF16),
        scratch_shapes=[rows, padded, padded, padded, rows, rows, rows, rows],
        compiler_params=pltpu.CompilerParams(
            dimension_semantics=("parallel", "parallel"), vmem_limit_bytes=VMEM_LIMIT),
        name="attn_a",
    )(qa, ka, va, bias)


def _t5_causal_bucket(dist):
    dist = np.asarray(dist, dtype=np.int64)
    max_exact = REL_BUCKETS // 2
    safe = np.maximum(dist, 1).astype(np.float32)
    large = max_exact + (np.log(safe / max_exact) / math.log(REL_MAX_DIST / max_exact)
                         * (REL_BUCKETS - max_exact)).astype(np.int64)
    large = np.minimum(large, REL_BUCKETS - 1)
    return np.where(dist < max_exact, dist, large).astype(np.int32)


def _attn_a_bias(rel_bias):
    span = 3 * WIN
    u = np.arange(span) - (WIN - 1)
    valid = (u >= 0) & (u <= WIN)
    tables = []
    for _, dil in DILATED_BRANCHES:
        bucket = _t5_causal_bucket(np.clip(WIN - u, 0, WIN) * dil)
        onehot = jnp.asarray(np.eye(REL_BUCKETS, dtype=np.float32)[bucket])
        vec = jnp.dot(onehot, rel_bias.astype(F32), precision=lax.Precision.HIGHEST)
        vec = jnp.where(jnp.asarray(valid)[:, None], vec * LOG2E, NEG).T
        flat = jnp.tile(vec, (1, WIN))[:, WIN - 1:WIN - 1 + WIN * (span - 1)]
        full = flat.reshape(N_HEADS_A, WIN, span - 1)[:, :, :2 * WIN]
        first = jnp.where(jnp.asarray(np.arange(2 * WIN) >= WIN)[None, None, :], full, NEG)
        tables.append(jnp.stack([full, first], axis=0))
    t = jnp.stack(tables, axis=0)
    t = t.reshape(len(DILATED_BRANCHES), 2, HEAD_PAIRS, 2, WIN, 2 * WIN)
    return jnp.transpose(t, (2, 0, 1, 3, 4, 5))


def _attn_b_kernel(q_ref, k_ref, v_ref, o_ref, acc0, acc1, m0, m1):
    qi = pl.program_id(2)
    tq = TQ_B
    heads = ((acc0, m0), (acc1, m1))
    for acc, m_s in heads:
        acc[...] = jnp.zeros_like(acc)
        m_s[...] = jnp.full(m_s.shape, NEG, F32)

    def step(kb, masked):
        krows = pl.ds(pl.multiple_of(kb * tq, tq), tq)
        for hh, (acc, m_s) in enumerate(heads):
            sl = slice(hh * LANES, (hh + 1) * LANES)
            s = _nt_dot(q_ref[:, sl], k_ref[krows, sl])
            if masked:
                row = lax.broadcasted_iota(jnp.int32, (tq, tq), 0)
                col = lax.broadcasted_iota(jnp.int32, (tq, tq), 1)
                s = jnp.where(col <= row, s, NEG)
            m_old = m_s[...]
            m_new = jnp.maximum(m_old, jnp.max(s, axis=-1, keepdims=True))
            alpha = jnp.exp2(m_old - m_new)
            p = jnp.exp2(s - jnp.concatenate([m_new] * (tq // LANES), axis=1)).astype(BF16)
            acc[...] = alpha * acc[...] + _dot(p, v_ref[krows, sl])
            m_s[...] = m_new

    def body(kb, c):
        step(kb, False)
        return c

    lax.fori_loop(0, qi, body, 0)
    step(qi, True)
    lane = lax.broadcasted_iota(jnp.int32, (1, LANES), 1)
    a0 = acc0[...]
    a1 = acc1[...]
    o_ref[...] = jnp.where(lane < HALF, a0 / pltpu.roll(a0, HALF, 1),
                           pltpu.roll(a1, HALF, 1) / a1).astype(o_ref.dtype)


def _attn_b_call(qb, kb, vb):
    b, s, _ = qb.shape
    tq = TQ_B
    return pl.pallas_call(
        _attn_b_kernel,
        grid=(b, HEAD_PAIRS, s // tq),
        in_specs=[pl.BlockSpec((None, tq, 2 * LANES), lambda bb, p, i: (bb, i, p)),
                  pl.BlockSpec((None, s, 2 * LANES), lambda bb, p, i: (bb, 0, p)),
                  pl.BlockSpec((None, s, 2 * LANES), lambda bb, p, i: (bb, 0, p))],
        out_specs=pl.BlockSpec((None, tq, LANES), lambda bb, p, i: (bb, i, p)),
        out_shape=jax.ShapeDtypeStruct((b, s, WIDTH_B), BF16),
        scratch_shapes=[pltpu.VMEM((tq, LANES), F32)] * 4,
        compiler_params=pltpu.CompilerParams(
            dimension_semantics=("parallel", "parallel", "arbitrary"),
            vmem_limit_bytes=VMEM_LIMIT),
        name="attn_b",
    )(qb, kb, vb)


def _ffn_kernel(x_ref, a_ref, b_ref, woa_ref, wob_ref, g_ref, w1_ref, w2_ref, o_ref):
    h = x_ref[...] + _dot(a_ref[...], woa_ref[...]) + _dot(b_ref[...], wob_ref[...])
    hn = (h * lax.rsqrt(jnp.mean(h * h, axis=-1, keepdims=True) + EPS) * g_ref[...]).astype(BF16)
    mlp = None
    for c in range(D_FF // FF_CHUNK):
        sl = slice(c * FF_CHUNK, (c + 1) * FF_CHUNK)
        hid = jnp.square(jnp.maximum(_dot(hn, w1_ref[:, sl]), 0.0)).astype(BF16)
        d = _dot(hid, w2_ref[sl, :])
        mlp = d if mlp is None else mlp + d
    o_ref[...] = h + mlp


def _ffn_call(x2, a2, b2, woa, wob, g, w1, w2):
    n = x2.shape[0]
    tm = TM_FFN
    row = lambda i: (i, 0)
    consts = (woa, wob, g, w1, w2)
    return pl.pallas_call(
        _ffn_kernel,
        grid=(n // tm,),
        in_specs=[pl.BlockSpec((tm, D_MODEL), row), pl.BlockSpec((tm, WIDTH_A), row),
                  pl.BlockSpec((tm, WIDTH_B), row)] + [_const_spec(c.shape) for c in consts],
        out_specs=pl.BlockSpec((tm, D_MODEL), row),
        out_shape=jax.ShapeDtypeStruct((n, D_MODEL), F32),
        compiler_params=pltpu.CompilerParams(
            dimension_semantics=("parallel",), vmem_limit_bytes=VMEM_LIMIT),
        name="ffn",
    )(x2, a2, b2, *consts)


def _pad_heads(w, n_heads, width):
    k = w.shape[0]
    w = w.reshape(k, n_heads, width)
    return jnp.pad(w, ((0, 0), (0, 0), (0, LANES - width))).reshape(k, n_heads * LANES)


def _lane_row(g, offset=0):
    return jnp.pad(g.astype(F32), (offset, LANES - offset - g.shape[0]))[None, :]


def kernel(x, positions, norm_mix_g, w_in, qnorm_a_g, knorm_a_g, rel_bias, cq_norm_g, ckv_norm_g,
           w_uq, w_ukv, qnorm_b_g, knorm_b_g, w_o, norm_ffn_g, w_ff1, w_ff2):
    b, s, d = x.shape
    n = b * s
    c0 = 3 * WIDTH_A
    c1 = c0 + Q_LORA_RANK
    c2 = c1 + KV_LORA_RANK
    w_qkv = w_in[:, :c0].astype(BF16)
    w_cq = w_in[:, c0:c1].astype(BF16)
    w_ckv = w_in[:, c1:c2].astype(BF16)
    w_kr = jnp.pad(w_in[:, c2:], ((0, 0), (QK_NOPE_DIM, LANES - QK_HEAD_DIM_B))).astype(BF16)
    w_uq_p = _pad_heads(w_uq, N_HEADS_B, QK_HEAD_DIM_B).astype(BF16)
    w_ukv3 = w_ukv.reshape(KV_LORA_RANK, N_HEADS_B, QK_NOPE_DIM + V_HEAD_DIM)
    w_uk_p = _pad_heads(w_ukv3[:, :, :QK_NOPE_DIM].reshape(KV_LORA_RANK, -1),
                        N_HEADS_B, QK_NOPE_DIM).astype(BF16)
    w_uv_p = _pad_heads(w_ukv3[:, :, QK_NOPE_DIM:].reshape(KV_LORA_RANK, -1),
                        N_HEADS_B, V_HEAD_DIM).astype(BF16)
    inv_freq = 1.0 / (ROPE_THETA ** (jnp.arange(0, QK_ROPE_DIM, 2, dtype=F32) / QK_ROPE_DIM))
    ones_row = jnp.tile(_lane_row(jnp.ones((HALF,), F32), HALF), (1, N_HEADS_B))
    consts = (
        norm_mix_g.astype(F32)[None, :], w_qkv, w_cq, w_ckv, w_kr,
        jnp.tile(qnorm_a_g.astype(F32), 2)[None, :], jnp.tile(knorm_a_g.astype(F32), 2)[None, :],
        cq_norm_g.astype(F32)[None, :], ckv_norm_g.astype(F32)[None, :],
        w_uq_p, w_uk_p, w_uv_p,
        _lane_row(qnorm_b_g), _lane_row(knorm_b_g[:QK_NOPE_DIM]),
        _lane_row(knorm_b_g[QK_NOPE_DIM:], QK_NOPE_DIM),
        _lane_row(jnp.tile(inv_freq, 2), QK_NOPE_DIM),
        ones_row,
    )
    qa, ka, va, qb, kb, vb = _proj_call(x.reshape(n, d), positions.reshape(n, 1), consts)

    out_a = _attn_a_call(qa.reshape(b, s, -1), ka.reshape(b, s, -1), va.reshape(b, s, -1),
                         _attn_a_bias(rel_bias))
    out_b = _attn_b_call(qb.reshape(b, s, -1), kb.reshape(b, s, -1), vb.reshape(b, s, -1))

    out = _ffn_call(x.reshape(n, d), out_a.reshape(n, -1), out_b.reshape(n, -1),
                    w_o[:WIDTH_A].astype(BF16), w_o[WIDTH_A:].astype(BF16),
                    norm_ffn_g.astype(F32)[None, :], w_ff1.astype(BF16), w_ff2.astype(BF16))
    return out.reshape(b, s, d)
```

```python
import functools
import math

import numpy as np
import jax
import jax.numpy as jnp
from jax import lax
from jax.experimental import pallas as pl
from jax.experimental.pallas import tpu as pltpu

F32 = jnp.float32
BF16 = jnp.bfloat16

D_MODEL = 1024
HEAD_DIM_A = 64
N_HEADS_A = 8
WIDTH_A = N_HEADS_A * HEAD_DIM_A
DILATED_BRANCHES = ((128, 1), (512, 4), (2048, 16))
N_HEADS_B = 8
QK_NOPE_DIM = 64
QK_ROPE_DIM = 32
V_HEAD_DIM = 64
QK_HEAD_DIM_B = QK_NOPE_DIM + QK_ROPE_DIM
Q_LORA_RANK = 768
KV_LORA_RANK = 256
WIDTH_B = N_HEADS_B * V_HEAD_DIM
ROPE_THETA = 10000.0
D_FF = 4 * D_MODEL
REL_BUCKETS = 32
REL_MAX_DIST = 2048
EPS = 1e-6

LANES = 128
HALF = LANES // 2
HEAD_PAIRS = N_HEADS_A // 2
WIN = 128
NEG = -1e30
VMEM_LIMIT = 56 * 1024 * 1024

TM_PROJ = 512
TM_FFN = 512
TQ_B = 512
TSUB_B = 256
FF_CHUNK = 1024
UNROLL_A = 4

LOG2E = math.log2(math.e)
SCALE_A = LOG2E / math.sqrt(HEAD_DIM_A)
SCALE_B = LOG2E / math.sqrt(QK_HEAD_DIM_B)

VT_ROWS = V_HEAD_DIM + 16

assert HEAD_DIM_A == HALF and V_HEAD_DIM == HALF
assert TM_PROJ == TQ_B


def _nt_dot(a, b):
    return lax.dot_general(a, b, (((1,), (1,)), ((), ())), preferred_element_type=F32)


def _dot(a, b):
    return jnp.dot(a, b, preferred_element_type=F32)


def _const_spec(shape):
    nd = len(shape)
    return pl.BlockSpec(shape, lambda *_: (0,) * nd, pipeline_mode=pl.Buffered(1))


def _proj_kernel(x_ref, pos_ref, gmix_ref, wqkv_ref, wcq_ref, wckv_ref, wkr_ref,
                 gqa_ref, gka_ref, gcq_ref, gckv_ref, wuq_ref, wuk_ref, wuv_ref,
                 gqb_ref, gkbn_ref, gkbr_ref, invf_ref, ones_ref,
                 qa_ref, ka_ref, va_ref, qb_ref, kb_ref, vb_ref):
    x = x_ref[...]
    ms = jnp.mean(x * x, axis=-1, keepdims=True)
    xn = (x * lax.rsqrt(ms + EPS) * gmix_ref[...]).astype(BF16)

    lane = lax.broadcasted_iota(jnp.int32, (1, LANES), 1)
    first_head = lane < HALF

    def pair_norm(x2, g_row):
        sq = x2 * x2
        ss0 = jnp.sum(jnp.where(first_head, sq, 0.0), axis=-1, keepdims=True)
        ss1 = jnp.sum(jnp.where(first_head, 0.0, sq), axis=-1, keepdims=True)
        r = jnp.where(first_head,
                      lax.rsqrt(ss0 / HEAD_DIM_A + EPS),
                      lax.rsqrt(ss1 / HEAD_DIM_A + EPS))
        return x2 * r * g_row

    qkv = _dot(xn, wqkv_ref[...])
    for p in range(HEAD_PAIRS):
        sl = slice(p * LANES, (p + 1) * LANES)
        qa_ref[:, sl] = (pair_norm(qkv[:, sl], gqa_ref[...]) * SCALE_A).astype(BF16)
        ksl = slice(WIDTH_A + p * LANES, WIDTH_A + (p + 1) * LANES)
        ka_ref[:, sl] = pair_norm(qkv[:, ksl], gka_ref[...]).astype(BF16)
    va_ref[...] = qkv[:, 2 * WIDTH_A:].astype(BF16)

    cq = _dot(xn, wcq_ref[...])
    cqn = (cq * lax.rsqrt(jnp.mean(cq * cq, axis=-1, keepdims=True) + EPS)
           * gcq_ref[...]).astype(BF16)
    qb = _dot(cqn, wuq_ref[...])
    ckv = _dot(xn, wckv_ref[...])
    ckvn = (ckv * lax.rsqrt(jnp.mean(ckv * ckv, axis=-1, keepdims=True) + EPS)
            * gckv_ref[...]).astype(BF16)
    kn = _dot(ckvn, wuk_ref[...])
    vb_ref[...] = (_nt_dot(wuv_ref[...], ckvn) + ones_ref[...]).astype(BF16)
    kr = _dot(xn, wkr_ref[...])

    ang = pos_ref[...].astype(F32) * invf_ref[...]
    cs = jnp.cos(ang)
    sn = jnp.sin(ang)
    half = QK_ROPE_DIM // 2
    lo = (lane >= QK_NOPE_DIM) & (lane < QK_NOPE_DIM + half)
    hi = (lane >= QK_NOPE_DIM + half) & (lane < QK_HEAD_DIM_B)
    s_lo = jnp.where(lo, -sn, 0.0)
    s_hi = jnp.where(hi, sn, 0.0)

    def rope(y):
        return (y * cs + pltpu.roll(y, LANES - half, 1) * s_lo
                + pltpu.roll(y, half, 1) * s_hi)

    ss_kr = jnp.sum(kr * kr, axis=-1, keepdims=True)
    kr_rot = rope(kr * gkbr_ref[...])
    for h in range(N_HEADS_B):
        sl = slice(h * LANES, (h + 1) * LANES)
        qh = qb[:, sl]
        rq = lax.rsqrt(jnp.sum(qh * qh, axis=-1, keepdims=True) / QK_HEAD_DIM_B + EPS)
        qb_ref[:, sl] = (rope(qh * rq * gqb_ref[...]) * SCALE_B).astype(BF16)
        kh = kn[:, sl]
        rk = lax.rsqrt((jnp.sum(kh * kh, axis=-1, keepdims=True) + ss_kr) / QK_HEAD_DIM_B + EPS)
        kb_ref[:, sl] = ((kh * gkbn_ref[...] + kr_rot) * rk).astype(BF16)


def _proj_call(x2, pos2, consts):
    n = x2.shape[0]
    tm = TM_PROJ
    row = lambda i: (i, 0)
    in_specs = [pl.BlockSpec((tm, D_MODEL), row), pl.BlockSpec((tm, 1), row)]
    in_specs += [_const_spec(c.shape) for c in consts]
    wide = N_HEADS_B * LANES
    widths = (WIDTH_A, WIDTH_A, WIDTH_A, wide, wide)
    vt_rows = N_HEADS_B * VT_ROWS
    return pl.pallas_call(
        _proj_kernel,
        grid=(n // tm,),
        in_specs=in_specs,
        out_specs=[pl.BlockSpec((tm, w), row) for w in widths]
        + [pl.BlockSpec((None, vt_rows, tm), lambda i: (i, 0, 0))],
        out_shape=[jax.ShapeDtypeStruct((n, w), BF16) for w in widths]
        + [jax.ShapeDtypeStruct((n // tm, vt_rows, tm), BF16)],
        compiler_params=pltpu.CompilerParams(
            dimension_semantics=("parallel",), vmem_limit_bytes=VMEM_LIMIT),
        name="proj",
    )(x2, pos2, *consts)


def _attn_a_kernel(q_ref, k_ref, v_ref, bias_ref, o_ref,
                   qf, kf, vf0, vf1, acc0, acc1, m0, m1, *, seq, pad):
    lane = lax.broadcasted_iota(jnp.int32, (1, LANES), 1)
    first_head = lane < HALF
    qf[...] = q_ref[...].astype(F32)
    zeros = jnp.zeros((pad, LANES), F32)
    kf[pl.ds(0, pad), :] = zeros
    vf0[pl.ds(0, pad), :] = zeros
    vf1[pl.ds(0, pad), :] = zeros
    kf[pl.ds(pad, seq), :] = k_ref[...].astype(F32)
    v = v_ref[...].astype(F32)
    vf0[pl.ds(pad, seq), :] = jnp.where(first_head, v, 1.0)
    vf1[pl.ds(pad, seq), :] = jnp.where(first_head, 1.0, v)
    for acc, m_s in ((acc0, m0), (acc1, m1)):
        acc[...] = jnp.zeros_like(acc)
        m_s[...] = jnp.full(m_s.shape, NEG, F32)
    heads = ((first_head, vf0, acc0, m0), (jnp.logical_not(first_head), vf1, acc1, m1))

    for bi, (window, dil) in enumerate(DILATED_BRANCHES):
        assert window // dil == WIN
        nblk = seq // (dil * WIN)
        unroll = min(UNROLL_A, nblk)

        def rows(start, size, dil=dil):
            return pl.ds(start, size, stride=dil) if dil > 1 else pl.ds(start, size)

        def group(r, g, bi=bi, dil=dil, unroll=unroll, rows=rows):
            q0 = r + dil * WIN * unroll * g
            k0 = pad + q0 - dil * WIN
            krows = rows(k0, (unroll + 1) * WIN)
            kk = kf[krows, :].astype(BF16)
            vvs = [vf[krows, :].astype(BF16) for vf in (vf0, vf1)]
            for u in range(unroll):
                qrows = rows(q0 + u * dil * WIN, WIN)
                q = qf[qrows, :]
                ksl = slice(u * WIN, (u + 2) * WIN)
                first_blk = jnp.where(g == 0, 1, 0) if u == 0 else 0
                for hh, (mine, _, acc, m_s) in enumerate(heads):
                    qh = jnp.where(mine, q, 0.0).astype(BF16)
                    s = _nt_dot(qh, kk[ksl]) + bias_ref[bi, first_blk, hh]
                    m_old = m_s[qrows, :]
                    m_new = jnp.maximum(m_old, jnp.max(s, axis=-1, keepdims=True))
                    alpha = jnp.exp2(m_old - m_new)
                    p = jnp.exp2(s - jnp.concatenate([m_new, m_new], axis=1)).astype(BF16)
                    acc[qrows, :] = alpha * acc[qrows, :] + _dot(p, vvs[hh][ksl])
                    m_s[qrows, :] = m_new

        def class_body(r, carry, ngrp=nblk // unroll, group=group):
            def grp_body(g, c):
                group(r, g)
                return c
            return lax.fori_loop(0, ngrp, grp_body, carry)

        lax.fori_loop(0, dil, class_body, 0)

    a0 = acc0[...]
    a1 = acc1[...]
    o_ref[...] = jnp.where(first_head, a0 / pltpu.roll(a0, HALF, 1),
                           a1 / pltpu.roll(a1, HALF, 1)).astype(o_ref.dtype)


def _attn_a_call(qa, ka, va, bias):
    b, s, _ = qa.shape
    pad = DILATED_BRANCHES[-1][0]
    blk = pl.BlockSpec((None, s, LANES), lambda bb, p: (bb, 0, p))
    bias_spec = pl.BlockSpec((None,) + bias.shape[1:], lambda bb, p: (p, 0, 0, 0, 0, 0))
    rows = pltpu.VMEM((s, LANES), F32)
    padded = pltpu.VMEM((pad + s, LANES), F32)
    return pl.pallas_call(
        functools.partial(_attn_a_kernel, seq=s, pad=pad),
        grid=(b, HEAD_PAIRS),
        in_specs=[blk, blk, blk, bias_spec],
        out_specs=blk,
        out_shape=jax.ShapeDtypeStruct((b, s, WIDTH_A), BF16),
        scratch_shapes=[rows, padded, padded, padded, rows, rows, rows, rows],
        compiler_params=pltpu.CompilerParams(
            dimension_semantics=("parallel", "parallel"), vmem_limit_bytes=VMEM_LIMIT),
        name="attn_a",
    )(qa, ka, va, bias)


def _t5_causal_bucket(dist):
    dist = np.asarray(dist, dtype=np.int64)
    max_exact = REL_BUCKETS // 2
    safe = np.maximum(dist, 1).astype(np.float32)
    large = max_exact + (np.log(safe / max_exact) / math.log(REL_MAX_DIST / max_exact)
                         * (REL_BUCKETS - max_exact)).astype(np.int64)
    large = np.minimum(large, REL_BUCKETS - 1)
    return np.where(dist < max_exact, dist, large).astype(np.int32)


def _attn_a_bias(rel_bias):
    span = 3 * WIN
    u = np.arange(span) - (WIN - 1)
    valid = (u >= 0) & (u <= WIN)
    tables = []
    for _, dil in DILATED_BRANCHES:
        bucket = _t5_causal_bucket(np.clip(WIN - u, 0, WIN) * dil)
        onehot = jnp.asarray(np.eye(REL_BUCKETS, dtype=np.float32)[bucket])
        vec = jnp.dot(onehot, rel_bias.astype(F32), precision=lax.Precision.HIGHEST)
        vec = jnp.where(jnp.asarray(valid)[:, None], vec * LOG2E, NEG).T
        flat = jnp.tile(vec, (1, WIN))[:, WIN - 1:WIN - 1 + WIN * (span - 1)]
        full = flat.reshape(N_HEADS_A, WIN, span - 1)[:, :, :2 * WIN]
        first = jnp.where(jnp.asarray(np.arange(2 * WIN) >= WIN)[None, None, :], full, NEG)
        tables.append(jnp.stack([full, first], axis=0))
    t = jnp.stack(tables, axis=0)
    t = t.reshape(len(DILATED_BRANCHES), 2, HEAD_PAIRS, 2, WIN, 2 * WIN)
    return jnp.transpose(t, (2, 0, 1, 3, 4, 5))


def _attn_b_kernel(q_ref, k_ref, vt_ref, o_ref, acc0, acc1, m0, m1, s_even, s_odd, mb_even, mb_odd):
    qi = pl.program_id(2)
    tq = TQ_B
    heads = ((acc0, m0), (acc1, m1))
    for acc, m_s in heads:
        acc[...] = jnp.zeros_like(acc)
        m_s[...] = jnp.full(m_s.shape, NEG, F32)

    def scores(kb, s_buf, mb_buf):
        krows = pl.ds(pl.multiple_of(kb * tq, tq), tq)
        for hh in range(2):
            sl = slice(hh * LANES, (hh + 1) * LANES)
            st = _nt_dot(k_ref[krows, sl], q_ref[:, sl])
            s_buf[hh] = st
            mb_buf[hh] = jnp.max(st, axis=0, keepdims=True)

    def update(kb, s_buf, mb_buf, masked):
        for hh, (acc, m_s) in enumerate(heads):
            st = s_buf[hh]
            if masked:
                key = lax.broadcasted_iota(jnp.int32, (tq, tq), 0)
                qry = lax.broadcasted_iota(jnp.int32, (tq, tq), 1)
                st = jnp.where(key <= qry, st, NEG)
                m_blk = jnp.max(st, axis=0, keepdims=True)
            else:
                m_blk = mb_buf[hh]
            m_old = m_s[...]
            m_new = jnp.maximum(m_old, m_blk)
            alpha = jnp.exp2(m_old - m_new)
            pt = jnp.exp2(st - m_new).astype(BF16)
            acc[...] = alpha * acc[...] + _dot(vt_ref[kb, hh], pt)
            m_s[...] = m_new

    scores(0, s_even, mb_even)

    def pair_body(i, c):
        kb = 2 * i
        scores(kb + 1, s_odd, mb_odd)
        update(kb, s_even, mb_even, False)
        scores(kb + 2, s_even, mb_even)
        update(kb + 1, s_odd, mb_odd, False)
        return c

    npairs = qi // 2
    lax.fori_loop(0, npairs, pair_body, 0)

    @pl.when(qi % 2 == 1)
    def _():
        scores(qi, s_odd, mb_odd)
        update(qi - 1, s_even, mb_even, False)
        update(qi, s_odd, mb_odd, True)

    @pl.when(qi % 2 == 0)
    def _():
        update(qi, s_even, mb_even, True)

    outs = []
    for acc, _ in heads:
        a = acc[...]
        outs.append(a[:V_HEAD_DIM] / a[V_HEAD_DIM:V_HEAD_DIM + 1])
    o_ref[...] = jnp.concatenate(outs, axis=0).T.astype(o_ref.dtype)


def _attn_b_call(qb, kb, vbt):
    b, s, _ = qb.shape
    tq = TQ_B
    nkb = s // tq
    vbt = vbt.reshape(b, nkb, HEAD_PAIRS, 2, VT_ROWS, tq)
    return pl.pallas_call(
        _attn_b_kernel,
        grid=(b, HEAD_PAIRS, nkb),
        in_specs=[pl.BlockSpec((None, tq, 2 * LANES), lambda bb, p, i: (bb, i, p)),
                  pl.BlockSpec((None, s, 2 * LANES), lambda bb, p, i: (bb, 0, p)),
                  pl.BlockSpec((None, nkb, None, 2, VT_ROWS, tq),
                               lambda bb, p, i: (bb, 0, p, 0, 0, 0))],
        out_specs=pl.BlockSpec((None, tq, LANES), lambda bb, p, i: (bb, i, p)),
        out_shape=jax.ShapeDtypeStruct((b, s, WIDTH_B), BF16),
        scratch_shapes=[pltpu.VMEM((VT_ROWS, tq), F32)] * 2 + [pltpu.VMEM((1, tq), F32)] * 2
        + [pltpu.VMEM((2, tq, tq), F32)] * 2 + [pltpu.VMEM((2, 1, tq), F32)] * 2,
        compiler_params=pltpu.CompilerParams(
            dimension_semantics=("parallel", "parallel", "arbitrary"),
            vmem_limit_bytes=VMEM_LIMIT),
        name="attn_b",
    )(qb, kb, vbt)


def _ffn_kernel(x_ref, a_ref, b_ref, woa_ref, wob_ref, g_ref, w1_ref, w2_ref, o_ref):
    h = x_ref[...] + _dot(a_ref[...], woa_ref[...]) + _dot(b_ref[...], wob_ref[...])
    hn = (h * lax.rsqrt(jnp.mean(h * h, axis=-1, keepdims=True) + EPS) * g_ref[...]).astype(BF16)
    mlp = None
    for c in range(D_FF // FF_CHUNK):
        sl = slice(c * FF_CHUNK, (c + 1) * FF_CHUNK)
        hid = jnp.square(jnp.maximum(_dot(hn, w1_ref[:, sl]), 0.0)).astype(BF16)
        d = _dot(hid, w2_ref[sl, :])
        mlp = d if mlp is None else mlp + d
    o_ref[...] = h + mlp


def _ffn_call(x2, a2, b2, woa, wob, g, w1, w2):
    n = x2.shape[0]
    tm = TM_FFN
    row = lambda i: (i, 0)
    consts = (woa, wob, g, w1, w2)
    return pl.pallas_call(
        _ffn_kernel,
        grid=(n // tm,),
        in_specs=[pl.BlockSpec((tm, D_MODEL), row), pl.BlockSpec((tm, WIDTH_A), row),
                  pl.BlockSpec((tm, WIDTH_B), row)] + [_const_spec(c.shape) for c in consts],
        out_specs=pl.BlockSpec((tm, D_MODEL), row),
        out_shape=jax.ShapeDtypeStruct((n, D_MODEL), F32),
        compiler_params=pltpu.CompilerParams(
            dimension_semantics=("parallel",), vmem_limit_bytes=VMEM_LIMIT),
        name="ffn",
    )(x2, a2, b2, *consts)


def _pad_heads(w, n_heads, width):
    k = w.shape[0]
    w = w.reshape(k, n_heads, width)
    return jnp.pad(w, ((0, 0), (0, 0), (0, LANES - width))).reshape(k, n_heads * LANES)


def _lane_row(g, offset=0):
    return jnp.pad(g.astype(F32), (offset, LANES - offset - g.shape[0]))[None, :]


def kernel(x, positions, norm_mix_g, w_in, qnorm_a_g, knorm_a_g, rel_bias, cq_norm_g, ckv_norm_g,
           w_uq, w_ukv, qnorm_b_g, knorm_b_g, w_o, norm_ffn_g, w_ff1, w_ff2):
    b, s, d = x.shape
    n = b * s
    c0 = 3 * WIDTH_A
    c1 = c0 + Q_LORA_RANK
    c2 = c1 + KV_LORA_RANK
    w_qkv = w_in[:, :c0].astype(BF16)
    w_cq = w_in[:, c0:c1].astype(BF16)
    w_ckv = w_in[:, c1:c2].astype(BF16)
    w_kr = jnp.pad(w_in[:, c2:], ((0, 0), (QK_NOPE_DIM, LANES - QK_HEAD_DIM_B))).astype(BF16)
    w_uq_p = _pad_heads(w_uq, N_HEADS_B, QK_HEAD_DIM_B).astype(BF16)
    w_ukv3 = w_ukv.reshape(KV_LORA_RANK, N_HEADS_B, QK_NOPE_DIM + V_HEAD_DIM)
    w_uk_p = _pad_heads(w_ukv3[:, :, :QK_NOPE_DIM].reshape(KV_LORA_RANK, -1),
                        N_HEADS_B, QK_NOPE_DIM).astype(BF16)
    ones_rows = VT_ROWS - V_HEAD_DIM
    w_uvt = jnp.transpose(w_ukv3[:, :, QK_NOPE_DIM:], (1, 2, 0))
    w_uvt = jnp.pad(w_uvt, ((0, 0), (0, ones_rows), (0, 0)))
    w_uvt = w_uvt.reshape(N_HEADS_B * VT_ROWS, KV_LORA_RANK).astype(BF16)
    inv_freq = 1.0 / (ROPE_THETA ** (jnp.arange(0, QK_ROPE_DIM, 2, dtype=F32) / QK_ROPE_DIM))
    ones_col = jnp.tile(jnp.concatenate([jnp.zeros((V_HEAD_DIM,), F32), jnp.ones((ones_rows,), F32)]),
                        N_HEADS_B)[:, None]
    consts = (
        norm_mix_g.astype(F32)[None, :], w_qkv, w_cq, w_ckv, w_kr,
        jnp.tile(qnorm_a_g.astype(F32), 2)[None, :], jnp.tile(knorm_a_g.astype(F32), 2)[None, :],
        cq_norm_g.astype(F32)[None, :], ckv_norm_g.astype(F32)[None, :],
        w_uq_p, w_uk_p, w_uvt,
        _lane_row(qnorm_b_g), _lane_row(knorm_b_g[:QK_NOPE_DIM]),
        _lane_row(knorm_b_g[QK_NOPE_DIM:], QK_NOPE_DIM),
        _lane_row(jnp.tile(inv_freq, 2), QK_NOPE_DIM),
        ones_col,
    )
    qa, ka, va, qb, kb, vbt = _proj_call(x.reshape(n, d), positions.reshape(n, 1), consts)

    out_a = _attn_a_call(qa.reshape(b, s, -1), ka.reshape(b, s, -1), va.reshape(b, s, -1),
                         _attn_a_bias(rel_bias))
    out_b = _attn_b_call(qb.reshape(b, s, -1), kb.reshape(b, s, -1), vbt)

    out = _ffn_call(x.reshape(n, d), out_a.reshape(n, -1), out_b.reshape(n, -1),
                    w_o[:WIDTH_A].astype(BF16), w_o[WIDTH_A:].astype(BF16),
                    norm_ffn_g.astype(F32)[None, :], w_ff1.astype(BF16), w_ff2.astype(BF16))
    return out.reshape(b, s, d)
```

```python
import functools
import math

import numpy as np
import jax
import jax.numpy as jnp
from jax import lax
from jax.experimental import pallas as pl
from jax.experimental.pallas import tpu as pltpu

F32 = jnp.float32
BF16 = jnp.bfloat16

D_MODEL = 1024
HEAD_DIM_A = 64
N_HEADS_A = 8
WIDTH_A = N_HEADS_A * HEAD_DIM_A
DILATED_BRANCHES = ((128, 1), (512, 4), (2048, 16))
N_HEADS_B = 8
QK_NOPE_DIM = 64
QK_ROPE_DIM = 32
V_HEAD_DIM = 64
QK_HEAD_DIM_B = QK_NOPE_DIM + QK_ROPE_DIM
Q_LORA_RANK = 768
KV_LORA_RANK = 256
WIDTH_B = N_HEADS_B * V_HEAD_DIM
ROPE_THETA = 10000.0
D_FF = 4 * D_MODEL
REL_BUCKETS = 32
REL_MAX_DIST = 2048
EPS = 1e-6

LANES = 128
HALF = LANES // 2
HEAD_PAIRS = N_HEADS_A // 2
WIN = 128
NEG = -1e30
VMEM_LIMIT = 56 * 1024 * 1024

TM_PROJ = 512
TM_FFN = 512
TQ_B = 512
TSUB_B = 256
FF_CHUNK = 1024
GROUP_A = 8

LOG2E = math.log2(math.e)
SCALE_A = LOG2E / math.sqrt(HEAD_DIM_A)
SCALE_B = LOG2E / math.sqrt(QK_HEAD_DIM_B)

VT_ROWS = V_HEAD_DIM + 16

assert HEAD_DIM_A == HALF and V_HEAD_DIM == HALF
assert TM_PROJ == TQ_B


def _nt_dot(a, b):
    return lax.dot_general(a, b, (((1,), (1,)), ((), ())), preferred_element_type=F32)


def _dot(a, b):
    return jnp.dot(a, b, preferred_element_type=F32)


def _const_spec(shape):
    nd = len(shape)
    return pl.BlockSpec(shape, lambda *_: (0,) * nd, pipeline_mode=pl.Buffered(1))


def _proj_kernel(x_ref, pos_ref, gmix_ref, wqkv_ref, wcq_ref, wckv_ref, wkr_ref,
                 gqa_ref, gka_ref, gcq_ref, gckv_ref, wuq_ref, wuk_ref, wuv_ref,
                 gqb_ref, gkbn_ref, gkbr_ref, invf_ref, ones_ref,
                 qa_ref, ka_ref, va_ref, qb_ref, kb_ref, vb_ref):
    x = x_ref[...]
    ms = jnp.mean(x * x, axis=-1, keepdims=True)
    xn = (x * lax.rsqrt(ms + EPS) * gmix_ref[...]).astype(BF16)

    lane = lax.broadcasted_iota(jnp.int32, (1, LANES), 1)
    first_head = lane < HALF

    def pair_norm(x2, g_row):
        sq = x2 * x2
        ss0 = jnp.sum(jnp.where(first_head, sq, 0.0), axis=-1, keepdims=True)
        ss1 = jnp.sum(jnp.where(first_head, 0.0, sq), axis=-1, keepdims=True)
        r = jnp.where(first_head,
                      lax.rsqrt(ss0 / HEAD_DIM_A + EPS),
                      lax.rsqrt(ss1 / HEAD_DIM_A + EPS))
        return x2 * r * g_row

    qkv = _dot(xn, wqkv_ref[...])
    for p in range(HEAD_PAIRS):
        sl = slice(p * LANES, (p + 1) * LANES)
        qa_ref[:, sl] = (pair_norm(qkv[:, sl], gqa_ref[...]) * SCALE_A).astype(BF16)
        ksl = slice(WIDTH_A + p * LANES, WIDTH_A + (p + 1) * LANES)
        ka_ref[:, sl] = pair_norm(qkv[:, ksl], gka_ref[...]).astype(BF16)
    va_ref[...] = qkv[:, 2 * WIDTH_A:].astype(BF16)

    cq = _dot(xn, wcq_ref[...])
    cqn = (cq * lax.rsqrt(jnp.mean(cq * cq, axis=-1, keepdims=True) + EPS)
           * gcq_ref[...]).astype(BF16)
    qb = _dot(cqn, wuq_ref[...])
    ckv = _dot(xn, wckv_ref[...])
    ckvn = (ckv * lax.rsqrt(jnp.mean(ckv * ckv, axis=-1, keepdims=True) + EPS)
            * gckv_ref[...]).astype(BF16)
    kn = _dot(ckvn, wuk_ref[...])
    vb_ref[...] = (_nt_dot(wuv_ref[...], ckvn) + ones_ref[...]).astype(BF16)
    kr = _dot(xn, wkr_ref[...])

    ang = pos_ref[...].astype(F32) * invf_ref[...]
    cs = jnp.cos(ang)
    sn = jnp.sin(ang)
    half = QK_ROPE_DIM // 2
    lo = (lane >= QK_NOPE_DIM) & (lane < QK_NOPE_DIM + half)
    hi = (lane >= QK_NOPE_DIM + half) & (lane < QK_HEAD_DIM_B)
    s_lo = jnp.where(lo, -sn, 0.0)
    s_hi = jnp.where(hi, sn, 0.0)

    def rope(y):
        return (y * cs + pltpu.roll(y, LANES - half, 1) * s_lo
                + pltpu.roll(y, half, 1) * s_hi)

    ss_kr = jnp.sum(kr * kr, axis=-1, keepdims=True)
    kr_rot = rope(kr * gkbr_ref[...])
    for h in range(N_HEADS_B):
        sl = slice(h * LANES, (h + 1) * LANES)
        qh = qb[:, sl]
        rq = lax.rsqrt(jnp.sum(qh * qh, axis=-1, keepdims=True) / QK_HEAD_DIM_B + EPS)
        qb_ref[:, sl] = (rope(qh * rq * gqb_ref[...]) * SCALE_B).astype(BF16)
        kh = kn[:, sl]
        rk = lax.rsqrt((jnp.sum(kh * kh, axis=-1, keepdims=True) + ss_kr) / QK_HEAD_DIM_B + EPS)
        kb_ref[:, sl] = ((kh * gkbn_ref[...] + kr_rot) * rk).astype(BF16)


def _proj_call(x2, pos2, consts):
    n = x2.shape[0]
    tm = TM_PROJ
    row = lambda i: (i, 0)
    in_specs = [pl.BlockSpec((tm, D_MODEL), row), pl.BlockSpec((tm, 1), row)]
    in_specs += [_const_spec(c.shape) for c in consts]
    wide = N_HEADS_B * LANES
    widths = (WIDTH_A, WIDTH_A, WIDTH_A, wide, wide)
    vt_rows = N_HEADS_B * VT_ROWS
    return pl.pallas_call(
        _proj_kernel,
        grid=(n // tm,),
        in_specs=in_specs,
        out_specs=[pl.BlockSpec((tm, w), row) for w in widths]
        + [pl.BlockSpec((None, vt_rows, tm), lambda i: (i, 0, 0))],
        out_shape=[jax.ShapeDtypeStruct((n, w), BF16) for w in widths]
        + [jax.ShapeDtypeStruct((n // tm, vt_rows, tm), BF16)],
        compiler_params=pltpu.CompilerParams(
            dimension_semantics=("parallel",), vmem_limit_bytes=VMEM_LIMIT),
        name="proj",
    )(x2, pos2, *consts)


def _attn_a_kernel(q_ref, k_ref, v_ref, bias_ref, o_ref,
                   qf, kf, vf, acc, den, m0, m1, *, seq, pad):
    lane = lax.broadcasted_iota(jnp.int32, (1, LANES), 1)
    first_head = lane < HALF
    qf[...] = q_ref[...].astype(F32)
    zeros = jnp.zeros((pad, LANES), F32)
    kf[pl.ds(0, pad), :] = zeros
    vf[pl.ds(0, pad), :] = zeros
    kf[pl.ds(pad, seq), :] = k_ref[...].astype(F32)
    vf[pl.ds(pad, seq), :] = v_ref[...].astype(F32)
    acc[...] = jnp.zeros_like(acc)
    den[...] = jnp.zeros_like(den)
    for m_s in (m0, m1):
        m_s[...] = jnp.full(m_s.shape, NEG, F32)

    for bi, (window, dil) in enumerate(DILATED_BRANCHES):
        assert window // dil == WIN
        nblk = seq // (dil * WIN)
        ublk = min(GROUP_A, nblk)
        ncls = GROUP_A // ublk

        def rows(start, size, dil=dil):
            return pl.ds(start, size, stride=dil) if dil > 1 else pl.ds(start, size)

        def group(r0, g, bi=bi, dil=dil, ublk=ublk, ncls=ncls, rows=rows):
            units = []
            for c in range(ncls):
                q0 = r0 + c + dil * WIN * ublk * g
                k0 = pad + q0 - dil * WIN
                krows = rows(k0, (ublk + 1) * WIN)
                kk = kf[krows, :].astype(BF16)
                vv = vf[krows, :].astype(BF16)
                vext = jnp.concatenate([vv, jnp.ones_like(vv)], axis=1)
                for u in range(ublk):
                    qrows = rows(q0 + u * dil * WIN, WIN)
                    q = qf[qrows, :]
                    qq = jnp.concatenate([jnp.where(first_head, q, 0.0),
                                          jnp.where(first_head, 0.0, q)], axis=0).astype(BF16)
                    ksl = slice(u * WIN, (u + 2) * WIN)
                    first_blk = jnp.where(g == 0, 1, 0) if u == 0 else 0
                    s = _nt_dot(qq, kk[ksl]) + bias_ref[bi, first_blk]
                    units.append((qrows, s, vext[ksl]))
            for qrows, s, vblk in units:
                ps, alphas = [], []
                for hh, m_s in enumerate((m0, m1)):
                    sh = s[hh * WIN:(hh + 1) * WIN]
                    m_old = m_s[qrows, :]
                    m_new = jnp.maximum(m_old, jnp.max(sh, axis=-1, keepdims=True))
                    alphas.append(jnp.exp2(m_old - m_new))
                    ps.append(jnp.exp2(sh - jnp.concatenate([m_new, m_new], axis=1)).astype(BF16))
                    m_s[qrows, :] = m_new
                res = _dot(jnp.concatenate(ps, axis=0), vblk)
                alpha = jnp.where(first_head, alphas[0], alphas[1])
                acc[qrows, :] = alpha * acc[qrows, :] + jnp.where(
                    first_head, res[:WIN, :LANES], res[WIN:, :LANES])
                den[qrows, :] = alpha * den[qrows, :] + jnp.where(
                    first_head, res[:WIN, LANES:], res[WIN:, LANES:])

        def class_body(ci, carry, ngrp=nblk // ublk, ncls=ncls, group=group):
            def grp_body(g, c):
                group(ci * ncls, g)
                return c
            return lax.fori_loop(0, ngrp, grp_body, carry)

        lax.fori_loop(0, dil // ncls, class_body, 0)

    o_ref[...] = (acc[...] / den[...]).astype(o_ref.dtype)


def _attn_a_call(qa, ka, va, bias):
    b, s, _ = qa.shape
    pad = DILATED_BRANCHES[-1][0]
    blk = pl.BlockSpec((None, s, LANES), lambda bb, p: (bb, 0, p))
    bias_spec = pl.BlockSpec((None,) + bias.shape[1:], lambda bb, p: (p, 0, 0, 0, 0))
    rows = pltpu.VMEM((s, LANES), F32)
    padded = pltpu.VMEM((pad + s, LANES), F32)
    return pl.pallas_call(
        functools.partial(_attn_a_kernel, seq=s, pad=pad),
        grid=(b, HEAD_PAIRS),
        in_specs=[blk, blk, blk, bias_spec],
        out_specs=blk,
        out_shape=jax.ShapeDtypeStruct((b, s, WIDTH_A), BF16),
        scratch_shapes=[rows, padded, padded, rows, rows, rows, rows],
        compiler_params=pltpu.CompilerParams(
            dimension_semantics=("parallel", "parallel"), vmem_limit_bytes=VMEM_LIMIT),
        name="attn_a",
    )(qa, ka, va, bias)


def _t5_causal_bucket(dist):
    dist = np.asarray(dist, dtype=np.int64)
    max_exact = REL_BUCKETS // 2
    safe = np.maximum(dist, 1).astype(np.float32)
    large = max_exact + (np.log(safe / max_exact) / math.log(REL_MAX_DIST / max_exact)
                         * (REL_BUCKETS - max_exact)).astype(np.int64)
    large = np.minimum(large, REL_BUCKETS - 1)
    return np.where(dist < max_exact, dist, large).astype(np.int32)


def _attn_a_bias(rel_bias):
    span = 3 * WIN
    u = np.arange(span) - (WIN - 1)
    valid = (u >= 0) & (u <= WIN)
    tables = []
    for _, dil in DILATED_BRANCHES:
        bucket = _t5_causal_bucket(np.clip(WIN - u, 0, WIN) * dil)
        onehot = jnp.asarray(np.eye(REL_BUCKETS, dtype=np.float32)[bucket])
        vec = jnp.dot(onehot, rel_bias.astype(F32), precision=lax.Precision.HIGHEST)
        vec = jnp.where(jnp.asarray(valid)[:, None], vec * LOG2E, NEG).T
        flat = jnp.tile(vec, (1, WIN))[:, WIN - 1:WIN - 1 + WIN * (span - 1)]
        full = flat.reshape(N_HEADS_A, WIN, span - 1)[:, :, :2 * WIN]
        first = jnp.where(jnp.asarray(np.arange(2 * WIN) >= WIN)[None, None, :], full, NEG)
        tables.append(jnp.stack([full, first], axis=0))
    t = jnp.stack(tables, axis=0)
    t = t.reshape(len(DILATED_BRANCHES), 2, HEAD_PAIRS, 2 * WIN, 2 * WIN)
    return jnp.transpose(t, (2, 0, 1, 3, 4))


def _attn_b_kernel(q_ref, k_ref, vt_ref, o_ref, acc0, acc1, m0, m1, s_even, s_odd, mb_even, mb_odd):
    qi = pl.program_id(2)
    tq = TQ_B
    heads = ((acc0, m0), (acc1, m1))
    for acc, m_s in heads:
        acc[...] = jnp.zeros_like(acc)
        m_s[...] = jnp.full(m_s.shape, NEG, F32)

    def scores(kb, s_buf, mb_buf):
        krows = pl.ds(pl.multiple_of(kb * tq, tq), tq)
        for hh in range(2):
            sl = slice(hh * LANES, (hh + 1) * LANES)
            st = _nt_dot(k_ref[krows, sl], q_ref[:, sl])
            s_buf[hh] = st
            mb_buf[hh] = jnp.max(st, axis=0, keepdims=True)

    def update(kb, s_buf, mb_buf, masked):
        for hh, (acc, m_s) in enumerate(heads):
            st = s_buf[hh]
            if masked:
                key = lax.broadcasted_iota(jnp.int32, (tq, tq), 0)
                qry = lax.broadcasted_iota(jnp.int32, (tq, tq), 1)
                st = jnp.where(key <= qry, st, NEG)
                m_blk = jnp.max(st, axis=0, keepdims=True)
            else:
                m_blk = mb_buf[hh]
            m_old = m_s[...]
            m_new = jnp.maximum(m_old, m_blk)
            alpha = jnp.exp2(m_old - m_new)
            pt = jnp.exp2(st - m_new).astype(BF16)
            acc[...] = alpha * acc[...] + _dot(vt_ref[kb, hh], pt)
            m_s[...] = m_new

    scores(0, s_even, mb_even)

    def pair_body(i, c):
        kb = 2 * i
        scores(kb + 1, s_odd, mb_odd)
        update(kb, s_even, mb_even, False)
        scores(kb + 2, s_even, mb_even)
        update(kb + 1, s_odd, mb_odd, False)
        return c

    npairs = qi // 2
    lax.fori_loop(0, npairs, pair_body, 0)

    @pl.when(qi % 2 == 1)
    def _():
        scores(qi, s_odd, mb_odd)
        update(qi - 1, s_even, mb_even, False)
        update(qi, s_odd, mb_odd, True)

    @pl.when(qi % 2 == 0)
    def _():
        update(qi, s_even, mb_even, True)

    outs = []
    for acc, _ in heads:
        a = acc[...]
        outs.append(a[:V_HEAD_DIM] / a[V_HEAD_DIM:V_HEAD_DIM + 1])
    o_ref[...] = jnp.concatenate(outs, axis=0).T.astype(o_ref.dtype)


def _attn_b_call(qb, kb, vbt):
    b, s, _ = qb.shape
    tq = TQ_B
    nkb = s // tq
    vbt = vbt.reshape(b, nkb, HEAD_PAIRS, 2, VT_ROWS, tq)
    return pl.pallas_call(
        _attn_b_kernel,
        grid=(b, HEAD_PAIRS, nkb),
        in_specs=[pl.BlockSpec((None, tq, 2 * LANES), lambda bb, p, i: (bb, i, p)),
                  pl.BlockSpec((None, s, 2 * LANES), lambda bb, p, i: (bb, 0, p)),
                  pl.BlockSpec((None, nkb, None, 2, VT_ROWS, tq),
                               lambda bb, p, i: (bb, 0, p, 0, 0, 0))],
        out_specs=pl.BlockSpec((None, tq, LANES), lambda bb, p, i: (bb, i, p)),
        out_shape=jax.ShapeDtypeStruct((b, s, WIDTH_B), BF16),
        scratch_shapes=[pltpu.VMEM((VT_ROWS, tq), F32)] * 2 + [pltpu.VMEM((1, tq), F32)] * 2
        + [pltpu.VMEM((2, tq, tq), F32)] * 2 + [pltpu.VMEM((2, 1, tq), F32)] * 2,
        compiler_params=pltpu.CompilerParams(
            dimension_semantics=("parallel", "parallel", "arbitrary"),
            vmem_limit_bytes=VMEM_LIMIT),
        name="attn_b",
    )(qb, kb, vbt)


def _ffn_kernel(x_ref, a_ref, b_ref, woa_ref, wob_ref, g_ref, w1_ref, w2_ref, o_ref):
    h = x_ref[...] + _dot(a_ref[...], woa_ref[...]) + _dot(b_ref[...], wob_ref[...])
    hn = (h * lax.rsqrt(jnp.mean(h * h, axis=-1, keepdims=True) + EPS) * g_ref[...]).astype(BF16)
    mlp = None
    for c in range(D_FF // FF_CHUNK):
        sl = slice(c * FF_CHUNK, (c + 1) * FF_CHUNK)
        hid = jnp.square(jnp.maximum(_dot(hn, w1_ref[:, sl]), 0.0)).astype(BF16)
        d = _dot(hid, w2_ref[sl, :])
        mlp = d if mlp is None else mlp + d
    o_ref[...] = h + mlp


def _ffn_call(x2, a2, b2, woa, wob, g, w1, w2):
    n = x2.shape[0]
    tm = TM_FFN
    row = lambda i: (i, 0)
    consts = (woa, wob, g, w1, w2)
    return pl.pallas_call(
        _ffn_kernel,
        grid=(n // tm,),
        in_specs=[pl.BlockSpec((tm, D_MODEL), row), pl.BlockSpec((tm, WIDTH_A), row),
                  pl.BlockSpec((tm, WIDTH_B), row)] + [_const_spec(c.shape) for c in consts],
        out_specs=pl.BlockSpec((tm, D_MODEL), row),
        out_shape=jax.ShapeDtypeStruct((n, D_MODEL), F32),
        compiler_params=pltpu.CompilerParams(
            dimension_semantics=("parallel",), vmem_limit_bytes=VMEM_LIMIT),
        name="ffn",
    )(x2, a2, b2, *consts)


def _pad_heads(w, n_heads, width):
    k = w.shape[0]
    w = w.reshape(k, n_heads, width)
    return jnp.pad(w, ((0, 0), (0, 0), (0, LANES - width))).reshape(k, n_heads * LANES)


def _lane_row(g, offset=0):
    return jnp.pad(g.astype(F32), (offset, LANES - offset - g.shape[0]))[None, :]


def kernel(x, positions, norm_mix_g, w_in, qnorm_a_g, knorm_a_g, rel_bias, cq_norm_g, ckv_norm_g,
           w_uq, w_ukv, qnorm_b_g, knorm_b_g, w_o, norm_ffn_g, w_ff1, w_ff2):
    b, s, d = x.shape
    n = b * s
    c0 = 3 * WIDTH_A
    c1 = c0 + Q_LORA_RANK
    c2 = c1 + KV_LORA_RANK
    w_qkv = w_in[:, :c0].astype(BF16)
    w_cq = w_in[:, c0:c1].astype(BF16)
    w_ckv = w_in[:, c1:c2].astype(BF16)
    w_kr = jnp.pad(w_in[:, c2:], ((0, 0), (QK_NOPE_DIM, LANES - QK_HEAD_DIM_B))).astype(BF16)
    w_uq_p = _pad_heads(w_uq, N_HEADS_B, QK_HEAD_DIM_B).astype(BF16)
    w_ukv3 = w_ukv.reshape(KV_LORA_RANK, N_HEADS_B, QK_NOPE_DIM + V_HEAD_DIM)
    w_uk_p = _pad_heads(w_ukv3[:, :, :QK_NOPE_DIM].reshape(KV_LORA_RANK, -1),
                        N_HEADS_B, QK_NOPE_DIM).astype(BF16)
    ones_rows = VT_ROWS - V_HEAD_DIM
    w_uvt = jnp.transpose(w_ukv3[:, :, QK_NOPE_DIM:], (1, 2, 0))
    w_uvt = jnp.pad(w_uvt, ((0, 0), (0, ones_rows), (0, 0)))
    w_uvt = w_uvt.reshape(N_HEADS_B * VT_ROWS, KV_LORA_RANK).astype(BF16)
    inv_freq = 1.0 / (ROPE_THETA ** (jnp.arange(0, QK_ROPE_DIM, 2, dtype=F32) / QK_ROPE_DIM))
    ones_col = jnp.tile(jnp.concatenate([jnp.zeros((V_HEAD_DIM,), F32), jnp.ones((ones_rows,), F32)]),
                        N_HEADS_B)[:, None]
    consts = (
        norm_mix_g.astype(F32)[None, :], w_qkv, w_cq, w_ckv, w_kr,
        jnp.tile(qnorm_a_g.astype(F32), 2)[None, :], jnp.tile(knorm_a_g.astype(F32), 2)[None, :],
        cq_norm_g.astype(F32)[None, :], ckv_norm_g.astype(F32)[None, :],
        w_uq_p, w_uk_p, w_uvt,
        _lane_row(qnorm_b_g), _lane_row(knorm_b_g[:QK_NOPE_DIM]),
        _lane_row(knorm_b_g[QK_NOPE_DIM:], QK_NOPE_DIM),
        _lane_row(jnp.tile(inv_freq, 2), QK_NOPE_DIM),
        ones_col,
    )
    qa, ka, va, qb, kb, vbt = _proj_call(x.reshape(n, d), positions.reshape(n, 1), consts)

    out_a = _attn_a_call(qa.reshape(b, s, -1), ka.reshape(b, s, -1), va.reshape(b, s, -1),
                         _attn_a_bias(rel_bias))
    out_b = _attn_b_call(qb.reshape(b, s, -1), kb.reshape(b, s, -1), vbt)

    out = _ffn_call(x.reshape(n, d), out_a.reshape(n, -1), out_b.reshape(n, -1),
                    w_o[:WIDTH_A].astype(BF16), w_o[WIDTH_A:].astype(BF16),
                    norm_ffn_g.astype(F32)[None, :], w_ff1.astype(BF16), w_ff2.astype(BF16))
    return out.reshape(b, s, d)
```

```python
import functools
import math

import numpy as np
import jax
import jax.numpy as jnp
from jax import lax
from jax.experimental import pallas as pl
from jax.experimental.pallas import tpu as pltpu

F32 = jnp.float32
BF16 = jnp.bfloat16

D_MODEL = 1024
HEAD_DIM_A = 64
N_HEADS_A = 8
WIDTH_A = N_HEADS_A * HEAD_DIM_A
DILATED_BRANCHES = ((128, 1), (512, 4), (2048, 16))
N_HEADS_B = 8
QK_NOPE_DIM = 64
QK_ROPE_DIM = 32
V_HEAD_DIM = 64
QK_HEAD_DIM_B = QK_NOPE_DIM + QK_ROPE_DIM
Q_LORA_RANK = 768
KV_LORA_RANK = 256
WIDTH_B = N_HEADS_B * V_HEAD_DIM
ROPE_THETA = 10000.0
D_FF = 4 * D_MODEL
REL_BUCKETS = 32
REL_MAX_DIST = 2048
EPS = 1e-6

LANES = 128
HALF = LANES // 2
HEAD_PAIRS = N_HEADS_A // 2
WIN = 128
NEG = -1e30
VMEM_LIMIT = 56 * 1024 * 1024

TM_PROJ = 512
TM_FFN = 512
TQ_B = 512
TSUB_B = 256
FF_CHUNK = 1024
GROUP_A = 8

LOG2E = math.log2(math.e)
SCALE_A = LOG2E / math.sqrt(HEAD_DIM_A)
SCALE_B = LOG2E / math.sqrt(QK_HEAD_DIM_B)

VT_ROWS = V_HEAD_DIM + 16

assert HEAD_DIM_A == HALF and V_HEAD_DIM == HALF
assert TM_PROJ == TQ_B


def _nt_dot(a, b):
    return lax.dot_general(a, b, (((1,), (1,)), ((), ())), preferred_element_type=F32)


def _dot(a, b):
    return jnp.dot(a, b, preferred_element_type=F32)


def _const_spec(shape):
    nd = len(shape)
    return pl.BlockSpec(shape, lambda *_: (0,) * nd, pipeline_mode=pl.Buffered(1))


def _rms_bf16(x, g_row):
    return (x * lax.rsqrt(jnp.mean(x * x, axis=-1, keepdims=True) + EPS) * g_row).astype(BF16)


def _proj_a_kernel(x_ref, gmix_ref, wqkv_ref, hsum_ref, gqa_ref, gka_ref, qa_ref, ka_ref, va_ref):
    xn = _rms_bf16(x_ref[...], gmix_ref[...])
    qkv = _dot(xn, wqkv_ref[...])

    def head_norm(y, g_row):
        ss = _dot((y * y).astype(BF16), hsum_ref[...])
        return y * lax.rsqrt(ss / HEAD_DIM_A + EPS) * g_row

    qa_ref[...] = (head_norm(qkv[:, :WIDTH_A], gqa_ref[...]) * SCALE_A).astype(BF16)
    ka_ref[...] = head_norm(qkv[:, WIDTH_A:2 * WIDTH_A], gka_ref[...]).astype(BF16)
    va_ref[...] = qkv[:, 2 * WIDTH_A:].astype(BF16)


def _proj_a_call(x2, consts):
    n = x2.shape[0]
    tm = TM_PROJ
    row = lambda i: (i, 0)
    return pl.pallas_call(
        _proj_a_kernel,
        grid=(n // tm,),
        in_specs=[pl.BlockSpec((tm, D_MODEL), row)] + [_const_spec(c.shape) for c in consts],
        out_specs=[pl.BlockSpec((tm, WIDTH_A), row)] * 3,
        out_shape=[jax.ShapeDtypeStruct((n, WIDTH_A), BF16)] * 3,
        compiler_params=pltpu.CompilerParams(
            dimension_semantics=("parallel",), vmem_limit_bytes=VMEM_LIMIT),
        name="proj_a",
    )(x2, *consts)


def _proj_b_kernel(x_ref, pos_ref, posr_ref, gmix_ref, wcq_ref, wckv_ref, wkr_ref,
                   gcq_ref, gckv_ref, wuqt_ref, wuk_ref, wuvt_ref,
                   gqbt_ref, gkbn_ref, gkbr_ref, invf_ref, invft_ref, ones_ref,
                   qt_ref, kb_ref, vt_ref):
    xn = _rms_bf16(x_ref[...], gmix_ref[...])
    cqn = _rms_bf16(_dot(xn, wcq_ref[...]), gcq_ref[...])
    ckvn = _rms_bf16(_dot(xn, wckv_ref[...]), gckv_ref[...])
    half = QK_ROPE_DIM // 2
    r0, r1, r2 = QK_NOPE_DIM, QK_NOPE_DIM + half, QK_HEAD_DIM_B

    qt = _nt_dot(wuqt_ref[...], cqn)
    ang_t = invft_ref[...] * posr_ref[...].astype(F32)
    cs_t = jnp.cos(ang_t)
    sn_t = jnp.sin(ang_t)
    for h in range(N_HEADS_B):
        qh = qt[h * LANES:(h + 1) * LANES]
        rq = lax.rsqrt(jnp.sum(qh * qh, axis=0, keepdims=True) / QK_HEAD_DIM_B + EPS)
        y = qh * rq * gqbt_ref[...]
        y1, y2 = y[r0:r1], y[r1:r2]
        out = jnp.concatenate([y[:r0], y1 * cs_t - y2 * sn_t, y2 * cs_t + y1 * sn_t, y[r2:]], axis=0)
        qt_ref[h * LANES:(h + 1) * LANES, :] = (out * SCALE_B).astype(BF16)

    vt_ref[...] = (_nt_dot(wuvt_ref[...], ckvn) + ones_ref[...]).astype(BF16)

    kn = _dot(ckvn, wuk_ref[...])
    kr = _dot(xn, wkr_ref[...])
    lane = lax.broadcasted_iota(jnp.int32, (1, LANES), 1)
    ang = pos_ref[...].astype(F32) * invf_ref[...]
    cs = jnp.cos(ang)
    sn = jnp.sin(ang)
    s_lo = jnp.where((lane >= r0) & (lane < r1), -sn, 0.0)
    s_hi = jnp.where((lane >= r1) & (lane < r2), sn, 0.0)
    ss_kr = jnp.sum(kr * kr, axis=-1, keepdims=True)
    krg = kr * gkbr_ref[...]
    kr_rot = krg * cs + pltpu.roll(krg, LANES - half, 1) * s_lo + pltpu.roll(krg, half, 1) * s_hi
    for h in range(N_HEADS_B):
        sl = slice(h * LANES, (h + 1) * LANES)
        kh = kn[:, sl]
        rk = lax.rsqrt((jnp.sum(kh * kh, axis=-1, keepdims=True) + ss_kr) / QK_HEAD_DIM_B + EPS)
        kb_ref[:, sl] = ((kh * gkbn_ref[...] + kr_rot) * rk).astype(BF16)


def _proj_b_call(x2, pos_col, pos_row, consts):
    n = x2.shape[0]
    tm = TM_PROJ
    row = lambda i: (i, 0)
    blk3 = lambda i: (i, 0, 0)
    wide = N_HEADS_B * LANES
    vt_rows = N_HEADS_B * VT_ROWS
    return pl.pallas_call(
        _proj_b_kernel,
        grid=(n // tm,),
        in_specs=[pl.BlockSpec((tm, D_MODEL), row), pl.BlockSpec((tm, 1), row),
                  pl.BlockSpec((None, 1, tm), blk3)] + [_const_spec(c.shape) for c in consts],
        out_specs=[pl.BlockSpec((None, wide, tm), blk3), pl.BlockSpec((tm, wide), row),
                   pl.BlockSpec((None, vt_rows, tm), blk3)],
        out_shape=[jax.ShapeDtypeStruct((n // tm, wide, tm), BF16),
                   jax.ShapeDtypeStruct((n, wide), BF16),
                   jax.ShapeDtypeStruct((n // tm, vt_rows, tm), BF16)],
        compiler_params=pltpu.CompilerParams(
            dimension_semantics=("parallel",), vmem_limit_bytes=VMEM_LIMIT),
        name="proj_b",
    )(x2, pos_col, pos_row, *consts)


def _attn_a_kernel(q_ref, k_ref, v_ref, bias_ref, o_ref,
                   qf, kf, vf, acc, den, m0, m1, *, seq, pad):
    lane = lax.broadcasted_iota(jnp.int32, (1, LANES), 1)
    first_head = lane < HALF
    qf[...] = q_ref[...].astype(F32)
    zeros = jnp.zeros((pad, LANES), F32)
    kf[pl.ds(0, pad), :] = zeros
    vf[pl.ds(0, pad), :] = zeros
    kf[pl.ds(pad, seq), :] = k_ref[...].astype(F32)
    vf[pl.ds(pad, seq), :] = v_ref[...].astype(F32)
    acc[...] = jnp.zeros_like(acc)
    den[...] = jnp.zeros_like(den)
    for m_s in (m0, m1):
        m_s[...] = jnp.full(m_s.shape, NEG, F32)

    for bi, (window, dil) in enumerate(DILATED_BRANCHES):
        assert window // dil == WIN
        nblk = seq // (dil * WIN)
        ublk = min(GROUP_A, nblk)
        ncls = GROUP_A // ublk

        def rows(start, size, dil=dil):
            return pl.ds(start, size, stride=dil) if dil > 1 else pl.ds(start, size)

        def group(r0, g, bi=bi, dil=dil, ublk=ublk, ncls=ncls, rows=rows):
            units = []
            for c in range(ncls):
                q0 = r0 + c + dil * WIN * ublk * g
                k0 = pad + q0 - dil * WIN
                krows = rows(k0, (ublk + 1) * WIN)
                kk = kf[krows, :].astype(BF16)
                vv = vf[krows, :].astype(BF16)
                vext = jnp.concatenate([vv, jnp.ones_like(vv)], axis=1)
                for u in range(ublk):
                    qrows = rows(q0 + u * dil * WIN, WIN)
                    q = qf[qrows, :]
                    qq = jnp.concatenate([jnp.where(first_head, q, 0.0),
                                          jnp.where(first_head, 0.0, q)], axis=0).astype(BF16)
                    ksl = slice(u * WIN, (u + 2) * WIN)
                    first_blk = jnp.where(g == 0, 1, 0) if u == 0 else 0
                    s = _nt_dot(qq, kk[ksl]) + bias_ref[bi, first_blk]
                    units.append((qrows, s, vext[ksl]))
            for qrows, s, vblk in units:
                ps, alphas = [], []
                for hh, m_s in enumerate((m0, m1)):
                    sh = s[hh * WIN:(hh + 1) * WIN]
                    m_old = m_s[qrows, :]
                    m_new = jnp.maximum(m_old, jnp.max(sh, axis=-1, keepdims=True))
                    alphas.append(jnp.exp2(m_old - m_new))
                    ps.append(jnp.exp2(sh - jnp.concatenate([m_new, m_new], axis=1)).astype(BF16))
                    m_s[qrows, :] = m_new
                res = _dot(jnp.concatenate(ps, axis=0), vblk)
                alpha = jnp.where(first_head, alphas[0], alphas[1])
                acc[qrows, :] = alpha * acc[qrows, :] + jnp.where(
                    first_head, res[:WIN, :LANES], res[WIN:, :LANES])
                den[qrows, :] = alpha * den[qrows, :] + jnp.where(
                    first_head, res[:WIN, LANES:], res[WIN:, LANES:])

        def class_body(ci, carry, ngrp=nblk // ublk, ncls=ncls, group=group):
            def grp_body(g, c):
                group(ci * ncls, g)
                return c
            return lax.fori_loop(0, ngrp, grp_body, carry)

        lax.fori_loop(0, dil // ncls, class_body, 0)

    o_ref[...] = (acc[...] / den[...]).astype(o_ref.dtype)


def _attn_a_call(qa, ka, va, bias):
    b, s, _ = qa.shape
    pad = DILATED_BRANCHES[-1][0]
    blk = pl.BlockSpec((None, s, LANES), lambda bb, p: (bb, 0, p))
    bias_spec = pl.BlockSpec((None,) + bias.shape[1:], lambda bb, p: (p, 0, 0, 0, 0))
    rows = pltpu.VMEM((s, LANES), F32)
    padded = pltpu.VMEM((pad + s, LANES), F32)
    return pl.pallas_call(
        functools.partial(_attn_a_kernel, seq=s, pad=pad),
        grid=(b, HEAD_PAIRS),
        in_specs=[blk, blk, blk, bias_spec],
        out_specs=blk,
        out_shape=jax.ShapeDtypeStruct((b, s, WIDTH_A), BF16),
        scratch_shapes=[rows, padded, padded, rows, rows, rows, rows],
        compiler_params=pltpu.CompilerParams(
            dimension_semantics=("parallel", "parallel"), vmem_limit_bytes=VMEM_LIMIT),
        name="attn_a",
    )(qa, ka, va, bias)


def _t5_causal_bucket(dist):
    dist = np.asarray(dist, dtype=np.int64)
    max_exact = REL_BUCKETS // 2
    safe = np.maximum(dist, 1).astype(np.float32)
    large = max_exact + (np.log(safe / max_exact) / math.log(REL_MAX_DIST / max_exact)
                         * (REL_BUCKETS - max_exact)).astype(np.int64)
    large = np.minimum(large, REL_BUCKETS - 1)
    return np.where(dist < max_exact, dist, large).astype(np.int32)


def _attn_a_bias(rel_bias):
    span = 3 * WIN
    u = np.arange(span) - (WIN - 1)
    valid = (u >= 0) & (u <= WIN)
    tables = []
    for _, dil in DILATED_BRANCHES:
        bucket = _t5_causal_bucket(np.clip(WIN - u, 0, WIN) * dil)
        onehot = jnp.asarray(np.eye(REL_BUCKETS, dtype=np.float32)[bucket])
        vec = jnp.dot(onehot, rel_bias.astype(F32), precision=lax.Precision.HIGHEST)
        vec = jnp.where(jnp.asarray(valid)[:, None], vec * LOG2E, NEG).T
        flat = jnp.tile(vec, (1, WIN))[:, WIN - 1:WIN - 1 + WIN * (span - 1)]
        full = flat.reshape(N_HEADS_A, WIN, span - 1)[:, :, :2 * WIN]
        first = jnp.where(jnp.asarray(np.arange(2 * WIN) >= WIN)[None, None, :], full, NEG)
        tables.append(jnp.stack([full, first], axis=0))
    t = jnp.stack(tables, axis=0)
    t = t.reshape(len(DILATED_BRANCHES), 2, HEAD_PAIRS, 2 * WIN, 2 * WIN)
    return jnp.transpose(t, (2, 0, 1, 3, 4))


def _attn_b_kernel(qt_ref, k_ref, vt_ref, o_ref, acc0, acc1, m0, m1, s_even, s_odd, mb_even, mb_odd):
    qi = pl.program_id(2)
    tq = TQ_B
    heads = ((acc0, m0), (acc1, m1))
    for acc, m_s in heads:
        acc[...] = jnp.zeros_like(acc)
        m_s[...] = jnp.full(m_s.shape, NEG, F32)

    def scores(kb, s_buf, mb_buf):
        krows = pl.ds(pl.multiple_of(kb * tq, tq), tq)
        for hh in range(2):
            sl = slice(hh * LANES, (hh + 1) * LANES)
            st = _dot(k_ref[krows, sl], qt_ref[hh])
            s_buf[hh] = st
            mb_buf[hh] = jnp.max(st, axis=0, keepdims=True)

    def update(kb, s_buf, mb_buf, masked):
        for hh, (acc, m_s) in enumerate(heads):
            st = s_buf[hh]
            if masked:
                key = lax.broadcasted_iota(jnp.int32, (tq, tq), 0)
                qry = lax.broadcasted_iota(jnp.int32, (tq, tq), 1)
                st = jnp.where(key <= qry, st, NEG)
                m_blk = jnp.max(st, axis=0, keepdims=True)
            else:
                m_blk = mb_buf[hh]
            m_old = m_s[...]
            m_new = jnp.maximum(m_old, m_blk)
            alpha = jnp.exp2(m_old - m_new)
            pt = jnp.exp2(st - m_new).astype(BF16)
            acc[...] = alpha * acc[...] + _dot(vt_ref[kb, hh], pt)
            m_s[...] = m_new

    scores(0, s_even, mb_even)

    def pair_body(i, c):
        kb = 2 * i
        scores(kb + 1, s_odd, mb_odd)
        update(kb, s_even, mb_even, False)
        scores(kb + 2, s_even, mb_even)
        update(kb + 1, s_odd, mb_odd, False)
        return c

    npairs = qi // 2
    lax.fori_loop(0, npairs, pair_body, 0)

    @pl.when(qi % 2 == 1)
    def _():
        scores(qi, s_odd, mb_odd)
        update(qi - 1, s_even, mb_even, False)
        update(qi, s_odd, mb_odd, True)

    @pl.when(qi % 2 == 0)
    def _():
        update(qi, s_even, mb_even, True)

    outs = []
    for acc, _ in heads:
        a = acc[...]
        outs.append(a[:V_HEAD_DIM] / a[V_HEAD_DIM:V_HEAD_DIM + 1])
    o_ref[...] = jnp.concatenate(outs, axis=0).T.astype(o_ref.dtype)


def _attn_b_call(qbt, kb, vbt):
    b, s, _ = kb.shape
    tq = TQ_B
    nkb = s // tq
    qbt = qbt.reshape(b, nkb, HEAD_PAIRS, 2, LANES, tq)
    vbt = vbt.reshape(b, nkb, HEAD_PAIRS, 2, VT_ROWS, tq)
    return pl.pallas_call(
        _attn_b_kernel,
        grid=(b, HEAD_PAIRS, nkb),
        in_specs=[pl.BlockSpec((None, None, None, 2, LANES, tq),
                               lambda bb, p, i: (bb, i, p, 0, 0, 0)),
                  pl.BlockSpec((None, s, 2 * LANES), lambda bb, p, i: (bb, 0, p)),
                  pl.BlockSpec((None, nkb, None, 2, VT_ROWS, tq),
                               lambda bb, p, i: (bb, 0, p, 0, 0, 0))],
        out_specs=pl.BlockSpec((None, tq, LANES), lambda bb, p, i: (bb, i, p)),
        out_shape=jax.ShapeDtypeStruct((b, s, WIDTH_B), BF16),
        scratch_shapes=[pltpu.VMEM((VT_ROWS, tq), F32)] * 2 + [pltpu.VMEM((1, tq), F32)] * 2
        + [pltpu.VMEM((2, tq, tq), F32)] * 2 + [pltpu.VMEM((2, 1, tq), F32)] * 2,
        compiler_params=pltpu.CompilerParams(
            dimension_semantics=("parallel", "parallel", "arbitrary"),
            vmem_limit_bytes=VMEM_LIMIT),
        name="attn_b",
    )(qbt, kb, vbt)


def _ffn_kernel(x_ref, a_ref, b_ref, woa_ref, wob_ref, g_ref, w1_ref, w2_ref, o_ref):
    h = x_ref[...] + _dot(a_ref[...], woa_ref[...]) + _dot(b_ref[...], wob_ref[...])
    hn = (h * lax.rsqrt(jnp.mean(h * h, axis=-1, keepdims=True) + EPS) * g_ref[...]).astype(BF16)
    mlp = None
    for c in range(D_FF // FF_CHUNK):
        sl = slice(c * FF_CHUNK, (c + 1) * FF_CHUNK)
        hid = jnp.square(jnp.maximum(_dot(hn, w1_ref[:, sl]), 0.0)).astype(BF16)
        d = _dot(hid, w2_ref[sl, :])
        mlp = d if mlp is None else mlp + d
    o_ref[...] = h + mlp


def _ffn_call(x2, a2, b2, woa, wob, g, w1, w2):
    n = x2.shape[0]
    tm = TM_FFN
    row = lambda i: (i, 0)
    consts = (woa, wob, g, w1, w2)
    return pl.pallas_call(
        _ffn_kernel,
        grid=(n // tm,),
        in_specs=[pl.BlockSpec((tm, D_MODEL), row), pl.BlockSpec((tm, WIDTH_A), row),
                  pl.BlockSpec((tm, WIDTH_B), row)] + [_const_spec(c.shape) for c in consts],
        out_specs=pl.BlockSpec((tm, D_MODEL), row),
        out_shape=jax.ShapeDtypeStruct((n, D_MODEL), F32),
        compiler_params=pltpu.CompilerParams(
            dimension_semantics=("parallel",), vmem_limit_bytes=VMEM_LIMIT),
        name="ffn",
    )(x2, a2, b2, *consts)


def _pad_heads(w, n_heads, width):
    k = w.shape[0]
    w = w.reshape(k, n_heads, width)
    return jnp.pad(w, ((0, 0), (0, 0), (0, LANES - width))).reshape(k, n_heads * LANES)


def _lane_row(g, offset=0):
    return jnp.pad(g.astype(F32), (offset, LANES - offset - g.shape[0]))[None, :]


def kernel(x, positions, norm_mix_g, w_in, qnorm_a_g, knorm_a_g, rel_bias, cq_norm_g, ckv_norm_g,
           w_uq, w_ukv, qnorm_b_g, knorm_b_g, w_o, norm_ffn_g, w_ff1, w_ff2):
    b, s, d = x.shape
    n = b * s
    c0 = 3 * WIDTH_A
    c1 = c0 + Q_LORA_RANK
    c2 = c1 + KV_LORA_RANK
    w_qkv = w_in[:, :c0].astype(BF16)
    w_cq = w_in[:, c0:c1].astype(BF16)
    w_ckv = w_in[:, c1:c2].astype(BF16)
    w_kr = jnp.pad(w_in[:, c2:], ((0, 0), (QK_NOPE_DIM, LANES - QK_HEAD_DIM_B))).astype(BF16)
    w_uqt = _pad_heads(w_uq, N_HEADS_B, QK_HEAD_DIM_B).T.astype(BF16)
    w_ukv3 = w_ukv.reshape(KV_LORA_RANK, N_HEADS_B, QK_NOPE_DIM + V_HEAD_DIM)
    w_uk_p = _pad_heads(w_ukv3[:, :, :QK_NOPE_DIM].reshape(KV_LORA_RANK, -1),
                        N_HEADS_B, QK_NOPE_DIM).astype(BF16)
    ones_rows = VT_ROWS - V_HEAD_DIM
    w_uvt = jnp.transpose(w_ukv3[:, :, QK_NOPE_DIM:], (1, 2, 0))
    w_uvt = jnp.pad(w_uvt, ((0, 0), (0, ones_rows), (0, 0)))
    w_uvt = w_uvt.reshape(N_HEADS_B * VT_ROWS, KV_LORA_RANK).astype(BF16)
    inv_freq = 1.0 / (ROPE_THETA ** (jnp.arange(0, QK_ROPE_DIM, 2, dtype=F32) / QK_ROPE_DIM))
    ones_col = jnp.tile(jnp.concatenate([jnp.zeros((V_HEAD_DIM,), F32), jnp.ones((ones_rows,), F32)]),
                        N_HEADS_B)[:, None]
    g_mix = norm_mix_g.astype(F32)[None, :]
    head_sum = jnp.asarray(np.kron(np.eye(N_HEADS_A), np.ones((HEAD_DIM_A, HEAD_DIM_A))), BF16)
    consts_a = (
        g_mix, w_qkv, head_sum,
        jnp.tile(qnorm_a_g.astype(F32), N_HEADS_A)[None, :],
        jnp.tile(knorm_a_g.astype(F32), N_HEADS_A)[None, :],
    )
    tm = TM_PROJ
    gq_col = jnp.pad(qnorm_b_g.astype(F32), (0, LANES - QK_HEAD_DIM_B))[:, None]
    consts_b = (
        g_mix, w_cq, w_ckv, w_kr,
        cq_norm_g.astype(F32)[None, :], ckv_norm_g.astype(F32)[None, :],
        w_uqt, w_uk_p, w_uvt,
        jnp.broadcast_to(gq_col, (LANES, tm)), _lane_row(knorm_b_g[:QK_NOPE_DIM]),
        _lane_row(knorm_b_g[QK_NOPE_DIM:], QK_NOPE_DIM),
        _lane_row(jnp.tile(inv_freq, 2), QK_NOPE_DIM),
        jnp.broadcast_to(inv_freq[:, None], (QK_ROPE_DIM // 2, tm)),
        ones_col,
    )
    x2 = x.reshape(n, d)
    qa, ka, va = _proj_a_call(x2, consts_a)
    qbt, kb, vbt = _proj_b_call(x2, positions.reshape(n, 1), positions.reshape(n // tm, 1, tm),
                                consts_b)

    out_a = _attn_a_call(qa.reshape(b, s, -1), ka.reshape(b, s, -1), va.reshape(b, s, -1),
                         _attn_a_bias(rel_bias))
    out_b = _attn_b_call(qbt, kb.reshape(b, s, -1), vbt)

    out = _ffn_call(x.reshape(n, d), out_a.reshape(n, -1), out_b.reshape(n, -1),
                    w_o[:WIDTH_A].astype(BF16), w_o[WIDTH_A:].astype(BF16),
                    norm_ffn_g.astype(F32)[None, :], w_ff1.astype(BF16), w_ff2.astype(BF16))
    return out.reshape(b, s, d)
```

```python
import functools
import math

import numpy as np
import jax
import jax.numpy as jnp
from jax import lax
from jax.experimental import pallas as pl
from jax.experimental.pallas import tpu as pltpu

F32 = jnp.float32
BF16 = jnp.bfloat16

D_MODEL = 1024
HEAD_DIM_A = 64
N_HEADS_A = 8
WIDTH_A = N_HEADS_A * HEAD_DIM_A
DILATED_BRANCHES = ((128, 1), (512, 4), (2048, 16))
N_HEADS_B = 8
QK_NOPE_DIM = 64
QK_ROPE_DIM = 32
V_HEAD_DIM = 64
QK_HEAD_DIM_B = QK_NOPE_DIM + QK_ROPE_DIM
Q_LORA_RANK = 768
KV_LORA_RANK = 256
WIDTH_B = N_HEADS_B * V_HEAD_DIM
ROPE_THETA = 10000.0
D_FF = 4 * D_MODEL
REL_BUCKETS = 32
REL_MAX_DIST = 2048
EPS = 1e-6

LANES = 128
HALF = LANES // 2
HEAD_PAIRS = N_HEADS_A // 2
WIN = 128
NEG = -1e30
VMEM_LIMIT = 56 * 1024 * 1024

TM_PROJ = 512
TM_FFN = 512
TQ_B = 512
TSUB_B = 256
FF_CHUNK = 1024
GROUP_A = 16
CLASSES = 16

LOG2E = math.log2(math.e)
SCALE_A = LOG2E / math.sqrt(HEAD_DIM_A)
SCALE_B = LOG2E / math.sqrt(QK_HEAD_DIM_B)

VT_ROWS = V_HEAD_DIM + 16

assert HEAD_DIM_A == HALF and V_HEAD_DIM == HALF
assert TM_PROJ == TQ_B


def _nt_dot(a, b):
    return lax.dot_general(a, b, (((1,), (1,)), ((), ())), preferred_element_type=F32)


def _dot(a, b):
    return jnp.dot(a, b, preferred_element_type=F32)


def _const_spec(shape):
    nd = len(shape)
    return pl.BlockSpec(shape, lambda *_: (0,) * nd, pipeline_mode=pl.Buffered(1))


def _rms_bf16(x, g_row):
    return (x * lax.rsqrt(jnp.mean(x * x, axis=-1, keepdims=True) + EPS) * g_row).astype(BF16)


def _proj_a_kernel(x_ref, gmix_ref, wqkv_ref, hsum_ref, gqa_ref, gka_ref, qa_ref, ka_ref, va_ref):
    xn = _rms_bf16(x_ref[...], gmix_ref[...])
    qkv = _dot(xn, wqkv_ref[...])

    def head_norm(y, g_row):
        ss = _dot((y * y).astype(BF16), hsum_ref[...])
        return y * lax.rsqrt(ss / HEAD_DIM_A + EPS) * g_row

    qa_ref[...] = (head_norm(qkv[:, :WIDTH_A], gqa_ref[...]) * SCALE_A).astype(BF16)
    ka_ref[...] = head_norm(qkv[:, WIDTH_A:2 * WIDTH_A], gka_ref[...]).astype(BF16)
    va_ref[...] = qkv[:, 2 * WIDTH_A:].astype(BF16)


def _proj_a_call(x, consts):
    b, s, d = x.shape
    rows = s // CLASSES
    xv = x.reshape(b, rows, CLASSES * d)
    out = jax.ShapeDtypeStruct((b, CLASSES, rows, WIDTH_A), BF16)
    outs = pl.pallas_call(
        _proj_a_kernel,
        grid=(b, CLASSES),
        in_specs=[pl.BlockSpec((None, rows, d), lambda bb, r: (bb, 0, r))]
        + [_const_spec(c.shape) for c in consts],
        out_specs=[pl.BlockSpec((None, None, rows, WIDTH_A), lambda bb, r: (bb, r, 0, 0))] * 3,
        out_shape=[out] * 3,
        compiler_params=pltpu.CompilerParams(
            dimension_semantics=("parallel", "parallel"), vmem_limit_bytes=VMEM_LIMIT),
        name="proj_a",
    )(xv, *consts)
    return [o.reshape(b, s, WIDTH_A) for o in outs]


def _proj_b_kernel(x_ref, pos_ref, posr_ref, gmix_ref, wcq_ref, wckv_ref, wkr_ref,
                   gcq_ref, gckv_ref, wuqt_ref, wuk_ref, wuvt_ref,
                   gqbt_ref, gkbn_ref, gkbr_ref, invf_ref, invft_ref, ones_ref,
                   qt_ref, kb_ref, vt_ref):
    xn = _rms_bf16(x_ref[...], gmix_ref[...])
    cqn = _rms_bf16(_dot(xn, wcq_ref[...]), gcq_ref[...])
    ckvn = _rms_bf16(_dot(xn, wckv_ref[...]), gckv_ref[...])
    half = QK_ROPE_DIM // 2
    r0, r1, r2 = QK_NOPE_DIM, QK_NOPE_DIM + half, QK_HEAD_DIM_B

    qt = _nt_dot(wuqt_ref[...], cqn)
    ang_t = invft_ref[...] * posr_ref[...].astype(F32)
    cs_t = jnp.cos(ang_t)
    sn_t = jnp.sin(ang_t)
    for h in range(N_HEADS_B):
        qh = qt[h * LANES:(h + 1) * LANES]
        rq = lax.rsqrt(jnp.sum(qh * qh, axis=0, keepdims=True) / QK_HEAD_DIM_B + EPS)
        y = qh * rq * gqbt_ref[...]
        y1, y2 = y[r0:r1], y[r1:r2]
        out = jnp.concatenate([y[:r0], y1 * cs_t - y2 * sn_t, y2 * cs_t + y1 * sn_t, y[r2:]], axis=0)
        qt_ref[h * LANES:(h + 1) * LANES, :] = (out * SCALE_B).astype(BF16)

    vt_ref[...] = (_nt_dot(wuvt_ref[...], ckvn) + ones_ref[...]).astype(BF16)

    kn = _dot(ckvn, wuk_ref[...])
    kr = _dot(xn, wkr_ref[...])
    lane = lax.broadcasted_iota(jnp.int32, (1, LANES), 1)
    ang = pos_ref[...].astype(F32) * invf_ref[...]
    cs = jnp.cos(ang)
    sn = jnp.sin(ang)
    s_lo = jnp.where((lane >= r0) & (lane < r1), -sn, 0.0)
    s_hi = jnp.where((lane >= r1) & (lane < r2), sn, 0.0)
    ss_kr = jnp.sum(kr * kr, axis=-1, keepdims=True)
    krg = kr * gkbr_ref[...]
    kr_rot = krg * cs + pltpu.roll(krg, LANES - half, 1) * s_lo + pltpu.roll(krg, half, 1) * s_hi
    for h in range(N_HEADS_B):
        sl = slice(h * LANES, (h + 1) * LANES)
        kh = kn[:, sl]
        rk = lax.rsqrt((jnp.sum(kh * kh, axis=-1, keepdims=True) + ss_kr) / QK_HEAD_DIM_B + EPS)
        kb_ref[:, sl] = ((kh * gkbn_ref[...] + kr_rot) * rk).astype(BF16)


def _proj_b_call(x2, pos_col, pos_row, consts):
    n = x2.shape[0]
    tm = TM_PROJ
    row = lambda i: (i, 0)
    blk3 = lambda i: (i, 0, 0)
    wide = N_HEADS_B * LANES
    vt_rows = N_HEADS_B * VT_ROWS
    return pl.pallas_call(
        _proj_b_kernel,
        grid=(n // tm,),
        in_specs=[pl.BlockSpec((tm, D_MODEL), row), pl.BlockSpec((tm, 1), row),
                  pl.BlockSpec((None, 1, tm), blk3)] + [_const_spec(c.shape) for c in consts],
        out_specs=[pl.BlockSpec((None, wide, tm), blk3), pl.BlockSpec((tm, wide), row),
                   pl.BlockSpec((None, vt_rows, tm), blk3)],
        out_shape=[jax.ShapeDtypeStruct((n // tm, wide, tm), BF16),
                   jax.ShapeDtypeStruct((n, wide), BF16),
                   jax.ShapeDtypeStruct((n // tm, vt_rows, tm), BF16)],
        compiler_params=pltpu.CompilerParams(
            dimension_semantics=("parallel",), vmem_limit_bytes=VMEM_LIMIT),
        name="proj_b",
    )(x2, pos_col, pos_row, *consts)


def _attn_a_kernel(q_ref, k_ref, v_ref, bias_ref, o_ref,
                   qf, kf, vf, acc, den, m0, m1, onat, *, seq, pad):
    lane = lax.broadcasted_iota(jnp.int32, (1, LANES), 1)
    first_head = lane < HALF
    qf[...] = q_ref[...].astype(F32)
    zeros = jnp.zeros((pad, LANES), F32)
    kf[pl.ds(0, pad), :] = zeros
    vf[pl.ds(0, pad), :] = zeros
    kf[pl.ds(pad, seq), :] = k_ref[...].astype(F32)
    vf[pl.ds(pad, seq), :] = v_ref[...].astype(F32)
    acc[...] = jnp.zeros_like(acc)
    den[...] = jnp.zeros_like(den)
    for m_s in (m0, m1):
        m_s[...] = jnp.full(m_s.shape, NEG, F32)

    cls_rows = seq // CLASSES

    for bi, (window, dil) in enumerate(DILATED_BRANCHES):
        assert window // dil == WIN
        nchunk = CLASSES // dil
        clen = WIN // nchunk
        nblk = seq // (dil * WIN)
        ublk = min(GROUP_A, nblk)
        ncls = GROUP_A // ublk

        def gather(ref, starts, size):
            return jnp.concatenate([ref[pl.ds(st, size), :] for st in starts], axis=0)

        def group(r0, g, bi=bi, dil=dil, nchunk=nchunk, clen=clen, ublk=ublk, ncls=ncls,
                  gather=gather):
            plans = []
            for cc in range(ncls):
                bases = [pl.multiple_of((dil * c + r0 + cc) * cls_rows + clen * ublk * g, 8)
                         for c in range(nchunk)]
                for u in range(ublk):
                    plans.append((bases, u))

            def scores(bases, u):
                qstarts = [b0 + clen * u for b0 in bases]
                kstarts = [pad + b0 + clen * (u - 1) for b0 in bases]
                q = gather(qf, qstarts, clen)
                qq = jnp.concatenate([jnp.where(first_head, q, 0.0),
                                      jnp.where(first_head, 0.0, q)], axis=0).astype(BF16)
                kk = jnp.concatenate([gather(kf, kstarts, clen),
                                      gather(kf, [k0 + clen for k0 in kstarts], clen)],
                                     axis=0).astype(BF16)
                vv = jnp.concatenate([gather(vf, kstarts, clen),
                                      gather(vf, [k0 + clen for k0 in kstarts], clen)],
                                     axis=0).astype(BF16)
                first_blk = jnp.where(g == 0, 1, 0) if u == 0 else 0
                s = _nt_dot(qq, kk) + bias_ref[bi, first_blk]
                return qstarts, s, jnp.concatenate([vv, jnp.ones_like(vv)], axis=1)

            def update(qstarts, s, vext):
                ps, alphas = [], []
                for hh, m_s in enumerate((m0, m1)):
                    sh = s[hh * WIN:(hh + 1) * WIN]
                    m_old = gather(m_s, qstarts, clen)
                    m_new = jnp.maximum(m_old, jnp.max(sh, axis=-1, keepdims=True))
                    alphas.append(jnp.exp2(m_old - m_new))
                    ps.append(jnp.exp2(sh - jnp.concatenate([m_new, m_new], axis=1)).astype(BF16))
                    for c, st in enumerate(qstarts):
                        m_s[pl.ds(st, clen), :] = m_new[c * clen:(c + 1) * clen]
                res = _dot(jnp.concatenate(ps, axis=0), vext)
                alpha = jnp.where(first_head, alphas[0], alphas[1])
                acc_new = alpha * gather(acc, qstarts, clen) + jnp.where(
                    first_head, res[:WIN, :LANES], res[WIN:, :LANES])
                den_new = alpha * gather(den, qstarts, clen) + jnp.where(
                    first_head, res[:WIN, LANES:], res[WIN:, LANES:])
                for c, st in enumerate(qstarts):
                    acc[pl.ds(st, clen), :] = acc_new[c * clen:(c + 1) * clen]
                    den[pl.ds(st, clen), :] = den_new[c * clen:(c + 1) * clen]

            pending = scores(*plans[0])
            for nxt in plans[1:]:
                ahead = scores(*nxt)
                update(*pending)
                pending = ahead
            update(*pending)

        def class_body(ci, carry, ngrp=nblk // ublk, ncls=ncls, group=group):
            def grp_body(g, c):
                group(ci * ncls, g)
                return c
            return lax.fori_loop(0, ngrp, grp_body, carry)

        lax.fori_loop(0, dil // ncls, class_body, 0)

    for r in range(CLASSES):
        rows = pl.ds(r * cls_rows, cls_rows)
        onat[pl.ds(r, cls_rows, stride=CLASSES), :] = acc[rows, :] / den[rows, :]
    o_ref[...] = onat[...].astype(o_ref.dtype)


def _attn_a_call(qa, ka, va, bias):
    b, s, _ = qa.shape
    pad = WIN
    blk = pl.BlockSpec((None, s, LANES), lambda bb, p: (bb, 0, p))
    bias_spec = pl.BlockSpec((None,) + bias.shape[1:], lambda bb, p: (p, 0, 0, 0, 0))
    rows = pltpu.VMEM((s, LANES), F32)
    padded = pltpu.VMEM((pad + s, LANES), F32)
    return pl.pallas_call(
        functools.partial(_attn_a_kernel, seq=s, pad=pad),
        grid=(b, HEAD_PAIRS),
        in_specs=[blk, blk, blk, bias_spec],
        out_specs=blk,
        out_shape=jax.ShapeDtypeStruct((b, s, WIDTH_A), BF16),
        scratch_shapes=[rows, padded, padded, rows, rows, rows, rows, rows],
        compiler_params=pltpu.CompilerParams(
            dimension_semantics=("parallel", "parallel"), vmem_limit_bytes=VMEM_LIMIT),
        name="attn_a",
    )(qa, ka, va, bias)


def _t5_causal_bucket(dist):
    dist = np.asarray(dist, dtype=np.int64)
    max_exact = REL_BUCKETS // 2
    safe = np.maximum(dist, 1).astype(np.float32)
    large = max_exact + (np.log(safe / max_exact) / math.log(REL_MAX_DIST / max_exact)
                         * (REL_BUCKETS - max_exact)).astype(np.int64)
    large = np.minimum(large, REL_BUCKETS - 1)
    return np.where(dist < max_exact, dist, large).astype(np.int32)


def _attn_a_bias(rel_bias):
    span = 3 * WIN
    u = np.arange(span) - (WIN - 1)
    valid = (u >= 0) & (u <= WIN)
    tables = []
    for _, dil in DILATED_BRANCHES:
        bucket = _t5_causal_bucket(np.clip(WIN - u, 0, WIN) * dil)
        onehot = jnp.asarray(np.eye(REL_BUCKETS, dtype=np.float32)[bucket])
        vec = jnp.dot(onehot, rel_bias.astype(F32), precision=lax.Precision.HIGHEST)
        vec = jnp.where(jnp.asarray(valid)[:, None], vec * LOG2E, NEG).T
        flat = jnp.tile(vec, (1, WIN))[:, WIN - 1:WIN - 1 + WIN * (span - 1)]
        full = flat.reshape(N_HEADS_A, WIN, span - 1)[:, :, :2 * WIN]
        nchunk = CLASSES // dil
        clen = WIN // nchunk
        full = full.reshape(N_HEADS_A, clen, nchunk, 2, clen, nchunk)
        full = jnp.transpose(full, (0, 2, 1, 3, 5, 4)).reshape(N_HEADS_A, WIN, 2 * WIN)
        first = jnp.where(jnp.asarray(np.arange(2 * WIN) >= WIN)[None, None, :], full, NEG)
        tables.append(jnp.stack([full, first], axis=0))
    t = jnp.stack(tables, axis=0)
    t = t.reshape(len(DILATED_BRANCHES), 2, HEAD_PAIRS, 2 * WIN, 2 * WIN)
    return jnp.transpose(t, (2, 0, 1, 3, 4))


def _attn_b_kernel(qt_ref, k_ref, vt_ref, o_ref, acc0, acc1, m0, m1, s_even, s_odd, mb_even, mb_odd):
    qi = pl.program_id(2)
    tq = TQ_B
    heads = ((acc0, m0), (acc1, m1))
    for acc, m_s in heads:
        acc[...] = jnp.zeros_like(acc)
        m_s[...] = jnp.full(m_s.shape, NEG, F32)

    def scores(kb, s_buf, mb_buf):
        krows = pl.ds(pl.multiple_of(kb * tq, tq), tq)
        for hh in range(2):
            sl = slice(hh * LANES, (hh + 1) * LANES)
            st = _dot(k_ref[krows, sl], qt_ref[hh])
            s_buf[hh] = st
            mb_buf[hh] = jnp.max(st, axis=0, keepdims=True)

    def update(kb, s_buf, mb_buf, masked):
        for hh, (acc, m_s) in enumerate(heads):
            st = s_buf[hh]
            if masked:
                key = lax.broadcasted_iota(jnp.int32, (tq, tq), 0)
                qry = lax.broadcasted_iota(jnp.int32, (tq, tq), 1)
                st = jnp.where(key <= qry, st, NEG)
                m_blk = jnp.max(st, axis=0, keepdims=True)
            else:
                m_blk = mb_buf[hh]
            m_old = m_s[...]
            m_new = jnp.maximum(m_old, m_blk)
            alpha = jnp.exp2(m_old - m_new)
            pt = jnp.exp2(st - m_new).astype(BF16)
            acc[...] = alpha * acc[...] + _dot(vt_ref[kb, hh], pt)
            m_s[...] = m_new

    scores(0, s_even, mb_even)

    def pair_body(i, c):
        kb = 2 * i
        scores(kb + 1, s_odd, mb_odd)
        update(kb, s_even, mb_even, False)
        scores(kb + 2, s_even, mb_even)
        update(kb + 1, s_odd, mb_odd, False)
        return c

    npairs = qi // 2
    lax.fori_loop(0, npairs, pair_body, 0)

    @pl.when(qi % 2 == 1)
    def _():
        scores(qi, s_odd, mb_odd)
        update(qi - 1, s_even, mb_even, False)
        update(qi, s_odd, mb_odd, True)

    @pl.when(qi % 2 == 0)
    def _():
        update(qi, s_even, mb_even, True)

    outs = []
    for acc, _ in heads:
        a = acc[...]
        outs.append(a[:V_HEAD_DIM] / a[V_HEAD_DIM:V_HEAD_DIM + 1])
    o_ref[...] = jnp.concatenate(outs, axis=0).T.astype(o_ref.dtype)


def _attn_b_call(qbt, kb, vbt):
    b, s, _ = kb.shape
    tq = TQ_B
    nkb = s // tq
    qbt = qbt.reshape(b, nkb, HEAD_PAIRS, 2, LANES, tq)
    vbt = vbt.reshape(b, nkb, HEAD_PAIRS, 2, VT_ROWS, tq)
    return pl.pallas_call(
        _attn_b_kernel,
        grid=(b, HEAD_PAIRS, nkb),
        in_specs=[pl.BlockSpec((None, None, None, 2, LANES, tq),
                               lambda bb, p, i: (bb, i, p, 0, 0, 0)),
                  pl.BlockSpec((None, s, 2 * LANES), lambda bb, p, i: (bb, 0, p)),
                  pl.BlockSpec((None, nkb, None, 2, VT_ROWS, tq),
                               lambda bb, p, i: (bb, 0, p, 0, 0, 0))],
        out_specs=pl.BlockSpec((None, tq, LANES), lambda bb, p, i: (bb, i, p)),
        out_shape=jax.ShapeDtypeStruct((b, s, WIDTH_B), BF16),
        scratch_shapes=[pltpu.VMEM((VT_ROWS, tq), F32)] * 2 + [pltpu.VMEM((1, tq), F32)] * 2
        + [pltpu.VMEM((2, tq, tq), F32)] * 2 + [pltpu.VMEM((2, 1, tq), F32)] * 2,
        compiler_params=pltpu.CompilerParams(
            dimension_semantics=("parallel", "parallel", "arbitrary"),
            vmem_limit_bytes=VMEM_LIMIT),
        name="attn_b",
    )(qbt, kb, vbt)


def _ffn_kernel(x_ref, a_ref, b_ref, woa_ref, wob_ref, g_ref, w1_ref, w2_ref, o_ref):
    h = x_ref[...] + _dot(a_ref[...], woa_ref[...]) + _dot(b_ref[...], wob_ref[...])
    hn = (h * lax.rsqrt(jnp.mean(h * h, axis=-1, keepdims=True) + EPS) * g_ref[...]).astype(BF16)
    mlp = None
    for c in range(D_FF // FF_CHUNK):
        sl = slice(c * FF_CHUNK, (c + 1) * FF_CHUNK)
        hid = jnp.square(jnp.maximum(_dot(hn, w1_ref[:, sl]), 0.0)).astype(BF16)
        d = _dot(hid, w2_ref[sl, :])
        mlp = d if mlp is None else mlp + d
    o_ref[...] = h + mlp


def _ffn_call(x2, a2, b2, woa, wob, g, w1, w2):
    n = x2.shape[0]
    tm = TM_FFN
    row = lambda i: (i, 0)
    consts = (woa, wob, g, w1, w2)
    return pl.pallas_call(
        _ffn_kernel,
        grid=(n // tm,),
        in_specs=[pl.BlockSpec((tm, D_MODEL), row), pl.BlockSpec((tm, WIDTH_A), row),
                  pl.BlockSpec((tm, WIDTH_B), row)] + [_const_spec(c.shape) for c in consts],
        out_specs=pl.BlockSpec((tm, D_MODEL), row),
        out_shape=jax.ShapeDtypeStruct((n, D_MODEL), F32),
        compiler_params=pltpu.CompilerParams(
            dimension_semantics=("parallel",), vmem_limit_bytes=VMEM_LIMIT),
        name="ffn",
    )(x2, a2, b2, *consts)


def _pad_heads(w, n_heads, width):
    k = w.shape[0]
    w = w.reshape(k, n_heads, width)
    return jnp.pad(w, ((0, 0), (0, 0), (0, LANES - width))).reshape(k, n_heads * LANES)


def _lane_row(g, offset=0):
    return jnp.pad(g.astype(F32), (offset, LANES - offset - g.shape[0]))[None, :]


def kernel(x, positions, norm_mix_g, w_in, qnorm_a_g, knorm_a_g, rel_bias, cq_norm_g, ckv_norm_g,
           w_uq, w_ukv, qnorm_b_g, knorm_b_g, w_o, norm_ffn_g, w_ff1, w_ff2):
    b, s, d = x.shape
    n = b * s
    c0 = 3 * WIDTH_A
    c1 = c0 + Q_LORA_RANK
    c2 = c1 + KV_LORA_RANK
    w_qkv = w_in[:, :c0].astype(BF16)
    w_cq = w_in[:, c0:c1].astype(BF16)
    w_ckv = w_in[:, c1:c2].astype(BF16)
    w_kr = jnp.pad(w_in[:, c2:], ((0, 0), (QK_NOPE_DIM, LANES - QK_HEAD_DIM_B))).astype(BF16)
    w_uqt = _pad_heads(w_uq, N_HEADS_B, QK_HEAD_DIM_B).T.astype(BF16)
    w_ukv3 = w_ukv.reshape(KV_LORA_RANK, N_HEADS_B, QK_NOPE_DIM + V_HEAD_DIM)
    w_uk_p = _pad_heads(w_ukv3[:, :, :QK_NOPE_DIM].reshape(KV_LORA_RANK, -1),
                        N_HEADS_B, QK_NOPE_DIM).astype(BF16)
    ones_rows = VT_ROWS - V_HEAD_DIM
    w_uvt = jnp.transpose(w_ukv3[:, :, QK_NOPE_DIM:], (1, 2, 0))
    w_uvt = jnp.pad(w_uvt, ((0, 0), (0, ones_rows), (0, 0)))
    w_uvt = w_uvt.reshape(N_HEADS_B * VT_ROWS, KV_LORA_RANK).astype(BF16)
    inv_freq = 1.0 / (ROPE_THETA ** (jnp.arange(0, QK_ROPE_DIM, 2, dtype=F32) / QK_ROPE_DIM))
    ones_col = jnp.tile(jnp.concatenate([jnp.zeros((V_HEAD_DIM,), F32), jnp.ones((ones_rows,), F32)]),
                        N_HEADS_B)[:, None]
    g_mix = norm_mix_g.astype(F32)[None, :]
    head_sum = jnp.asarray(np.kron(np.eye(N_HEADS_A), np.ones((HEAD_DIM_A, HEAD_DIM_A))), BF16)
    consts_a = (
        g_mix, w_qkv, head_sum,
        jnp.tile(qnorm_a_g.astype(F32), N_HEADS_A)[None, :],
        jnp.tile(knorm_a_g.astype(F32), N_HEADS_A)[None, :],
    )
    tm = TM_PROJ
    gq_col = jnp.pad(qnorm_b_g.astype(F32), (0, LANES - QK_HEAD_DIM_B))[:, None]
    consts_b = (
        g_mix, w_cq, w_ckv, w_kr,
        cq_norm_g.astype(F32)[None, :], ckv_norm_g.astype(F32)[None, :],
        w_uqt, w_uk_p, w_uvt,
        jnp.broadcast_to(gq_col, (LANES, tm)), _lane_row(knorm_b_g[:QK_NOPE_DIM]),
        _lane_row(knorm_b_g[QK_NOPE_DIM:], QK_NOPE_DIM),
        _lane_row(jnp.tile(inv_freq, 2), QK_NOPE_DIM),
        jnp.broadcast_to(inv_freq[:, None], (QK_ROPE_DIM // 2, tm)),
        ones_col,
    )
    x2 = x.reshape(n, d)
    qa, ka, va = _proj_a_call(x, consts_a)
    qbt, kb, vbt = _proj_b_call(x2, positions.reshape(n, 1), positions.reshape(n // tm, 1, tm),
                                consts_b)

    out_a = _attn_a_call(qa, ka, va, _attn_a_bias(rel_bias))
    out_b = _attn_b_call(qbt, kb.reshape(b, s, -1), vbt)

    out = _ffn_call(x.reshape(n, d), out_a.reshape(n, -1), out_b.reshape(n, -1),
                    w_o[:WIDTH_A].astype(BF16), w_o[WIDTH_A:].astype(BF16),
                    norm_ffn_g.astype(F32)[None, :], w_ff1.astype(BF16), w_ff2.astype(BF16))
    return out.reshape(b, s, d)
```

```python
import functools
import math

import numpy as np
import jax
import jax.numpy as jnp
from jax import lax
from jax.experimental import pallas as pl
from jax.experimental.pallas import tpu as pltpu

F32 = jnp.float32
BF16 = jnp.bfloat16

D_MODEL = 1024
HEAD_DIM_A = 64
N_HEADS_A = 8
WIDTH_A = N_HEADS_A * HEAD_DIM_A
DILATED_BRANCHES = ((128, 1), (512, 4), (2048, 16))
N_HEADS_B = 8
QK_NOPE_DIM = 64
QK_ROPE_DIM = 32
V_HEAD_DIM = 64
QK_HEAD_DIM_B = QK_NOPE_DIM + QK_ROPE_DIM
Q_LORA_RANK = 768
KV_LORA_RANK = 256
WIDTH_B = N_HEADS_B * V_HEAD_DIM
ROPE_THETA = 10000.0
D_FF = 4 * D_MODEL
REL_BUCKETS = 32
REL_MAX_DIST = 2048
EPS = 1e-6

LANES = 128
HALF = LANES // 2
HEAD_PAIRS = N_HEADS_A // 2
WIN = 128
NEG = -1e30
VMEM_LIMIT = 56 * 1024 * 1024

TM_PROJ = 512
TM_FFN = 512
TQ_B = 512
TSUB_B = 256
FF_CHUNK = 1024
GROUP_A = 16
CLASSES = 4

LOG2E = math.log2(math.e)
SCALE_A = LOG2E / math.sqrt(HEAD_DIM_A)
SCALE_B = LOG2E / math.sqrt(QK_HEAD_DIM_B)

VT_ROWS = V_HEAD_DIM + 16

assert HEAD_DIM_A == HALF and V_HEAD_DIM == HALF
assert TM_PROJ == TQ_B


def _nt_dot(a, b):
    return lax.dot_general(a, b, (((1,), (1,)), ((), ())), preferred_element_type=F32)


def _dot(a, b):
    return jnp.dot(a, b, preferred_element_type=F32)


def _const_spec(shape):
    nd = len(shape)
    return pl.BlockSpec(shape, lambda *_: (0,) * nd, pipeline_mode=pl.Buffered(1))


def _rms_bf16(x, g_row):
    return (x * lax.rsqrt(jnp.mean(x * x, axis=-1, keepdims=True) + EPS) * g_row).astype(BF16)


def _proj_a_kernel(x_ref, gmix_ref, wqkv_ref, hsum_ref, gqa_ref, gka_ref, qa_ref, ka_ref, va_ref,
                   xs_ref):
    tm, d = x_ref.shape
    run = tm // CLASSES
    for c in range(d // LANES):
        xs_ref[c] = x_ref[:, c * LANES:(c + 1) * LANES]
    x = jnp.concatenate(
        [jnp.concatenate([xs_ref[c, pl.ds(r, run, stride=CLASSES), :] for r in range(CLASSES)], axis=0)
         for c in range(d // LANES)], axis=1)
    xn = _rms_bf16(x, gmix_ref[...])
    qkv = _dot(xn, wqkv_ref[...])

    def head_norm(y, g_row):
        ss = _dot((y * y).astype(BF16), hsum_ref[...])
        return y * lax.rsqrt(ss / HEAD_DIM_A + EPS) * g_row

    q = (head_norm(qkv[:, :WIDTH_A], gqa_ref[...]) * SCALE_A).astype(BF16)
    k = head_norm(qkv[:, WIDTH_A:2 * WIDTH_A], gka_ref[...]).astype(BF16)
    v = qkv[:, 2 * WIDTH_A:].astype(BF16)
    for r in range(CLASSES):
        rows = slice(r * run, (r + 1) * run)
        qa_ref[r] = q[rows]
        ka_ref[r] = k[rows]
        va_ref[r] = v[rows]


def _proj_a_call(x, consts):
    b, s, d = x.shape
    tm = TM_PROJ
    run = tm // CLASSES
    out = jax.ShapeDtypeStruct((b, CLASSES, s // CLASSES, WIDTH_A), BF16)
    outs = pl.pallas_call(
        _proj_a_kernel,
        grid=(b, s // tm),
        in_specs=[pl.BlockSpec((None, tm, d), lambda bb, i: (bb, i, 0))]
        + [_const_spec(c.shape) for c in consts],
        out_specs=[pl.BlockSpec((None, CLASSES, run, WIDTH_A), lambda bb, i: (bb, 0, i, 0))] * 3,
        out_shape=[out] * 3,
        scratch_shapes=[pltpu.VMEM((d // LANES, tm, LANES), F32)],
        compiler_params=pltpu.CompilerParams(
            dimension_semantics=("parallel", "parallel"), vmem_limit_bytes=VMEM_LIMIT),
        name="proj_a",
    )(x, *consts)
    return [o.reshape(b, s, WIDTH_A) for o in outs]


def _proj_b_kernel(x_ref, pos_ref, posr_ref, gmix_ref, wcq_ref, wckv_ref, wkr_ref,
                   gcq_ref, gckv_ref, wuqt_ref, wuk_ref, wuvt_ref,
                   gqbt_ref, gkbn_ref, gkbr_ref, invf_ref, invft_ref, ones_ref,
                   qt_ref, kb_ref, vt_ref):
    xn = _rms_bf16(x_ref[...], gmix_ref[...])
    cqn = _rms_bf16(_dot(xn, wcq_ref[...]), gcq_ref[...])
    ckvn = _rms_bf16(_dot(xn, wckv_ref[...]), gckv_ref[...])
    half = QK_ROPE_DIM // 2
    r0, r1, r2 = QK_NOPE_DIM, QK_NOPE_DIM + half, QK_HEAD_DIM_B

    qt = _nt_dot(wuqt_ref[...], cqn)
    ang_t = invft_ref[...] * posr_ref[...].astype(F32)
    cs_t = jnp.cos(ang_t)
    sn_t = jnp.sin(ang_t)
    for h in range(N_HEADS_B):
        qh = qt[h * LANES:(h + 1) * LANES]
        rq = lax.rsqrt(jnp.sum(qh * qh, axis=0, keepdims=True) / QK_HEAD_DIM_B + EPS)
        y = qh * rq * gqbt_ref[...]
        y1, y2 = y[r0:r1], y[r1:r2]
        out = jnp.concatenate([y[:r0], y1 * cs_t - y2 * sn_t, y2 * cs_t + y1 * sn_t, y[r2:]], axis=0)
        qt_ref[h * LANES:(h + 1) * LANES, :] = (out * SCALE_B).astype(BF16)

    vt_ref[...] = (_nt_dot(wuvt_ref[...], ckvn) + ones_ref[...]).astype(BF16)

    kn = _dot(ckvn, wuk_ref[...])
    kr = _dot(xn, wkr_ref[...])
    lane = lax.broadcasted_iota(jnp.int32, (1, LANES), 1)
    ang = pos_ref[...].astype(F32) * invf_ref[...]
    cs = jnp.cos(ang)
    sn = jnp.sin(ang)
    s_lo = jnp.where((lane >= r0) & (lane < r1), -sn, 0.0)
    s_hi = jnp.where((lane >= r1) & (lane < r2), sn, 0.0)
    ss_kr = jnp.sum(kr * kr, axis=-1, keepdims=True)
    krg = kr * gkbr_ref[...]
    kr_rot = krg * cs + pltpu.roll(krg, LANES - half, 1) * s_lo + pltpu.roll(krg, half, 1) * s_hi
    for h in range(N_HEADS_B):
        sl = slice(h * LANES, (h + 1) * LANES)
        kh = kn[:, sl]
        rk = lax.rsqrt((jnp.sum(kh * kh, axis=-1, keepdims=True) + ss_kr) / QK_HEAD_DIM_B + EPS)
        kb_ref[:, sl] = ((kh * gkbn_ref[...] + kr_rot) * rk).astype(BF16)


def _proj_b_call(x2, pos_col, pos_row, consts):
    n = x2.shape[0]
    tm = TM_PROJ
    row = lambda i: (i, 0)
    blk3 = lambda i: (i, 0, 0)
    wide = N_HEADS_B * LANES
    vt_rows = N_HEADS_B * VT_ROWS
    return pl.pallas_call(
        _proj_b_kernel,
        grid=(n // tm,),
        in_specs=[pl.BlockSpec((tm, D_MODEL), row), pl.BlockSpec((tm, 1), row),
                  pl.BlockSpec((None, 1, tm), blk3)] + [_const_spec(c.shape) for c in consts],
        out_specs=[pl.BlockSpec((None, wide, tm), blk3), pl.BlockSpec((tm, wide), row),
                   pl.BlockSpec((None, vt_rows, tm), blk3)],
        out_shape=[jax.ShapeDtypeStruct((n // tm, wide, tm), BF16),
                   jax.ShapeDtypeStruct((n, wide), BF16),
                   jax.ShapeDtypeStruct((n // tm, vt_rows, tm), BF16)],
        compiler_params=pltpu.CompilerParams(
            dimension_semantics=("parallel",), vmem_limit_bytes=VMEM_LIMIT),
        name="proj_b",
    )(x2, pos_col, pos_row, *consts)


def _attn_a_kernel(q_ref, k_ref, v_ref, bias_ref, o_ref,
                   qf, kf, vf, acc, den, m0, m1, onat, *, seq, pad):
    lane = lax.broadcasted_iota(jnp.int32, (1, LANES), 1)
    first_head = lane < HALF
    qf[...] = q_ref[...].astype(F32)
    zeros = jnp.zeros((pad, LANES), F32)
    kf[pl.ds(0, pad), :] = zeros
    vf[pl.ds(0, pad), :] = zeros
    kf[pl.ds(pad, seq), :] = k_ref[...].astype(F32)
    vf[pl.ds(pad, seq), :] = v_ref[...].astype(F32)
    acc[...] = jnp.zeros_like(acc)
    den[...] = jnp.zeros_like(den)
    for m_s in (m0, m1):
        m_s[...] = jnp.full(m_s.shape, NEG, F32)

    cls_rows = seq // CLASSES

    for bi, (window, dil) in enumerate(DILATED_BRANCHES):
        assert window // dil == WIN
        nchunk = max(CLASSES // dil, 1)
        clen = WIN // nchunk
        stride = max(dil // CLASSES, 1)
        step = stride * clen
        nblk = seq // (dil * WIN)
        ublk = min(GROUP_A, nblk)
        ncls = GROUP_A // ublk

        def run(start, clen=clen, stride=stride):
            return pl.ds(start, clen, stride=stride) if stride > 1 else pl.ds(start, clen)

        def gather(ref, starts, run=run):
            return jnp.concatenate([ref[run(st), :] for st in starts], axis=0)

        def group(r0, g, bi=bi, dil=dil, nchunk=nchunk, clen=clen, stride=stride, step=step,
                  ublk=ublk, ncls=ncls, run=run, gather=gather):
            plans = []
            for cc in range(ncls):
                rd = r0 + cc
                if stride == 1:
                    bases = [pl.multiple_of((dil * c + rd) * cls_rows + step * ublk * g, 8)
                             for c in range(nchunk)]
                else:
                    bases = [(rd % CLASSES) * cls_rows + rd // CLASSES + step * ublk * g]
                for u in range(ublk):
                    plans.append((bases, u))

            def scores(bases, u):
                qstarts = [b0 + step * u for b0 in bases]
                kstarts = [pad + b0 + step * (u - 1) for b0 in bases]
                q = gather(qf, qstarts)
                qq = jnp.concatenate([jnp.where(first_head, q, 0.0),
                                      jnp.where(first_head, 0.0, q)], axis=0).astype(BF16)
                kk = jnp.concatenate([gather(kf, kstarts),
                                      gather(kf, [k0 + step for k0 in kstarts])],
                                     axis=0).astype(BF16)
                vv = jnp.concatenate([gather(vf, kstarts),
                                      gather(vf, [k0 + step for k0 in kstarts])],
                                     axis=0).astype(BF16)
                first_blk = jnp.where(g == 0, 1, 0) if u == 0 else 0
                s = _nt_dot(qq, kk) + bias_ref[bi, first_blk]
                return qstarts, s, jnp.concatenate([vv, jnp.ones_like(vv)], axis=1)

            def update(qstarts, s, vext):
                ps, alphas = [], []
                for hh, m_s in enumerate((m0, m1)):
                    sh = s[hh * WIN:(hh + 1) * WIN]
                    m_old = gather(m_s, qstarts)
                    m_new = jnp.maximum(m_old, jnp.max(sh, axis=-1, keepdims=True))
                    alphas.append(jnp.exp2(m_old - m_new))
                    ps.append(jnp.exp2(sh - jnp.concatenate([m_new, m_new], axis=1)).astype(BF16))
                    for c, st in enumerate(qstarts):
                        m_s[run(st), :] = m_new[c * clen:(c + 1) * clen]
                res = _dot(jnp.concatenate(ps, axis=0), vext)
                alpha = jnp.where(first_head, alphas[0], alphas[1])
                acc_new = alpha * gather(acc, qstarts) + jnp.where(
                    first_head, res[:WIN, :LANES], res[WIN:, :LANES])
                den_new = alpha * gather(den, qstarts) + jnp.where(
                    first_head, res[:WIN, LANES:], res[WIN:, LANES:])
                for c, st in enumerate(qstarts):
                    acc[run(st), :] = acc_new[c * clen:(c + 1) * clen]
                    den[run(st), :] = den_new[c * clen:(c + 1) * clen]

            pending = scores(*plans[0])
            for nxt in plans[1:]:
                ahead = scores(*nxt)
                update(*pending)
                pending = ahead
            update(*pending)

        def class_body(ci, carry, ngrp=nblk // ublk, ncls=ncls, group=group):
            def grp_body(g, c):
                group(ci * ncls, g)
                return c
            return lax.fori_loop(0, ngrp, grp_body, carry)

        lax.fori_loop(0, dil // ncls, class_body, 0)

    for r in range(CLASSES):
        rows = pl.ds(r * cls_rows, cls_rows)
        onat[pl.ds(r, cls_rows, stride=CLASSES), :] = acc[rows, :] / den[rows, :]
    o_ref[...] = onat[...].astype(o_ref.dtype)


def _attn_a_call(qa, ka, va, bias):
    b, s, _ = qa.shape
    pad = WIN * DILATED_BRANCHES[-1][1] // CLASSES
    blk = pl.BlockSpec((None, s, LANES), lambda bb, p: (bb, 0, p))
    bias_spec = pl.BlockSpec((None,) + bias.shape[1:], lambda bb, p: (p, 0, 0, 0, 0))
    rows = pltpu.VMEM((s, LANES), F32)
    padded = pltpu.VMEM((pad + s, LANES), F32)
    return pl.pallas_call(
        functools.partial(_attn_a_kernel, seq=s, pad=pad),
        grid=(b, HEAD_PAIRS),
        in_specs=[blk, blk, blk, bias_spec],
        out_specs=blk,
        out_shape=jax.ShapeDtypeStruct((b, s, WIDTH_A), BF16),
        scratch_shapes=[rows, padded, padded, rows, rows, rows, rows, rows],
        compiler_params=pltpu.CompilerParams(
            dimension_semantics=("parallel", "parallel"), vmem_limit_bytes=VMEM_LIMIT),
        name="attn_a",
    )(qa, ka, va, bias)


def _t5_causal_bucket(dist):
    dist = np.asarray(dist, dtype=np.int64)
    max_exact = REL_BUCKETS // 2
    safe = np.maximum(dist, 1).astype(np.float32)
    large = max_exact + (np.log(safe / max_exact) / math.log(REL_MAX_DIST / max_exact)
                         * (REL_BUCKETS - max_exact)).astype(np.int64)
    large = np.minimum(large, REL_BUCKETS - 1)
    return np.where(dist < max_exact, dist, large).astype(np.int32)


def _attn_a_bias(rel_bias):
    span = 3 * WIN
    u = np.arange(span) - (WIN - 1)
    valid = (u >= 0) & (u <= WIN)
    tables = []
    for _, dil in DILATED_BRANCHES:
        bucket = _t5_causal_bucket(np.clip(WIN - u, 0, WIN) * dil)
        onehot = jnp.asarray(np.eye(REL_BUCKETS, dtype=np.float32)[bucket])
        vec = jnp.dot(onehot, rel_bias.astype(F32), precision=lax.Precision.HIGHEST)
        vec = jnp.where(jnp.asarray(valid)[:, None], vec * LOG2E, NEG).T
        flat = jnp.tile(vec, (1, WIN))[:, WIN - 1:WIN - 1 + WIN * (span - 1)]
        full = flat.reshape(N_HEADS_A, WIN, span - 1)[:, :, :2 * WIN]
        nchunk = max(CLASSES // dil, 1)
        clen = WIN // nchunk
        full = full.reshape(N_HEADS_A, clen, nchunk, 2, clen, nchunk)
        full = jnp.transpose(full, (0, 2, 1, 3, 5, 4)).reshape(N_HEADS_A, WIN, 2 * WIN)
        first = jnp.where(jnp.asarray(np.arange(2 * WIN) >= WIN)[None, None, :], full, NEG)
        tables.append(jnp.stack([full, first], axis=0))
    t = jnp.stack(tables, axis=0)
    t = t.reshape(len(DILATED_BRANCHES), 2, HEAD_PAIRS, 2 * WIN, 2 * WIN)
    return jnp.transpose(t, (2, 0, 1, 3, 4))


def _attn_b_kernel(qt_ref, k_ref, vt_ref, o_ref, acc0, acc1, m0, m1, s_even, s_odd, mb_even, mb_odd):
    qi = pl.program_id(2)
    tq = TQ_B
    heads = ((acc0, m0), (acc1, m1))
    for acc, m_s in heads:
        acc[...] = jnp.zeros_like(acc)
        m_s[...] = jnp.full(m_s.shape, NEG, F32)

    def scores(kb, s_buf, mb_buf):
        krows = pl.ds(pl.multiple_of(kb * tq, tq), tq)
        for hh in range(2):
            sl = slice(hh * LANES, (hh + 1) * LANES)
            st = _dot(k_ref[krows, sl], qt_ref[hh])
            s_buf[hh] = st
            mb_buf[hh] = jnp.max(st, axis=0, keepdims=True)

    def update(kb, s_buf, mb_buf, masked):
        for hh, (acc, m_s) in enumerate(heads):
            st = s_buf[hh]
            if masked:
                key = lax.broadcasted_iota(jnp.int32, (tq, tq), 0)
                qry = lax.broadcasted_iota(jnp.int32, (tq, tq), 1)
                st = jnp.where(key <= qry, st, NEG)
                m_blk = jnp.max(st, axis=0, keepdims=True)
            else:
                m_blk = mb_buf[hh]
            m_old = m_s[...]
            m_new = jnp.maximum(m_old, m_blk)
            alpha = jnp.exp2(m_old - m_new)
            pt = jnp.exp2(st - m_new).astype(BF16)
            acc[...] = alpha * acc[...] + _dot(vt_ref[kb, hh], pt)
            m_s[...] = m_new

    scores(0, s_even, mb_even)

    def pair_body(i, c):
        kb = 2 * i
        scores(kb + 1, s_odd, mb_odd)
        update(kb, s_even, mb_even, False)
        scores(kb + 2, s_even, mb_even)
        update(kb + 1, s_odd, mb_odd, False)
        return c

    npairs = qi // 2
    lax.fori_loop(0, npairs, pair_body, 0)

    @pl.when(qi % 2 == 1)
    def _():
        scores(qi, s_odd, mb_odd)
        update(qi - 1, s_even, mb_even, False)
        update(qi, s_odd, mb_odd, True)

    @pl.when(qi % 2 == 0)
    def _():
        update(qi, s_even, mb_even, True)

    outs = []
    for acc, _ in heads:
        a = acc[...]
        outs.append(a[:V_HEAD_DIM] / a[V_HEAD_DIM:V_HEAD_DIM + 1])
    o_ref[...] = jnp.concatenate(outs, axis=0).T.astype(o_ref.dtype)


def _attn_b_call(qbt, kb, vbt):
    b, s, _ = kb.shape
    tq = TQ_B
    nkb = s // tq
    qbt = qbt.reshape(b, nkb, HEAD_PAIRS, 2, LANES, tq)
    vbt = vbt.reshape(b, nkb, HEAD_PAIRS, 2, VT_ROWS, tq)
    return pl.pallas_call(
        _attn_b_kernel,
        grid=(b, HEAD_PAIRS, nkb),
        in_specs=[pl.BlockSpec((None, None, None, 2, LANES, tq),
                               lambda bb, p, i: (bb, i, p, 0, 0, 0)),
                  pl.BlockSpec((None, s, 2 * LANES), lambda bb, p, i: (bb, 0, p)),
                  pl.BlockSpec((None, nkb, None, 2, VT_ROWS, tq),
                               lambda bb, p, i: (bb, 0, p, 0, 0, 0))],
        out_specs=pl.BlockSpec((None, tq, LANES), lambda bb, p, i: (bb, i, p)),
        out_shape=jax.ShapeDtypeStruct((b, s, WIDTH_B), BF16),
        scratch_shapes=[pltpu.VMEM((VT_ROWS, tq), F32)] * 2 + [pltpu.VMEM((1, tq), F32)] * 2
        + [pltpu.VMEM((2, tq, tq), F32)] * 2 + [pltpu.VMEM((2, 1, tq), F32)] * 2,
        compiler_params=pltpu.CompilerParams(
            dimension_semantics=("parallel", "parallel", "arbitrary"),
            vmem_limit_bytes=VMEM_LIMIT),
        name="attn_b",
    )(qbt, kb, vbt)


def _ffn_kernel(x_ref, a_ref, b_ref, woa_ref, wob_ref, g_ref, w1_ref, w2_ref, o_ref):
    h = x_ref[...] + _dot(a_ref[...], woa_ref[...]) + _dot(b_ref[...], wob_ref[...])
    hn = (h * lax.rsqrt(jnp.mean(h * h, axis=-1, keepdims=True) + EPS) * g_ref[...]).astype(BF16)
    mlp = None
    for c in range(D_FF // FF_CHUNK):
        sl = slice(c * FF_CHUNK, (c + 1) * FF_CHUNK)
        hid = jnp.square(jnp.maximum(_dot(hn, w1_ref[:, sl]), 0.0)).astype(BF16)
        d = _dot(hid, w2_ref[sl, :])
        mlp = d if mlp is None else mlp + d
    o_ref[...] = h + mlp


def _ffn_call(x2, a2, b2, woa, wob, g, w1, w2):
    n = x2.shape[0]
    tm = TM_FFN
    row = lambda i: (i, 0)
    consts = (woa, wob, g, w1, w2)
    return pl.pallas_call(
        _ffn_kernel,
        grid=(n // tm,),
        in_specs=[pl.BlockSpec((tm, D_MODEL), row), pl.BlockSpec((tm, WIDTH_A), row),
                  pl.BlockSpec((tm, WIDTH_B), row)] + [_const_spec(c.shape) for c in consts],
        out_specs=pl.BlockSpec((tm, D_MODEL), row),
        out_shape=jax.ShapeDtypeStruct((n, D_MODEL), F32),
        compiler_params=pltpu.CompilerParams(
            dimension_semantics=("parallel",), vmem_limit_bytes=VMEM_LIMIT),
        name="ffn",
    )(x2, a2, b2, *consts)


def _pad_heads(w, n_heads, width):
    k = w.shape[0]
    w = w.reshape(k, n_heads, width)
    return jnp.pad(w, ((0, 0), (0, 0), (0, LANES - width))).reshape(k, n_heads * LANES)


def _lane_row(g, offset=0):
    return jnp.pad(g.astype(F32), (offset, LANES - offset - g.shape[0]))[None, :]


def kernel(x, positions, norm_mix_g, w_in, qnorm_a_g, knorm_a_g, rel_bias, cq_norm_g, ckv_norm_g,
           w_uq, w_ukv, qnorm_b_g, knorm_b_g, w_o, norm_ffn_g, w_ff1, w_ff2):
    b, s, d = x.shape
    n = b * s
    c0 = 3 * WIDTH_A
    c1 = c0 + Q_LORA_RANK
    c2 = c1 + KV_LORA_RANK
    w_qkv = w_in[:, :c0].astype(BF16)
    w_cq = w_in[:, c0:c1].astype(BF16)
    w_ckv = w_in[:, c1:c2].astype(BF16)
    w_kr = jnp.pad(w_in[:, c2:], ((0, 0), (QK_NOPE_DIM, LANES - QK_HEAD_DIM_B))).astype(BF16)
    w_uqt = _pad_heads(w_uq, N_HEADS_B, QK_HEAD_DIM_B).T.astype(BF16)
    w_ukv3 = w_ukv.reshape(KV_LORA_RANK, N_HEADS_B, QK_NOPE_DIM + V_HEAD_DIM)
    w_uk_p = _pad_heads(w_ukv3[:, :, :QK_NOPE_DIM].reshape(KV_LORA_RANK, -1),
                        N_HEADS_B, QK_NOPE_DIM).astype(BF16)
    ones_rows = VT_ROWS - V_HEAD_DIM
    w_uvt = jnp.transpose(w_ukv3[:, :, QK_NOPE_DIM:], (1, 2, 0))
    w_uvt = jnp.pad(w_uvt, ((0, 0), (0, ones_rows), (0, 0)))
    w_uvt = w_uvt.reshape(N_HEADS_B * VT_ROWS, KV_LORA_RANK).astype(BF16)
    inv_freq = 1.0 / (ROPE_THETA ** (jnp.arange(0, QK_ROPE_DIM, 2, dtype=F32) / QK_ROPE_DIM))
    ones_col = jnp.tile(jnp.concatenate([jnp.zeros((V_HEAD_DIM,), F32), jnp.ones((ones_rows,), F32)]),
                        N_HEADS_B)[:, None]
    g_mix = norm_mix_g.astype(F32)[None, :]
    head_sum = jnp.asarray(np.kron(np.eye(N_HEADS_A), np.ones((HEAD_DIM_A, HEAD_DIM_A))), BF16)
    consts_a = (
        g_mix, w_qkv, head_sum,
        jnp.tile(qnorm_a_g.astype(F32), N_HEADS_A)[None, :],
        jnp.tile(knorm_a_g.astype(F32), N_HEADS_A)[None, :],
    )
    tm = TM_PROJ
    gq_col = jnp.pad(qnorm_b_g.astype(F32), (0, LANES - QK_HEAD_DIM_B))[:, None]
    consts_b = (
        g_mix, w_cq, w_ckv, w_kr,
        cq_norm_g.astype(F32)[None, :], ckv_norm_g.astype(F32)[None, :],
        w_uqt, w_uk_p, w_uvt,
        jnp.broadcast_to(gq_col, (LANES, tm)), _lane_row(knorm_b_g[:QK_NOPE_DIM]),
        _lane_row(knorm_b_g[QK_NOPE_DIM:], QK_NOPE_DIM),
        _lane_row(jnp.tile(inv_freq, 2), QK_NOPE_DIM),
        jnp.broadcast_to(inv_freq[:, None], (QK_ROPE_DIM // 2, tm)),
        ones_col,
    )
    x2 = x.reshape(n, d)
    qa, ka, va = _proj_a_call(x, consts_a)
    qbt, kb, vbt = _proj_b_call(x2, positions.reshape(n, 1), positions.reshape(n // tm, 1, tm),
                                consts_b)

    out_a = _attn_a_call(qa, ka, va, _attn_a_bias(rel_bias))
    out_b = _attn_b_call(qbt, kb.reshape(b, s, -1), vbt)

    out = _ffn_call(x.reshape(n, d), out_a.reshape(n, -1), out_b.reshape(n, -1),
                    w_o[:WIDTH_A].astype(BF16), w_o[WIDTH_A:].astype(BF16),
                    norm_ffn_g.astype(F32)[None, :], w_ff1.astype(BF16), w_ff2.astype(BF16))
    return out.reshape(b, s, d)
```

```python
import functools
import math

import numpy as np
import jax
import jax.numpy as jnp
from jax import lax
from jax.experimental import pallas as pl
from jax.experimental.pallas import tpu as pltpu

F32 = jnp.float32
BF16 = jnp.bfloat16

D_MODEL = 1024
HEAD_DIM_A = 64
N_HEADS_A = 8
WIDTH_A = N_HEADS_A * HEAD_DIM_A
DILATED_BRANCHES = ((128, 1), (512, 4), (2048, 16))
N_HEADS_B = 8
QK_NOPE_DIM = 64
QK_ROPE_DIM = 32
V_HEAD_DIM = 64
QK_HEAD_DIM_B = QK_NOPE_DIM + QK_ROPE_DIM
Q_LORA_RANK = 768
KV_LORA_RANK = 256
WIDTH_B = N_HEADS_B * V_HEAD_DIM
ROPE_THETA = 10000.0
D_FF = 4 * D_MODEL
REL_BUCKETS = 32
REL_MAX_DIST = 2048
EPS = 1e-6

LANES = 128
HALF = LANES // 2
HEAD_PAIRS = N_HEADS_A // 2
WIN = 128
NEG = -1e30
VMEM_LIMIT = 56 * 1024 * 1024

TM_PROJ = 512
TM_FFN = 512
TQ_B = 512
TSUB_B = 256
FF_CHUNK = 1024
GROUP_A = 16
CLASSES = 4

LOG2E = math.log2(math.e)
SCALE_A = LOG2E / math.sqrt(HEAD_DIM_A)
SCALE_B = LOG2E / math.sqrt(QK_HEAD_DIM_B)

VT_ROWS = V_HEAD_DIM + 16

assert HEAD_DIM_A == HALF and V_HEAD_DIM == HALF
assert TM_PROJ == TQ_B


def _nt_dot(a, b):
    return lax.dot_general(a, b, (((1,), (1,)), ((), ())), preferred_element_type=F32)


def _dot(a, b):
    return jnp.dot(a, b, preferred_element_type=F32)


def _const_spec(shape):
    nd = len(shape)
    return pl.BlockSpec(shape, lambda *_: (0,) * nd, pipeline_mode=pl.Buffered(1))


def _rms_bf16(x, g_row):
    return (x * lax.rsqrt(jnp.mean(x * x, axis=-1, keepdims=True) + EPS) * g_row).astype(BF16)


def _proj_a_kernel(x_ref, gmix_ref, wqkv_ref, hsum_ref, gqa_ref, gka_ref, qa_ref, ka_ref, va_ref,
                   xs_ref):
    tm, d = x_ref.shape
    run = tm // CLASSES
    for c in range(d // LANES):
        xs_ref[c] = x_ref[:, c * LANES:(c + 1) * LANES]

    def head_sumsq(y):
        return _dot((y * y).astype(BF16), hsum_ref[...])

    sub = run // 2
    for h in range(2):
        x = jnp.concatenate(
            [jnp.concatenate([xs_ref[c, pl.ds(r + h * sub * CLASSES, sub, stride=CLASSES), :]
                              for r in range(CLASSES)], axis=0)
             for c in range(d // LANES)], axis=1)
        xn = _rms_bf16(x, gmix_ref[...])
        q = _dot(xn, wqkv_ref[:, :WIDTH_A])
        k = _dot(xn, wqkv_ref[:, WIDTH_A:2 * WIDTH_A])
        ssq = head_sumsq(q)
        v = _dot(xn, wqkv_ref[:, 2 * WIDTH_A:]).astype(BF16)
        ssk = head_sumsq(k)
        q = (q * lax.rsqrt(ssq / HEAD_DIM_A + EPS) * gqa_ref[...] * SCALE_A).astype(BF16)
        k = (k * lax.rsqrt(ssk / HEAD_DIM_A + EPS) * gka_ref[...]).astype(BF16)
        for r in range(CLASSES):
            src = slice(r * sub, (r + 1) * sub)
            dst = slice(h * sub, (h + 1) * sub)
            qa_ref[r, dst] = q[src]
            ka_ref[r, dst] = k[src]
            va_ref[r, dst] = v[src]


def _proj_a_call(x, consts):
    b, s, d = x.shape
    tm = TM_PROJ
    run = tm // CLASSES
    out = jax.ShapeDtypeStruct((b, CLASSES, s // CLASSES, WIDTH_A), BF16)
    outs = pl.pallas_call(
        _proj_a_kernel,
        grid=(b, s // tm),
        in_specs=[pl.BlockSpec((None, tm, d), lambda bb, i: (bb, i, 0))]
        + [_const_spec(c.shape) for c in consts],
        out_specs=[pl.BlockSpec((None, CLASSES, run, WIDTH_A), lambda bb, i: (bb, 0, i, 0))] * 3,
        out_shape=[out] * 3,
        scratch_shapes=[pltpu.VMEM((d // LANES, tm, LANES), F32)],
        compiler_params=pltpu.CompilerParams(
            dimension_semantics=("parallel", "parallel"), vmem_limit_bytes=VMEM_LIMIT),
        name="proj_a",
    )(x, *consts)
    return [o.reshape(b, s, WIDTH_A) for o in outs]


def _proj_b_kernel(x_ref, posr_ref, gmix_ref, wcq_ref, wckv_ref, wkr_ref,
                   gcq_ref, gckv_ref, wuqt_ref, wuk_ref, wuvt_ref,
                   gqbt_ref, gkbn_ref, gkbr_ref, invft_ref, ones_ref,
                   qt_ref, kb_ref, vt_ref):
    tm = x_ref.shape[0]
    xn = _rms_bf16(x_ref[...], gmix_ref[...])
    cq = _dot(xn, wcq_ref[...])
    ckv = _dot(xn, wckv_ref[...])
    kr = _dot(xn, wkr_ref[...])
    half = QK_ROPE_DIM // 2
    r0, r1, r2 = QK_NOPE_DIM, QK_NOPE_DIM + half, QK_HEAD_DIM_B
    ang_t = invft_ref[...] * posr_ref[...].astype(F32)
    cs_t = jnp.cos(ang_t)
    sn_t = jnp.sin(ang_t)

    ckvn = _rms_bf16(ckv, gckv_ref[...])
    kn = _dot(ckvn, wuk_ref[...])
    vt_ref[...] = (_nt_dot(wuvt_ref[...], ckvn) + ones_ref[...]).astype(BF16)

    cqn = _rms_bf16(cq, gcq_ref[...])
    qt = _nt_dot(wuqt_ref[...], cqn)

    zt = jnp.zeros((r0, tm), F32)
    zb = jnp.zeros((LANES - r2, tm), F32)
    cs = jnp.concatenate([zt, cs_t, cs_t, zb], axis=0).T
    sn = jnp.concatenate([zt, -sn_t, sn_t, zb], axis=0).T
    ss_kr = jnp.sum(kr * kr, axis=-1, keepdims=True)
    krg = kr * gkbr_ref[...]
    kr_rot = krg * cs + (pltpu.roll(krg, LANES - half, 1) + pltpu.roll(krg, half, 1)) * sn
    for h in range(N_HEADS_B):
        sl = slice(h * LANES, (h + 1) * LANES)
        kh = kn[:, sl]
        rk = lax.rsqrt((jnp.sum(kh * kh, axis=-1, keepdims=True) + ss_kr) / QK_HEAD_DIM_B + EPS)
        kb_ref[:, sl] = ((kh * gkbn_ref[...] + kr_rot) * rk).astype(BF16)

    for h in range(N_HEADS_B):
        qh = qt[h * LANES:(h + 1) * LANES]
        rq = lax.rsqrt(jnp.sum(qh * qh, axis=0, keepdims=True) / QK_HEAD_DIM_B + EPS)
        y = qh * rq * gqbt_ref[...]
        y1, y2 = y[r0:r1], y[r1:r2]
        out = jnp.concatenate([y[:r0], y1 * cs_t - y2 * sn_t, y2 * cs_t + y1 * sn_t, y[r2:]], axis=0)
        qt_ref[h * LANES:(h + 1) * LANES, :] = (out * SCALE_B).astype(BF16)


def _proj_b_call(x2, pos_row, consts):
    n = x2.shape[0]
    tm = TM_PROJ
    row = lambda i: (i, 0)
    blk3 = lambda i: (i, 0, 0)
    wide = N_HEADS_B * LANES
    vt_rows = N_HEADS_B * VT_ROWS
    return pl.pallas_call(
        _proj_b_kernel,
        grid=(n // tm,),
        in_specs=[pl.BlockSpec((tm, D_MODEL), row),
                  pl.BlockSpec((None, 1, tm), blk3)] + [_const_spec(c.shape) for c in consts],
        out_specs=[pl.BlockSpec((None, wide, tm), blk3), pl.BlockSpec((tm, wide), row),
                   pl.BlockSpec((None, vt_rows, tm), blk3)],
        out_shape=[jax.ShapeDtypeStruct((n // tm, wide, tm), BF16),
                   jax.ShapeDtypeStruct((n, wide), BF16),
                   jax.ShapeDtypeStruct((n // tm, vt_rows, tm), BF16)],
        compiler_params=pltpu.CompilerParams(
            dimension_semantics=("parallel",), vmem_limit_bytes=VMEM_LIMIT),
        name="proj_b",
    )(x2, pos_row, *consts)


def _attn_a_kernel(q_ref, k_ref, v_ref, bias_ref, o_ref,
                   qf, kf, vf, acc, den, m0, m1, onat, *, seq, pad):
    lane = lax.broadcasted_iota(jnp.int32, (1, LANES), 1)
    first_head = lane < HALF
    qf[...] = q_ref[...].astype(F32)
    zeros = jnp.zeros((pad, LANES), F32)
    kf[pl.ds(0, pad), :] = zeros
    vf[pl.ds(0, pad), :] = zeros
    kf[pl.ds(pad, seq), :] = k_ref[...].astype(F32)
    vf[pl.ds(pad, seq), :] = v_ref[...].astype(F32)
    acc[...] = jnp.zeros_like(acc)
    den[...] = jnp.zeros_like(den)
    for m_s in (m0, m1):
        m_s[...] = jnp.full(m_s.shape, NEG, F32)

    cls_rows = seq // CLASSES

    for bi, (window, dil) in enumerate(DILATED_BRANCHES):
        assert window // dil == WIN
        nchunk = max(CLASSES // dil, 1)
        clen = WIN // nchunk
        stride = max(dil // CLASSES, 1)
        step = stride * clen
        nblk = seq // (dil * WIN)
        ublk = min(GROUP_A, nblk)
        ncls = GROUP_A // ublk

        def run(start, clen=clen, stride=stride):
            return pl.ds(start, clen, stride=stride) if stride > 1 else pl.ds(start, clen)

        def gather(ref, starts, run=run):
            return jnp.concatenate([ref[run(st), :] for st in starts], axis=0)

        def group(r0, g, bi=bi, dil=dil, nchunk=nchunk, clen=clen, stride=stride, step=step,
                  ublk=ublk, ncls=ncls, run=run, gather=gather):
            plans = []
            for cc in range(ncls):
                rd = r0 + cc
                if stride == 1:
                    bases = [pl.multiple_of((dil * c + rd) * cls_rows + step * ublk * g, 8)
                             for c in range(nchunk)]
                else:
                    bases = [(rd % CLASSES) * cls_rows + rd // CLASSES + step * ublk * g]
                for u in range(ublk):
                    plans.append((bases, u))

            def scores(bases, u):
                qstarts = [b0 + step * u for b0 in bases]
                kstarts = [pad + b0 + step * (u - 1) for b0 in bases]
                q = gather(qf, qstarts)
                qq = jnp.concatenate([jnp.where(first_head, q, 0.0),
                                      jnp.where(first_head, 0.0, q)], axis=0).astype(BF16)
                kk = jnp.concatenate([gather(kf, kstarts),
                                      gather(kf, [k0 + step for k0 in kstarts])],
                                     axis=0).astype(BF16)
                vv = jnp.concatenate([gather(vf, kstarts),
                                      gather(vf, [k0 + step for k0 in kstarts])],
                                     axis=0).astype(BF16)
                first_blk = jnp.where(g == 0, 1, 0) if u == 0 else 0
                s = _nt_dot(qq, kk) + bias_ref[bi, first_blk]
                return qstarts, s, jnp.concatenate([vv, jnp.ones_like(vv)], axis=1)

            def update(qstarts, s, vext):
                ps, alphas = [], []
                for hh, m_s in enumerate((m0, m1)):
                    sh = s[hh * WIN:(hh + 1) * WIN]
                    m_old = gather(m_s, qstarts)
                    m_new = jnp.maximum(m_old, jnp.max(sh, axis=-1, keepdims=True))
                    alphas.append(jnp.exp2(m_old - m_new))
                    ps.append(jnp.exp2(sh - jnp.concatenate([m_new, m_new], axis=1)).astype(BF16))
                    for c, st in enumerate(qstarts):
                        m_s[run(st), :] = m_new[c * clen:(c + 1) * clen]
                res = _dot(jnp.concatenate(ps, axis=0), vext)
                alpha = jnp.where(first_head, alphas[0], alphas[1])
                acc_new = alpha * gather(acc, qstarts) + jnp.where(
                    first_head, res[:WIN, :LANES], res[WIN:, :LANES])
                den_new = alpha * gather(den, qstarts) + jnp.where(
                    first_head, res[:WIN, LANES:], res[WIN:, LANES:])
                for c, st in enumerate(qstarts):
                    acc[run(st), :] = acc_new[c * clen:(c + 1) * clen]
                    den[run(st), :] = den_new[c * clen:(c + 1) * clen]

            pending = scores(*plans[0])
            for nxt in plans[1:]:
                ahead = scores(*nxt)
                update(*pending)
                pending = ahead
            update(*pending)

        def class_body(ci, carry, ngrp=nblk // ublk, ncls=ncls, group=group):
            def grp_body(g, c):
                group(ci * ncls, g)
                return c
            return lax.fori_loop(0, ngrp, grp_body, carry)

        lax.fori_loop(0, dil // ncls, class_body, 0)

    for r in range(CLASSES):
        rows = pl.ds(r * cls_rows, cls_rows)
        onat[pl.ds(r, cls_rows, stride=CLASSES), :] = acc[rows, :] / den[rows, :]
    o_ref[...] = onat[...].astype(o_ref.dtype)


def _attn_a_call(qa, ka, va, bias):
    b, s, _ = qa.shape
    pad = WIN * DILATED_BRANCHES[-1][1] // CLASSES
    blk = pl.BlockSpec((None, s, LANES), lambda bb, p: (bb, 0, p))
    bias_spec = pl.BlockSpec((None,) + bias.shape[1:], lambda bb, p: (p, 0, 0, 0, 0))
    rows = pltpu.VMEM((s, LANES), F32)
    padded = pltpu.VMEM((pad + s, LANES), F32)
    return pl.pallas_call(
        functools.partial(_attn_a_kernel, seq=s, pad=pad),
        grid=(b, HEAD_PAIRS),
        in_specs=[blk, blk, blk, bias_spec],
        out_specs=blk,
        out_shape=jax.ShapeDtypeStruct((b, s, WIDTH_A), BF16),
        scratch_shapes=[rows, padded, padded, rows, rows, rows, rows, rows],
        compiler_params=pltpu.CompilerParams(
            dimension_semantics=("parallel", "parallel"), vmem_limit_bytes=VMEM_LIMIT),
        name="attn_a",
    )(qa, ka, va, bias)


def _t5_causal_bucket(dist):
    dist = np.asarray(dist, dtype=np.int64)
    max_exact = REL_BUCKETS // 2
    safe = np.maximum(dist, 1).astype(np.float32)
    large = max_exact + (np.log(safe / max_exact) / math.log(REL_MAX_DIST / max_exact)
                         * (REL_BUCKETS - max_exact)).astype(np.int64)
    large = np.minimum(large, REL_BUCKETS - 1)
    return np.where(dist < max_exact, dist, large).astype(np.int32)


def _attn_a_bias(rel_bias):
    span = 3 * WIN
    u = np.arange(span) - (WIN - 1)
    valid = (u >= 0) & (u <= WIN)
    tables = []
    for _, dil in DILATED_BRANCHES:
        bucket = _t5_causal_bucket(np.clip(WIN - u, 0, WIN) * dil)
        onehot = jnp.asarray(np.eye(REL_BUCKETS, dtype=np.float32)[bucket])
        vec = jnp.dot(onehot, rel_bias.astype(F32), precision=lax.Precision.HIGHEST)
        vec = jnp.where(jnp.asarray(valid)[:, None], vec * LOG2E, NEG).T
        flat = jnp.tile(vec, (1, WIN))[:, WIN - 1:WIN - 1 + WIN * (span - 1)]
        full = flat.reshape(N_HEADS_A, WIN, span - 1)[:, :, :2 * WIN]
        nchunk = max(CLASSES // dil, 1)
        clen = WIN // nchunk
        full = full.reshape(N_HEADS_A, clen, nchunk, 2, clen, nchunk)
        full = jnp.transpose(full, (0, 2, 1, 3, 5, 4)).reshape(N_HEADS_A, WIN, 2 * WIN)
        first = jnp.where(jnp.asarray(np.arange(2 * WIN) >= WIN)[None, None, :], full, NEG)
        tables.append(jnp.stack([full, first], axis=0))
    t = jnp.stack(tables, axis=0)
    t = t.reshape(len(DILATED_BRANCHES), 2, HEAD_PAIRS, 2 * WIN, 2 * WIN)
    return jnp.transpose(t, (2, 0, 1, 3, 4))


def _attn_b_kernel(qt_ref, k_ref, vt_ref, o_ref, acc0, acc1, m0, m1, s_even, s_odd, mb_even, mb_odd):
    qi = pl.program_id(2)
    tq = TQ_B
    heads = ((acc0, m0), (acc1, m1))
    for acc, m_s in heads:
        acc[...] = jnp.zeros_like(acc)
        m_s[...] = jnp.full(m_s.shape, NEG, F32)

    def scores(kb, s_buf, mb_buf):
        krows = pl.ds(pl.multiple_of(kb * tq, tq), tq)
        for hh in range(2):
            sl = slice(hh * LANES, (hh + 1) * LANES)
            st = _dot(k_ref[krows, sl], qt_ref[hh])
            s_buf[hh] = st
            mb_buf[hh] = jnp.max(st, axis=0, keepdims=True)

    def update(kb, s_buf, mb_buf, masked):
        for hh, (acc, m_s) in enumerate(heads):
            st = s_buf[hh]
            if masked:
                key = lax.broadcasted_iota(jnp.int32, (tq, tq), 0)
                qry = lax.broadcasted_iota(jnp.int32, (tq, tq), 1)
                st = jnp.where(key <= qry, st, NEG)
                m_blk = jnp.max(st, axis=0, keepdims=True)
            else:
                m_blk = mb_buf[hh]
            m_old = m_s[...]
            m_new = jnp.maximum(m_old, m_blk)
            alpha = jnp.exp2(m_old - m_new)
            pt = jnp.exp2(st - m_new).astype(BF16)
            acc[...] = alpha * acc[...] + _dot(vt_ref[kb, hh], pt)
            m_s[...] = m_new

    scores(0, s_even, mb_even)

    def pair_body(i, c):
        kb = 2 * i
        scores(kb + 1, s_odd, mb_odd)
        update(kb, s_even, mb_even, False)
        scores(kb + 2, s_even, mb_even)
        update(kb + 1, s_odd, mb_odd, False)
        return c

    npairs = qi // 2
    lax.fori_loop(0, npairs, pair_body, 0)

    @pl.when(qi % 2 == 1)
    def _():
        scores(qi, s_odd, mb_odd)
        update(qi - 1, s_even, mb_even, False)
        update(qi, s_odd, mb_odd, True)

    @pl.when(qi % 2 == 0)
    def _():
        update(qi, s_even, mb_even, True)

    outs = []
    for acc, _ in heads:
        a = acc[...]
        outs.append(a[:V_HEAD_DIM] / a[V_HEAD_DIM:V_HEAD_DIM + 1])
    o_ref[...] = jnp.concatenate(outs, axis=0).T.astype(o_ref.dtype)


def _attn_b_call(qbt, kb, vbt):
    b, s, _ = kb.shape
    tq = TQ_B
    nkb = s // tq
    qbt = qbt.reshape(b, nkb, HEAD_PAIRS, 2, LANES, tq)
    vbt = vbt.reshape(b, nkb, HEAD_PAIRS, 2, VT_ROWS, tq)
    return pl.pallas_call(
        _attn_b_kernel,
        grid=(b, HEAD_PAIRS, nkb),
        in_specs=[pl.BlockSpec((None, None, None, 2, LANES, tq),
                               lambda bb, p, i: (bb, i, p, 0, 0, 0)),
                  pl.BlockSpec((None, s, 2 * LANES), lambda bb, p, i: (bb, 0, p)),
                  pl.BlockSpec((None, nkb, None, 2, VT_ROWS, tq),
                               lambda bb, p, i: (bb, 0, p, 0, 0, 0))],
        out_specs=pl.BlockSpec((None, tq, LANES), lambda bb, p, i: (bb, i, p)),
        out_shape=jax.ShapeDtypeStruct((b, s, WIDTH_B), BF16),
        scratch_shapes=[pltpu.VMEM((VT_ROWS, tq), F32)] * 2 + [pltpu.VMEM((1, tq), F32)] * 2
        + [pltpu.VMEM((2, tq, tq), F32)] * 2 + [pltpu.VMEM((2, 1, tq), F32)] * 2,
        compiler_params=pltpu.CompilerParams(
            dimension_semantics=("parallel", "parallel", "arbitrary"),
            vmem_limit_bytes=VMEM_LIMIT),
        name="attn_b",
    )(qbt, kb, vbt)


def _ffn_kernel(x_ref, a_ref, b_ref, woa_ref, wob_ref, g_ref, w1_ref, w2_ref, o_ref):
    h = x_ref[...] + _dot(a_ref[...], woa_ref[...]) + _dot(b_ref[...], wob_ref[...])
    hn = (h * lax.rsqrt(jnp.mean(h * h, axis=-1, keepdims=True) + EPS) * g_ref[...]).astype(BF16)
    mlp = None
    for c in range(D_FF // FF_CHUNK):
        sl = slice(c * FF_CHUNK, (c + 1) * FF_CHUNK)
        hid = jnp.square(jnp.maximum(_dot(hn, w1_ref[:, sl]), 0.0)).astype(BF16)
        d = _dot(hid, w2_ref[sl, :])
        mlp = d if mlp is None else mlp + d
    o_ref[...] = h + mlp


def _ffn_call(x2, a2, b2, woa, wob, g, w1, w2):
    n = x2.shape[0]
    tm = TM_FFN
    row = lambda i: (i, 0)
    consts = (woa, wob, g, w1, w2)
    return pl.pallas_call(
        _ffn_kernel,
        grid=(n // tm,),
        in_specs=[pl.BlockSpec((tm, D_MODEL), row), pl.BlockSpec((tm, WIDTH_A), row),
                  pl.BlockSpec((tm, WIDTH_B), row)] + [_const_spec(c.shape) for c in consts],
        out_specs=pl.BlockSpec((tm, D_MODEL), row),
        out_shape=jax.ShapeDtypeStruct((n, D_MODEL), F32),
        compiler_params=pltpu.CompilerParams(
            dimension_semantics=("parallel",), vmem_limit_bytes=VMEM_LIMIT),
        name="ffn",
    )(x2, a2, b2, *consts)


def _pad_heads(w, n_heads, width):
    k = w.shape[0]
    w = w.reshape(k, n_heads, width)
    return jnp.pad(w, ((0, 0), (0, 0), (0, LANES - width))).reshape(k, n_heads * LANES)


def _lane_row(g, offset=0):
    return jnp.pad(g.astype(F32), (offset, LANES - offset - g.shape[0]))[None, :]


def kernel(x, positions, norm_mix_g, w_in, qnorm_a_g, knorm_a_g, rel_bias, cq_norm_g, ckv_norm_g,
           w_uq, w_ukv, qnorm_b_g, knorm_b_g, w_o, norm_ffn_g, w_ff1, w_ff2):
    b, s, d = x.shape
    n = b * s
    c0 = 3 * WIDTH_A
    c1 = c0 + Q_LORA_RANK
    c2 = c1 + KV_LORA_RANK
    w_qkv = w_in[:, :c0].astype(BF16)
    w_cq = w_in[:, c0:c1].astype(BF16)
    w_ckv = w_in[:, c1:c2].astype(BF16)
    w_kr = jnp.pad(w_in[:, c2:], ((0, 0), (QK_NOPE_DIM, LANES - QK_HEAD_DIM_B))).astype(BF16)
    w_uqt = _pad_heads(w_uq, N_HEADS_B, QK_HEAD_DIM_B).T.astype(BF16)
    w_ukv3 = w_ukv.reshape(KV_LORA_RANK, N_HEADS_B, QK_NOPE_DIM + V_HEAD_DIM)
    w_uk_p = _pad_heads(w_ukv3[:, :, :QK_NOPE_DIM].reshape(KV_LORA_RANK, -1),
                        N_HEADS_B, QK_NOPE_DIM).astype(BF16)
    ones_rows = VT_ROWS - V_HEAD_DIM
    w_uvt = jnp.transpose(w_ukv3[:, :, QK_NOPE_DIM:], (1, 2, 0))
    w_uvt = jnp.pad(w_uvt, ((0, 0), (0, ones_rows), (0, 0)))
    w_uvt = w_uvt.reshape(N_HEADS_B * VT_ROWS, KV_LORA_RANK).astype(BF16)
    inv_freq = 1.0 / (ROPE_THETA ** (jnp.arange(0, QK_ROPE_DIM, 2, dtype=F32) / QK_ROPE_DIM))
    ones_col = jnp.tile(jnp.concatenate([jnp.zeros((V_HEAD_DIM,), F32), jnp.ones((ones_rows,), F32)]),
                        N_HEADS_B)[:, None]
    g_mix = norm_mix_g.astype(F32)[None, :]
    head_sum = jnp.asarray(np.kron(np.eye(N_HEADS_A), np.ones((HEAD_DIM_A, HEAD_DIM_A))), BF16)
    consts_a = (
        g_mix, w_qkv, head_sum,
        jnp.tile(qnorm_a_g.astype(F32), N_HEADS_A)[None, :],
        jnp.tile(knorm_a_g.astype(F32), N_HEADS_A)[None, :],
    )
    tm = TM_PROJ
    gq_col = jnp.pad(qnorm_b_g.astype(F32), (0, LANES - QK_HEAD_DIM_B))[:, None]
    consts_b = (
        g_mix, w_cq, w_ckv, w_kr,
        cq_norm_g.astype(F32)[None, :], ckv_norm_g.astype(F32)[None, :],
        w_uqt, w_uk_p, w_uvt,
        jnp.broadcast_to(gq_col, (LANES, tm)), _lane_row(knorm_b_g[:QK_NOPE_DIM]),
        _lane_row(knorm_b_g[QK_NOPE_DIM:], QK_NOPE_DIM),
        jnp.broadcast_to(inv_freq[:, None], (QK_ROPE_DIM // 2, tm)),
        ones_col,
    )
    x2 = x.reshape(n, d)
    qa, ka, va = _proj_a_call(x, consts_a)
    qbt, kb, vbt = _proj_b_call(x2, positions.reshape(n // tm, 1, tm), consts_b)

    out_a = _attn_a_call(qa, ka, va, _attn_a_bias(rel_bias))
    out_b = _attn_b_call(qbt, kb.reshape(b, s, -1), vbt)

    out = _ffn_call(x.reshape(n, d), out_a.reshape(n, -1), out_b.reshape(n, -1),
                    w_o[:WIDTH_A].astype(BF16), w_o[WIDTH_A:].astype(BF16),
                    norm_ffn_g.astype(F32)[None, :], w_ff1.astype(BF16), w_ff2.astype(BF16))
    return out.reshape(b, s, d)
```

```python
import functools
import math

import numpy as np
import jax
import jax.numpy as jnp
from jax import lax
from jax.experimental import pallas as pl
from jax.experimental.pallas import tpu as pltpu

F32 = jnp.float32
BF16 = jnp.bfloat16

D_MODEL = 1024
HEAD_DIM_A = 64
N_HEADS_A = 8
WIDTH_A = N_HEADS_A * HEAD_DIM_A
DILATED_BRANCHES = ((128, 1), (512, 4), (2048, 16))
N_HEADS_B = 8
QK_NOPE_DIM = 64
QK_ROPE_DIM = 32
V_HEAD_DIM = 64
QK_HEAD_DIM_B = QK_NOPE_DIM + QK_ROPE_DIM
Q_LORA_RANK = 768
KV_LORA_RANK = 256
WIDTH_B = N_HEADS_B * V_HEAD_DIM
ROPE_THETA = 10000.0
D_FF = 4 * D_MODEL
REL_BUCKETS = 32
REL_MAX_DIST = 2048
EPS = 1e-6

LANES = 128
HALF = LANES // 2
HEAD_PAIRS = N_HEADS_A // 2
WIN = 128
NEG = -1e30
VMEM_LIMIT = 56 * 1024 * 1024

TM_PROJ = 512
TM_FFN = 512
TQ_B = 512
TSUB_B = 256
FF_CHUNK = 1024
GROUP_A = 16
CLASSES = 4

LOG2E = math.log2(math.e)
SCALE_A = LOG2E / math.sqrt(HEAD_DIM_A)
SCALE_B = LOG2E / math.sqrt(QK_HEAD_DIM_B)

VT_ROWS = V_HEAD_DIM + 16

assert HEAD_DIM_A == HALF and V_HEAD_DIM == HALF
assert TM_PROJ == TQ_B


def _nt_dot(a, b):
    return lax.dot_general(a, b, (((1,), (1,)), ((), ())), preferred_element_type=F32)


def _dot(a, b):
    return jnp.dot(a, b, preferred_element_type=F32)


def _const_spec(shape):
    nd = len(shape)
    return pl.BlockSpec(shape, lambda *_: (0,) * nd, pipeline_mode=pl.Buffered(1))


def _rms_bf16(x, g_row):
    return (x * lax.rsqrt(jnp.mean(x * x, axis=-1, keepdims=True) + EPS) * g_row).astype(BF16)


def _proj_a_kernel(x_ref, gmix_ref, wqkv_ref, hsum_ref, gqa_ref, gka_ref, qa_ref, ka_ref, va_ref,
                   xs_ref):
    tm, d = x_ref.shape
    run = tm // CLASSES
    for c in range(d // LANES):
        xs_ref[c] = x_ref[:, c * LANES:(c + 1) * LANES]

    def head_sumsq(y):
        return _dot((y * y).astype(BF16), hsum_ref[...])

    sub = run // 2
    for h in range(2):
        x = jnp.concatenate(
            [jnp.concatenate([xs_ref[c, pl.ds(r + h * sub * CLASSES, sub, stride=CLASSES), :]
                              for r in range(CLASSES)], axis=0)
             for c in range(d // LANES)], axis=1)
        xn = _rms_bf16(x, gmix_ref[...])
        q = _dot(xn, wqkv_ref[:, :WIDTH_A])
        k = _dot(xn, wqkv_ref[:, WIDTH_A:2 * WIDTH_A])
        ssq = head_sumsq(q)
        v = _dot(xn, wqkv_ref[:, 2 * WIDTH_A:]).astype(BF16)
        ssk = head_sumsq(k)
        q = (q * lax.rsqrt(ssq / HEAD_DIM_A + EPS) * gqa_ref[...] * SCALE_A).astype(BF16)
        k = (k * lax.rsqrt(ssk / HEAD_DIM_A + EPS) * gka_ref[...]).astype(BF16)
        for r in range(CLASSES):
            src = slice(r * sub, (r + 1) * sub)
            dst = slice(h * sub, (h + 1) * sub)
            qa_ref[r, dst] = q[src]
            ka_ref[r, dst] = k[src]
            va_ref[r, dst] = v[src]


def _proj_a_call(x, consts):
    b, s, d = x.shape
    tm = TM_PROJ
    run = tm // CLASSES
    out = jax.ShapeDtypeStruct((b, CLASSES, s // CLASSES, WIDTH_A), BF16)
    outs = pl.pallas_call(
        _proj_a_kernel,
        grid=(b, s // tm),
        in_specs=[pl.BlockSpec((None, tm, d), lambda bb, i: (bb, i, 0))]
        + [_const_spec(c.shape) for c in consts],
        out_specs=[pl.BlockSpec((None, CLASSES, run, WIDTH_A), lambda bb, i: (bb, 0, i, 0))] * 3,
        out_shape=[out] * 3,
        scratch_shapes=[pltpu.VMEM((d // LANES, tm, LANES), F32)],
        compiler_params=pltpu.CompilerParams(
            dimension_semantics=("parallel", "parallel"), vmem_limit_bytes=VMEM_LIMIT),
        name="proj_a",
    )(x, *consts)
    return [o.reshape(b, s, WIDTH_A) for o in outs]


def _proj_b_kernel(x_ref, posr_ref, gmix_ref, wcq_ref, wckv_ref, wkr_ref,
                   gcq_ref, gckv_ref, wuqt_ref, wuk_ref, wuvt_ref,
                   gqbt_ref, gkbn_ref, gkbr_ref, invft_ref, ones_ref,
                   qt_ref, kb_ref, vt_ref):
    tm = x_ref.shape[0]
    xn = _rms_bf16(x_ref[...], gmix_ref[...])
    cq = _dot(xn, wcq_ref[...])
    ckv = _dot(xn, wckv_ref[...])
    kr = _dot(xn, wkr_ref[...])
    half = QK_ROPE_DIM // 2
    r0, r1, r2 = QK_NOPE_DIM, QK_NOPE_DIM + half, QK_HEAD_DIM_B
    ang_t = invft_ref[...] * posr_ref[...].astype(F32)
    cs_t = jnp.cos(ang_t)
    sn_t = jnp.sin(ang_t)

    ckvn = _rms_bf16(ckv, gckv_ref[...])
    kn = _dot(ckvn, wuk_ref[...])
    vt_ref[...] = (_nt_dot(wuvt_ref[...], ckvn) + ones_ref[...]).astype(BF16)

    cqn = _rms_bf16(cq, gcq_ref[...])
    qt = _nt_dot(wuqt_ref[...], cqn)

    zt = jnp.zeros((r0, tm), F32)
    zb = jnp.zeros((LANES - r2, tm), F32)
    cs = jnp.concatenate([zt, cs_t, cs_t, zb], axis=0).T
    sn = jnp.concatenate([zt, -sn_t, sn_t, zb], axis=0).T
    ss_kr = jnp.sum(kr * kr, axis=-1, keepdims=True)
    krg = kr * gkbr_ref[...]
    kr_rot = krg * cs + (pltpu.roll(krg, LANES - half, 1) + pltpu.roll(krg, half, 1)) * sn
    for h in range(N_HEADS_B):
        sl = slice(h * LANES, (h + 1) * LANES)
        kh = kn[:, sl]
        rk = lax.rsqrt((jnp.sum(kh * kh, axis=-1, keepdims=True) + ss_kr) / QK_HEAD_DIM_B + EPS)
        kb_ref[:, sl] = ((kh * gkbn_ref[...] + kr_rot) * rk).astype(BF16)

    for h in range(N_HEADS_B):
        qh = qt[h * LANES:(h + 1) * LANES]
        rq = lax.rsqrt(jnp.sum(qh * qh, axis=0, keepdims=True) / QK_HEAD_DIM_B + EPS)
        y = qh * rq * gqbt_ref[...]
        y1, y2 = y[r0:r1], y[r1:r2]
        out = jnp.concatenate([y[:r0], y1 * cs_t - y2 * sn_t, y2 * cs_t + y1 * sn_t, y[r2:]], axis=0)
        qt_ref[h * LANES:(h + 1) * LANES, :] = (out * SCALE_B).astype(BF16)


def _proj_b_call(x2, pos_row, consts):
    n = x2.shape[0]
    tm = TM_PROJ
    row = lambda i: (i, 0)
    blk3 = lambda i: (i, 0, 0)
    wide = N_HEADS_B * LANES
    vt_rows = N_HEADS_B * VT_ROWS
    return pl.pallas_call(
        _proj_b_kernel,
        grid=(n // tm,),
        in_specs=[pl.BlockSpec((tm, D_MODEL), row),
                  pl.BlockSpec((None, 1, tm), blk3)] + [_const_spec(c.shape) for c in consts],
        out_specs=[pl.BlockSpec((None, wide, tm), blk3), pl.BlockSpec((tm, wide), row),
                   pl.BlockSpec((None, vt_rows, tm), blk3)],
        out_shape=[jax.ShapeDtypeStruct((n // tm, wide, tm), BF16),
                   jax.ShapeDtypeStruct((n, wide), BF16),
                   jax.ShapeDtypeStruct((n // tm, vt_rows, tm), BF16)],
        compiler_params=pltpu.CompilerParams(
            dimension_semantics=("parallel",), vmem_limit_bytes=VMEM_LIMIT),
        name="proj_b",
    )(x2, pos_row, *consts)


def _attn_a_kernel(q_ref, k_ref, v_ref, bias_ref, o_ref,
                   qf, kf, vf, acc, den, m0, m1, onat, *, seq, pad):
    lane = lax.broadcasted_iota(jnp.int32, (1, LANES), 1)
    first_head = lane < HALF
    qf[...] = q_ref[...].astype(F32)
    zeros = jnp.zeros((pad, LANES), F32)
    kf[pl.ds(0, pad), :] = zeros
    vf[pl.ds(0, pad), :] = zeros
    kf[pl.ds(pad, seq), :] = k_ref[...].astype(F32)
    vf[pl.ds(pad, seq), :] = v_ref[...].astype(F32)
    acc[...] = jnp.zeros_like(acc)
    den[...] = jnp.zeros_like(den)
    for m_s in (m0, m1):
        m_s[...] = jnp.full(m_s.shape, NEG, F32)

    cls_rows = seq // CLASSES

    for bi, (window, dil) in enumerate(DILATED_BRANCHES):
        assert window // dil == WIN
        nchunk = max(CLASSES // dil, 1)
        clen = WIN // nchunk
        stride = max(dil // CLASSES, 1)
        step = stride * clen
        nblk = seq // (dil * WIN)
        ublk = min(GROUP_A, nblk)
        ncls = GROUP_A // ublk

        def run(start, clen=clen, stride=stride):
            return pl.ds(start, clen, stride=stride) if stride > 1 else pl.ds(start, clen)

        def gather(ref, starts, run=run):
            return jnp.concatenate([ref[run(st), :] for st in starts], axis=0)

        def group(r0, g, bi=bi, dil=dil, nchunk=nchunk, clen=clen, stride=stride, step=step,
                  ublk=ublk, ncls=ncls, run=run, gather=gather):
            plans = []
            for cc in range(ncls):
                rd = r0 + cc
                if stride == 1:
                    bases = [pl.multiple_of((dil * c + rd) * cls_rows + step * ublk * g, 8)
                             for c in range(nchunk)]
                else:
                    bases = [(rd % CLASSES) * cls_rows + rd // CLASSES + step * ublk * g]
                for u in range(ublk):
                    plans.append((bases, u))

            def scores(bases, u):
                qstarts = [b0 + step * u for b0 in bases]
                kstarts = [pad + b0 + step * (u - 1) for b0 in bases]
                q = gather(qf, qstarts)
                qq = jnp.concatenate([jnp.where(first_head, q, 0.0),
                                      jnp.where(first_head, 0.0, q)], axis=0).astype(BF16)
                kk = jnp.concatenate([gather(kf, kstarts),
                                      gather(kf, [k0 + step for k0 in kstarts])],
                                     axis=0).astype(BF16)
                vv = jnp.concatenate([gather(vf, kstarts),
                                      gather(vf, [k0 + step for k0 in kstarts])],
                                     axis=0).astype(BF16)
                first_blk = jnp.where(g == 0, 1, 0) if u == 0 else 0
                s = _nt_dot(qq, kk) + bias_ref[bi, first_blk]
                return qstarts, s, jnp.concatenate([vv, jnp.ones_like(vv)], axis=1)

            def update(qstarts, s, vext):
                ps, alphas = [], []
                for hh, m_s in enumerate((m0, m1)):
                    sh = s[hh * WIN:(hh + 1) * WIN]
                    m_old = gather(m_s, qstarts)
                    m_new = jnp.maximum(m_old, jnp.max(sh, axis=-1, keepdims=True))
                    alphas.append(jnp.exp2(m_old - m_new))
                    ps.append(jnp.exp2(sh - jnp.concatenate([m_new, m_new], axis=1)).astype(BF16))
                    for c, st in enumerate(qstarts):
                        m_s[run(st), :] = m_new[c * clen:(c + 1) * clen]
                res = _dot(jnp.concatenate(ps, axis=0), vext)
                alpha = jnp.where(first_head, alphas[0], alphas[1])
                acc_new = alpha * gather(acc, qstarts) + jnp.where(
                    first_head, res[:WIN, :LANES], res[WIN:, :LANES])
                den_new = alpha * gather(den, qstarts) + jnp.where(
                    first_head, res[:WIN, LANES:], res[WIN:, LANES:])
                for c, st in enumerate(qstarts):
                    acc[run(st), :] = acc_new[c * clen:(c + 1) * clen]
                    den[run(st), :] = den_new[c * clen:(c + 1) * clen]

            pending = scores(*plans[0])
            for nxt in plans[1:]:
                ahead = scores(*nxt)
                update(*pending)
                pending = ahead
            update(*pending)

        def class_body(ci, carry, ngrp=nblk // ublk, ncls=ncls, group=group):
            def grp_body(g, c):
                group(ci * ncls, g)
                return c
            return lax.fori_loop(0, ngrp, grp_body, carry)

        lax.fori_loop(0, dil // ncls, class_body, 0)

    for r in range(CLASSES):
        rows = pl.ds(r * cls_rows, cls_rows)
        onat[pl.ds(r, cls_rows, stride=CLASSES), :] = acc[rows, :] / den[rows, :]
    o_ref[...] = onat[...].astype(o_ref.dtype)


def _attn_a_call(qa, ka, va, bias):
    b, s, _ = qa.shape
    pad = WIN * DILATED_BRANCHES[-1][1] // CLASSES
    blk = pl.BlockSpec((None, s, LANES), lambda bb, p: (bb, 0, p))
    bias_spec = pl.BlockSpec((None,) + bias.shape[1:], lambda bb, p: (p, 0, 0, 0, 0))
    rows = pltpu.VMEM((s, LANES), F32)
    padded = pltpu.VMEM((pad + s, LANES), F32)
    return pl.pallas_call(
        functools.partial(_attn_a_kernel, seq=s, pad=pad),
        grid=(b, HEAD_PAIRS),
        in_specs=[blk, blk, blk, bias_spec],
        out_specs=blk,
        out_shape=jax.ShapeDtypeStruct((b, s, WIDTH_A), BF16),
        scratch_shapes=[rows, padded, padded, rows, rows, rows, rows, rows],
        compiler_params=pltpu.CompilerParams(
            dimension_semantics=("parallel", "parallel"), vmem_limit_bytes=VMEM_LIMIT),
        name="attn_a",
    )(qa, ka, va, bias)


def _t5_causal_bucket(dist):
    dist = np.asarray(dist, dtype=np.int64)
    max_exact = REL_BUCKETS // 2
    safe = np.maximum(dist, 1).astype(np.float32)
    large = max_exact + (np.log(safe / max_exact) / math.log(REL_MAX_DIST / max_exact)
                         * (REL_BUCKETS - max_exact)).astype(np.int64)
    large = np.minimum(large, REL_BUCKETS - 1)
    return np.where(dist < max_exact, dist, large).astype(np.int32)


def _attn_a_bias(rel_bias):
    span = 3 * WIN
    u = np.arange(span) - (WIN - 1)
    valid = (u >= 0) & (u <= WIN)
    tables = []
    for _, dil in DILATED_BRANCHES:
        bucket = _t5_causal_bucket(np.clip(WIN - u, 0, WIN) * dil)
        onehot = jnp.asarray(np.eye(REL_BUCKETS, dtype=np.float32)[bucket])
        vec = jnp.dot(onehot, rel_bias.astype(F32), precision=lax.Precision.HIGHEST)
        vec = jnp.where(jnp.asarray(valid)[:, None], vec * LOG2E, NEG).T
        flat = jnp.tile(vec, (1, WIN))[:, WIN - 1:WIN - 1 + WIN * (span - 1)]
        full = flat.reshape(N_HEADS_A, WIN, span - 1)[:, :, :2 * WIN]
        nchunk = max(CLASSES // dil, 1)
        clen = WIN // nchunk
        full = full.reshape(N_HEADS_A, clen, nchunk, 2, clen, nchunk)
        full = jnp.transpose(full, (0, 2, 1, 3, 5, 4)).reshape(N_HEADS_A, WIN, 2 * WIN)
        first = jnp.where(jnp.asarray(np.arange(2 * WIN) >= WIN)[None, None, :], full, NEG)
        tables.append(jnp.stack([full, first], axis=0))
    t = jnp.stack(tables, axis=0)
    t = t.reshape(len(DILATED_BRANCHES), 2, HEAD_PAIRS, 2 * WIN, 2 * WIN)
    return jnp.transpose(t, (2, 0, 1, 3, 4))


def _attn_b_kernel(qt_ref, k_ref, vt_ref, o_ref, acc0, acc1, m0, m1, s_even, s_odd, mb_even, mb_odd):
    tq = TQ_B
    nq = qt_ref.shape[0]
    heads = ((acc0, m0), (acc1, m1))
    bufs = ((s_even, mb_even), (s_odd, mb_odd))

    def scores(qi, kb, s_buf, mb_buf):
        krows = pl.ds(pl.multiple_of(kb * tq, tq), tq)
        for hh in range(2):
            sl = slice(hh * LANES, (hh + 1) * LANES)
            st = _dot(k_ref[krows, sl], qt_ref[qi, hh])
            s_buf[hh] = st
            mb_buf[hh] = jnp.max(st, axis=0, keepdims=True)

    def update(kb, s_buf, mb_buf, masked):
        for hh, (acc, m_s) in enumerate(heads):
            st = s_buf[hh]
            if masked:
                key = lax.broadcasted_iota(jnp.int32, (tq, tq), 0)
                qry = lax.broadcasted_iota(jnp.int32, (tq, tq), 1)
                st = jnp.where(key <= qry, st, NEG)
                m_blk = jnp.max(st, axis=0, keepdims=True)
            else:
                m_blk = mb_buf[hh]
            m_old = m_s[...]
            m_new = jnp.maximum(m_old, m_blk)
            alpha = jnp.exp2(m_old - m_new)
            pt = jnp.exp2(st - m_new).astype(BF16)
            acc[...] = alpha * acc[...] + _dot(vt_ref[kb, hh], pt)
            m_s[...] = m_new

    scores(0, 0, *bufs[0])
    step0 = 0
    for qi in range(nq):
        cur, nxt = bufs[step0 % 2], bufs[(step0 + 1) % 2]
        for acc, m_s in heads:
            acc[...] = jnp.zeros_like(acc)
            m_s[...] = jnp.full(m_s.shape, NEG, F32)

        def pair_body(i, c, qi=qi, cur=cur, nxt=nxt):
            kb = 2 * i
            scores(qi, kb + 1, *nxt)
            update(kb, *cur, False)
            scores(qi, kb + 2, *cur)
            update(kb + 1, *nxt, False)
            return c

        lax.fori_loop(0, qi // 2, pair_body, 0)
        last = cur
        if qi % 2 == 1:
            scores(qi, qi, *nxt)
            update(qi - 1, *cur, False)
            last = nxt
        if qi + 1 < nq:
            scores(qi + 1, 0, *(cur if last is nxt else nxt))
        update(qi, *last, True)

        outs = []
        for acc, _ in heads:
            a = acc[...]
            outs.append(a[:V_HEAD_DIM] / a[V_HEAD_DIM:V_HEAD_DIM + 1])
        o_ref[pl.ds(qi * tq, tq), :] = jnp.concatenate(outs, axis=0).T.astype(o_ref.dtype)
        step0 += qi + 1


def _attn_b_call(qbt, kb, vbt):
    b, s, _ = kb.shape
    tq = TQ_B
    nkb = s // tq
    qbt = qbt.reshape(b, nkb, HEAD_PAIRS, 2, LANES, tq)
    vbt = vbt.reshape(b, nkb, HEAD_PAIRS, 2, VT_ROWS, tq)
    return pl.pallas_call(
        _attn_b_kernel,
        grid=(b, HEAD_PAIRS),
        in_specs=[pl.BlockSpec((None, nkb, None, 2, LANES, tq), lambda bb, p: (bb, 0, p, 0, 0, 0)),
                  pl.BlockSpec((None, s, 2 * LANES), lambda bb, p: (bb, 0, p)),
                  pl.BlockSpec((None, nkb, None, 2, VT_ROWS, tq),
                               lambda bb, p: (bb, 0, p, 0, 0, 0))],
        out_specs=pl.BlockSpec((None, s, LANES), lambda bb, p: (bb, 0, p)),
        out_shape=jax.ShapeDtypeStruct((b, s, WIDTH_B), BF16),
        scratch_shapes=[pltpu.VMEM((VT_ROWS, tq), F32)] * 2 + [pltpu.VMEM((1, tq), F32)] * 2
        + [pltpu.VMEM((2, tq, tq), F32)] * 2 + [pltpu.VMEM((2, 1, tq), F32)] * 2,
        compiler_params=pltpu.CompilerParams(
            dimension_semantics=("parallel", "parallel"), vmem_limit_bytes=VMEM_LIMIT),
        name="attn_b",
    )(qbt, kb, vbt)


def _ffn_kernel(x_ref, a_ref, b_ref, woa_ref, wob_ref, g_ref, w1_ref, w2_ref, o_ref):
    h = x_ref[...] + _dot(a_ref[...], woa_ref[...]) + _dot(b_ref[...], wob_ref[...])
    hn = (h * lax.rsqrt(jnp.mean(h * h, axis=-1, keepdims=True) + EPS) * g_ref[...]).astype(BF16)
    mlp = None
    for c in range(D_FF // FF_CHUNK):
        sl = slice(c * FF_CHUNK, (c + 1) * FF_CHUNK)
        hid = jnp.square(jnp.maximum(_dot(hn, w1_ref[:, sl]), 0.0)).astype(BF16)
        d = _dot(hid, w2_ref[sl, :])
        mlp = d if mlp is None else mlp + d
    o_ref[...] = h + mlp


def _ffn_call(x2, a2, b2, woa, wob, g, w1, w2):
    n = x2.shape[0]
    tm = TM_FFN
    row = lambda i: (i, 0)
    consts = (woa, wob, g, w1, w2)
    return pl.pallas_call(
        _ffn_kernel,
        grid=(n // tm,),
        in_specs=[pl.BlockSpec((tm, D_MODEL), row), pl.BlockSpec((tm, WIDTH_A), row),
                  pl.BlockSpec((tm, WIDTH_B), row)] + [_const_spec(c.shape) for c in consts],
        out_specs=pl.BlockSpec((tm, D_MODEL), row),
        out_shape=jax.ShapeDtypeStruct((n, D_MODEL), F32),
        compiler_params=pltpu.CompilerParams(
            dimension_semantics=("parallel",), vmem_limit_bytes=VMEM_LIMIT),
        name="ffn",
    )(x2, a2, b2, *consts)


def _pad_heads(w, n_heads, width):
    k = w.shape[0]
    w = w.reshape(k, n_heads, width)
    return jnp.pad(w, ((0, 0), (0, 0), (0, LANES - width))).reshape(k, n_heads * LANES)


def _lane_row(g, offset=0):
    return jnp.pad(g.astype(F32), (offset, LANES - offset - g.shape[0]))[None, :]


def kernel(x, positions, norm_mix_g, w_in, qnorm_a_g, knorm_a_g, rel_bias, cq_norm_g, ckv_norm_g,
           w_uq, w_ukv, qnorm_b_g, knorm_b_g, w_o, norm_ffn_g, w_ff1, w_ff2):
    b, s, d = x.shape
    n = b * s
    c0 = 3 * WIDTH_A
    c1 = c0 + Q_LORA_RANK
    c2 = c1 + KV_LORA_RANK
    w_qkv = w_in[:, :c0].astype(BF16)
    w_cq = w_in[:, c0:c1].astype(BF16)
    w_ckv = w_in[:, c1:c2].astype(BF16)
    w_kr = jnp.pad(w_in[:, c2:], ((0, 0), (QK_NOPE_DIM, LANES - QK_HEAD_DIM_B))).astype(BF16)
    w_uqt = _pad_heads(w_uq, N_HEADS_B, QK_HEAD_DIM_B).T.astype(BF16)
    w_ukv3 = w_ukv.reshape(KV_LORA_RANK, N_HEADS_B, QK_NOPE_DIM + V_HEAD_DIM)
    w_uk_p = _pad_heads(w_ukv3[:, :, :QK_NOPE_DIM].reshape(KV_LORA_RANK, -1),
                        N_HEADS_B, QK_NOPE_DIM).astype(BF16)
    ones_rows = VT_ROWS - V_HEAD_DIM
    w_uvt = jnp.transpose(w_ukv3[:, :, QK_NOPE_DIM:], (1, 2, 0))
    w_uvt = jnp.pad(w_uvt, ((0, 0), (0, ones_rows), (0, 0)))
    w_uvt = w_uvt.reshape(N_HEADS_B * VT_ROWS, KV_LORA_RANK).astype(BF16)
    inv_freq = 1.0 / (ROPE_THETA ** (jnp.arange(0, QK_ROPE_DIM, 2, dtype=F32) / QK_ROPE_DIM))
    ones_col = jnp.tile(jnp.concatenate([jnp.zeros((V_HEAD_DIM,), F32), jnp.ones((ones_rows,), F32)]),
                        N_HEADS_B)[:, None]
    g_mix = norm_mix_g.astype(F32)[None, :]
    head_sum = jnp.asarray(np.kron(np.eye(N_HEADS_A), np.ones((HEAD_DIM_A, HEAD_DIM_A))), BF16)
    consts_a = (
        g_mix, w_qkv, head_sum,
        jnp.tile(qnorm_a_g.astype(F32), N_HEADS_A)[None, :],
        jnp.tile(knorm_a_g.astype(F32), N_HEADS_A)[None, :],
    )
    tm = TM_PROJ
    gq_col = jnp.pad(qnorm_b_g.astype(F32), (0, LANES - QK_HEAD_DIM_B))[:, None]
    consts_b = (
        g_mix, w_cq, w_ckv, w_kr,
        cq_norm_g.astype(F32)[None, :], ckv_norm_g.astype(F32)[None, :],
        w_uqt, w_uk_p, w_uvt,
        jnp.broadcast_to(gq_col, (LANES, tm)), _lane_row(knorm_b_g[:QK_NOPE_DIM]),
        _lane_row(knorm_b_g[QK_NOPE_DIM:], QK_NOPE_DIM),
        jnp.broadcast_to(inv_freq[:, None], (QK_ROPE_DIM // 2, tm)),
        ones_col,
    )
    x2 = x.reshape(n, d)
    qa, ka, va = _proj_a_call(x, consts_a)
    qbt, kb, vbt = _proj_b_call(x2, positions.reshape(n // tm, 1, tm), consts_b)

    out_a = _attn_a_call(qa, ka, va, _attn_a_bias(rel_bias))
    out_b = _attn_b_call(qbt, kb.reshape(b, s, -1), vbt)

    out = _ffn_call(x.reshape(n, d), out_a.reshape(n, -1), out_b.reshape(n, -1),
                    w_o[:WIDTH_A].astype(BF16), w_o[WIDTH_A:].astype(BF16),
                    norm_ffn_g.astype(F32)[None, :], w_ff1.astype(BF16), w_ff2.astype(BF16))
    return out.reshape(b, s, d)
```

```python
import functools
import math

import numpy as np
import jax
import jax.numpy as jnp
from jax import lax
from jax.experimental import pallas as pl
from jax.experimental.pallas import tpu as pltpu

F32 = jnp.float32
BF16 = jnp.bfloat16

D_MODEL = 1024
HEAD_DIM_A = 64
N_HEADS_A = 8
WIDTH_A = N_HEADS_A * HEAD_DIM_A
DILATED_BRANCHES = ((128, 1), (512, 4), (2048, 16))
N_HEADS_B = 8
QK_NOPE_DIM = 64
QK_ROPE_DIM = 32
V_HEAD_DIM = 64
QK_HEAD_DIM_B = QK_NOPE_DIM + QK_ROPE_DIM
Q_LORA_RANK = 768
KV_LORA_RANK = 256
WIDTH_B = N_HEADS_B * V_HEAD_DIM
ROPE_THETA = 10000.0
D_FF = 4 * D_MODEL
REL_BUCKETS = 32
REL_MAX_DIST = 2048
EPS = 1e-6

LANES = 128
HALF = LANES // 2
HEAD_PAIRS = N_HEADS_A // 2
WIN = 128
NEG = -1e30
VMEM_LIMIT = 56 * 1024 * 1024

TM_PROJ = 512
TM_FFN = 512
TQ_B = 512
TSUB_B = 256
FF_CHUNK = 1024
GROUP_A = 16
CLASSES = 4

LOG2E = math.log2(math.e)
SCALE_A = LOG2E / math.sqrt(HEAD_DIM_A)
SCALE_B = LOG2E / math.sqrt(QK_HEAD_DIM_B)

VT_ROWS = V_HEAD_DIM + 16

assert HEAD_DIM_A == HALF and V_HEAD_DIM == HALF
assert TM_PROJ == TQ_B


def _nt_dot(a, b):
    return lax.dot_general(a, b, (((1,), (1,)), ((), ())), preferred_element_type=F32)


def _dot(a, b):
    return jnp.dot(a, b, preferred_element_type=F32)


def _const_spec(shape):
    nd = len(shape)
    return pl.BlockSpec(shape, lambda *_: (0,) * nd, pipeline_mode=pl.Buffered(1))


def _rms_bf16(x, g_row):
    return (x * lax.rsqrt(jnp.mean(x * x, axis=-1, keepdims=True) + EPS) * g_row).astype(BF16)


def _proj_a_kernel(x_ref, gmix_ref, wqkv_ref, hsum_ref, gqa_ref, gka_ref, qa_ref, ka_ref, va_ref,
                   xs_ref):
    tm, d = x_ref.shape
    run = tm // CLASSES
    for c in range(d // LANES):
        xs_ref[c] = x_ref[:, c * LANES:(c + 1) * LANES]

    def head_sumsq(y):
        return _dot((y * y).astype(BF16), hsum_ref[...])

    sub = run // 2
    for h in range(2):
        x = jnp.concatenate(
            [jnp.concatenate([xs_ref[c, pl.ds(r + h * sub * CLASSES, sub, stride=CLASSES), :]
                              for r in range(CLASSES)], axis=0)
             for c in range(d // LANES)], axis=1)
        xn = _rms_bf16(x, gmix_ref[...])
        q = _dot(xn, wqkv_ref[:, :WIDTH_A])
        k = _dot(xn, wqkv_ref[:, WIDTH_A:2 * WIDTH_A])
        ssq = head_sumsq(q)
        v = _dot(xn, wqkv_ref[:, 2 * WIDTH_A:]).astype(BF16)
        ssk = head_sumsq(k)
        q = (q * lax.rsqrt(ssq / HEAD_DIM_A + EPS) * gqa_ref[...] * SCALE_A).astype(BF16)
        k = (k * lax.rsqrt(ssk / HEAD_DIM_A + EPS) * gka_ref[...]).astype(BF16)
        for r in range(CLASSES):
            src = slice(r * sub, (r + 1) * sub)
            dst = slice(h * sub, (h + 1) * sub)
            qa_ref[r, dst] = q[src]
            ka_ref[r, dst] = k[src]
            va_ref[r, dst] = v[src]


def _proj_a_call(x, consts):
    b, s, d = x.shape
    tm = TM_PROJ
    run = tm // CLASSES
    out = jax.ShapeDtypeStruct((b, CLASSES, s // CLASSES, WIDTH_A), BF16)
    outs = pl.pallas_call(
        _proj_a_kernel,
        grid=(b, s // tm),
        in_specs=[pl.BlockSpec((None, tm, d), lambda bb, i: (bb, i, 0))]
        + [_const_spec(c.shape) for c in consts],
        out_specs=[pl.BlockSpec((None, CLASSES, run, WIDTH_A), lambda bb, i: (bb, 0, i, 0))] * 3,
        out_shape=[out] * 3,
        scratch_shapes=[pltpu.VMEM((d // LANES, tm, LANES), F32)],
        compiler_params=pltpu.CompilerParams(
            dimension_semantics=("parallel", "parallel"), vmem_limit_bytes=VMEM_LIMIT),
        name="proj_a",
    )(x, *consts)
    return [o.reshape(b, s, WIDTH_A) for o in outs]


def _proj_b_kernel(x_ref, posr_ref, gmix_ref, wcq_ref, wckv_ref, wkr_ref,
                   gcq_ref, gckv_ref, wuqt_ref, wuk_ref, wuvt_ref,
                   gqbt_ref, gkbn_ref, gkbr_ref, invft_ref, ones_ref,
                   qt_ref, kb_ref, vt_ref):
    nparts = 1
    tm = x_ref.shape[0] // nparts
    half = QK_ROPE_DIM // 2
    r0, r1, r2 = QK_NOPE_DIM, QK_NOPE_DIM + half, QK_HEAD_DIM_B
    for part in range(nparts):
        toks = slice(part * tm, (part + 1) * tm)
        xn = _rms_bf16(x_ref[toks, :], gmix_ref[...])
        cq = _dot(xn, wcq_ref[...])
        ckv = _dot(xn, wckv_ref[...])
        kr = _dot(xn, wkr_ref[...])
        ang_t = invft_ref[...] * posr_ref[:, toks].astype(F32)
        cs_t = jnp.cos(ang_t)
        sn_t = jnp.sin(ang_t)

        ckvn = _rms_bf16(ckv, gckv_ref[...])
        kn = _dot(ckvn, wuk_ref[...])
        vt_ref[:, toks] = (_nt_dot(wuvt_ref[...], ckvn) + ones_ref[...]).astype(BF16)

        cqn = _rms_bf16(cq, gcq_ref[...])
        qt = _nt_dot(wuqt_ref[...], cqn)

        zt = jnp.zeros((r0, tm), F32)
        zb = jnp.zeros((LANES - r2, tm), F32)
        cs = jnp.concatenate([zt, cs_t, cs_t, zb], axis=0).T
        sn = jnp.concatenate([zt, -sn_t, sn_t, zb], axis=0).T
        ss_kr = jnp.sum(kr * kr, axis=-1, keepdims=True)
        krg = kr * gkbr_ref[...]
        kr_rot = krg * cs + (pltpu.roll(krg, LANES - half, 1) + pltpu.roll(krg, half, 1)) * sn
        for h in range(N_HEADS_B):
            sl = slice(h * LANES, (h + 1) * LANES)
            kh = kn[:, sl]
            rk = lax.rsqrt((jnp.sum(kh * kh, axis=-1, keepdims=True) + ss_kr) / QK_HEAD_DIM_B + EPS)
            kb_ref[toks, sl] = ((kh * gkbn_ref[...] + kr_rot) * rk).astype(BF16)

        for h in range(N_HEADS_B):
            qh = qt[h * LANES:(h + 1) * LANES]
            rq = lax.rsqrt(jnp.sum(qh * qh, axis=0, keepdims=True) / QK_HEAD_DIM_B + EPS)
            y = qh * rq * gqbt_ref[...]
            y1, y2 = y[r0:r1], y[r1:r2]
            out = jnp.concatenate([y[:r0], y1 * cs_t - y2 * sn_t, y2 * cs_t + y1 * sn_t, y[r2:]],
                                  axis=0)
            qt_ref[h * LANES:(h + 1) * LANES, toks] = (out * SCALE_B).astype(BF16)


def _proj_b_call(x2, pos_row, consts):
    n = x2.shape[0]
    tm = TM_PROJ
    row = lambda i: (i, 0)
    blk3 = lambda i: (i, 0, 0)
    wide = N_HEADS_B * LANES
    vt_rows = N_HEADS_B * VT_ROWS
    return pl.pallas_call(
        _proj_b_kernel,
        grid=(n // tm,),
        in_specs=[pl.BlockSpec((tm, D_MODEL), row),
                  pl.BlockSpec((None, 1, tm), blk3)] + [_const_spec(c.shape) for c in consts],
        out_specs=[pl.BlockSpec((None, wide, tm), blk3), pl.BlockSpec((tm, wide), row),
                   pl.BlockSpec((None, vt_rows, tm), blk3)],
        out_shape=[jax.ShapeDtypeStruct((n // tm, wide, tm), BF16),
                   jax.ShapeDtypeStruct((n, wide), BF16),
                   jax.ShapeDtypeStruct((n // tm, vt_rows, tm), BF16)],
        compiler_params=pltpu.CompilerParams(
            dimension_semantics=("parallel",), vmem_limit_bytes=VMEM_LIMIT),
        name="proj_b",
    )(x2, pos_row, *consts)


def _attn_a_kernel(q_ref, k_ref, v_ref, bias_ref, o_ref,
                   qf, kf, vf, acc, den, m0, m1, onat, *, seq, pad):
    lane = lax.broadcasted_iota(jnp.int32, (1, LANES), 1)
    first_head = lane < HALF
    qf[...] = q_ref[...].astype(F32)
    zeros = jnp.zeros((pad, LANES), F32)
    kf[pl.ds(0, pad), :] = zeros
    vf[pl.ds(0, pad), :] = zeros
    kf[pl.ds(pad, seq), :] = k_ref[...].astype(F32)
    vf[pl.ds(pad, seq), :] = v_ref[...].astype(F32)

    cls_rows = seq // CLASSES

    for bi, (window, dil) in enumerate(DILATED_BRANCHES):
        assert window // dil == WIN
        nchunk = max(CLASSES // dil, 1)
        clen = WIN // nchunk
        stride = max(dil // CLASSES, 1)
        step = stride * clen
        nblk = seq // (dil * WIN)
        ublk = min(GROUP_A, nblk)
        ncls = GROUP_A // ublk

        def run(start, clen=clen, stride=stride):
            return pl.ds(start, clen, stride=stride) if stride > 1 else pl.ds(start, clen)

        def gather(ref, starts, run=run):
            return jnp.concatenate([ref[run(st), :] for st in starts], axis=0)

        def group(r0, g, bi=bi, dil=dil, nchunk=nchunk, clen=clen, stride=stride, step=step,
                  ublk=ublk, ncls=ncls, run=run, gather=gather):
            plans = []
            for cc in range(ncls):
                rd = r0 + cc
                if stride == 1:
                    bases = [pl.multiple_of((dil * c + rd) * cls_rows + step * ublk * g, 8)
                             for c in range(nchunk)]
                else:
                    bases = [(rd % CLASSES) * cls_rows + rd // CLASSES + step * ublk * g]
                for u in range(ublk):
                    plans.append((bases, u))

            def scores(bases, u):
                qstarts = [b0 + step * u for b0 in bases]
                kstarts = [pad + b0 + step * (u - 1) for b0 in bases]
                q = gather(qf, qstarts)
                qq = jnp.concatenate([jnp.where(first_head, q, 0.0),
                                      jnp.where(first_head, 0.0, q)], axis=0).astype(BF16)
                kk = jnp.concatenate([gather(kf, kstarts),
                                      gather(kf, [k0 + step for k0 in kstarts])],
                                     axis=0).astype(BF16)
                vv = jnp.concatenate([gather(vf, kstarts),
                                      gather(vf, [k0 + step for k0 in kstarts])],
                                     axis=0).astype(BF16)
                first_blk = jnp.where(g == 0, 1, 0) if u == 0 else 0
                s = _nt_dot(qq, kk) + bias_ref[bi, first_blk]
                return qstarts, s, jnp.concatenate([vv, jnp.ones_like(vv)], axis=1)

            def update(qstarts, s, vext):
                ps, alphas = [], []
                for hh, m_s in enumerate((m0, m1)):
                    sh = s[hh * WIN:(hh + 1) * WIN]
                    m_blk = jnp.max(sh, axis=-1, keepdims=True)
                    if bi == 0:
                        m_new = jnp.broadcast_to(m_blk, (WIN, LANES))
                    else:
                        m_old = gather(m_s, qstarts)
                        m_new = jnp.maximum(m_old, m_blk)
                        alphas.append(jnp.exp2(m_old - m_new))
                    ps.append(jnp.exp2(sh - jnp.concatenate([m_new, m_new], axis=1)).astype(BF16))
                    for c, st in enumerate(qstarts):
                        m_s[run(st), :] = m_new[c * clen:(c + 1) * clen]
                res = _dot(jnp.concatenate(ps, axis=0), vext)
                acc_new = jnp.where(first_head, res[:WIN, :LANES], res[WIN:, :LANES])
                den_new = jnp.where(first_head, res[:WIN, LANES:], res[WIN:, LANES:])
                if bi > 0:
                    alpha = jnp.where(first_head, alphas[0], alphas[1])
                    acc_new = alpha * gather(acc, qstarts) + acc_new
                    den_new = alpha * gather(den, qstarts) + den_new
                for c, st in enumerate(qstarts):
                    acc[run(st), :] = acc_new[c * clen:(c + 1) * clen]
                    den[run(st), :] = den_new[c * clen:(c + 1) * clen]

            pending = scores(*plans[0])
            for nxt in plans[1:]:
                ahead = scores(*nxt)
                update(*pending)
                pending = ahead
            update(*pending)

        def class_body(ci, carry, ngrp=nblk // ublk, ncls=ncls, group=group):
            def grp_body(g, c):
                group(ci * ncls, g)
                return c
            return lax.fori_loop(0, ngrp, grp_body, carry)

        lax.fori_loop(0, dil // ncls, class_body, 0)

    for r in range(CLASSES):
        rows = pl.ds(r * cls_rows, cls_rows)
        onat[pl.ds(r, cls_rows, stride=CLASSES), :] = acc[rows, :] / den[rows, :]
    o_ref[...] = onat[...].astype(o_ref.dtype)


def _attn_a_call(qa, ka, va, bias):
    b, s, _ = qa.shape
    pad = WIN * DILATED_BRANCHES[-1][1] // CLASSES
    blk = pl.BlockSpec((None, s, LANES), lambda bb, p: (bb, 0, p))
    bias_spec = pl.BlockSpec((None,) + bias.shape[1:], lambda bb, p: (p, 0, 0, 0, 0))
    rows = pltpu.VMEM((s, LANES), F32)
    padded = pltpu.VMEM((pad + s, LANES), F32)
    return pl.pallas_call(
        functools.partial(_attn_a_kernel, seq=s, pad=pad),
        grid=(b, HEAD_PAIRS),
        in_specs=[blk, blk, blk, bias_spec],
        out_specs=blk,
        out_shape=jax.ShapeDtypeStruct((b, s, WIDTH_A), BF16),
        scratch_shapes=[rows, padded, padded, rows, rows, rows, rows, rows],
        compiler_params=pltpu.CompilerParams(
            dimension_semantics=("parallel", "parallel"), vmem_limit_bytes=VMEM_LIMIT),
        name="attn_a",
    )(qa, ka, va, bias)


def _t5_causal_bucket(dist):
    dist = np.asarray(dist, dtype=np.int64)
    max_exact = REL_BUCKETS // 2
    safe = np.maximum(dist, 1).astype(np.float32)
    large = max_exact + (np.log(safe / max_exact) / math.log(REL_MAX_DIST / max_exact)
                         * (REL_BUCKETS - max_exact)).astype(np.int64)
    large = np.minimum(large, REL_BUCKETS - 1)
    return np.where(dist < max_exact, dist, large).astype(np.int32)


def _bias_bucket_tables():
    i = np.arange(WIN)[:, None]
    c = np.arange(2 * WIN)[None, :]
    sub = WIN + i - c
    tables = []
    for _, dil in DILATED_BRANCHES:
        bucket = _t5_causal_bucket(np.clip(sub, 0, WIN) * dil)
        bucket = np.where((sub >= 0) & (sub <= WIN), bucket, REL_BUCKETS)
        nchunk = max(CLASSES // dil, 1)
        clen = WIN // nchunk
        bucket = bucket.reshape(clen, nchunk, 2, clen, nchunk)
        tables.append(np.transpose(bucket, (1, 0, 2, 4, 3)).reshape(WIN, 2 * WIN))
    return np.stack(tables).astype(np.int32)


def _bias_kernel(bias_ref, bucket_ref, o_ref):
    bucket = bucket_ref[...]
    first_blk_cols = lax.broadcasted_iota(jnp.int32, bucket.shape, 1) >= WIN
    for h in range(N_HEADS_A):
        t = jnp.full(bucket.shape, NEG, F32)
        for bkt in range(REL_BUCKETS):
            t = jnp.where(bucket == bkt, bias_ref[bkt, h], t)
        rows = slice((h % 2) * WIN, (h % 2 + 1) * WIN)
        o_ref[h // 2, 0, rows, :] = t
        o_ref[h // 2, 1, rows, :] = jnp.where(first_blk_cols, t, NEG)


def _attn_a_bias(rel_bias):
    buckets = jnp.asarray(_bias_bucket_tables())
    nbr = len(DILATED_BRANCHES)
    return pl.pallas_call(
        _bias_kernel,
        grid=(nbr,),
        in_specs=[pl.BlockSpec(memory_space=pltpu.SMEM),
                  pl.BlockSpec((None, WIN, 2 * WIN), lambda i: (i, 0, 0))],
        out_specs=pl.BlockSpec((HEAD_PAIRS, None, 2, 2 * WIN, 2 * WIN), lambda i: (0, i, 0, 0, 0)),
        out_shape=jax.ShapeDtypeStruct((HEAD_PAIRS, nbr, 2, 2 * WIN, 2 * WIN), F32),
        compiler_params=pltpu.CompilerParams(dimension_semantics=("parallel",)),
        name="bias_a",
    )(rel_bias.astype(F32) * LOG2E, buckets)


def _attn_b_kernel(qt_ref, k_ref, vt_ref, o_ref, acc0, acc1, m0, m1, s_even, s_odd, mb_even, mb_odd):
    tq = TQ_B
    nq = qt_ref.shape[0]
    heads = ((acc0, m0), (acc1, m1))
    bufs = ((s_even, mb_even), (s_odd, mb_odd))

    def scores(qi, kb, s_buf, mb_buf):
        krows = pl.ds(pl.multiple_of(kb * tq, tq), tq)
        for hh in range(2):
            sl = slice(hh * LANES, (hh + 1) * LANES)
            st = _dot(k_ref[krows, sl], qt_ref[qi, hh])
            s_buf[hh] = st
            mb_buf[hh] = jnp.max(st, axis=0, keepdims=True)

    def update(kb, s_buf, mb_buf, masked):
        for hh, (acc, m_s) in enumerate(heads):
            st = s_buf[hh]
            if masked:
                key = lax.broadcasted_iota(jnp.int32, (tq, tq), 0)
                qry = lax.broadcasted_iota(jnp.int32, (tq, tq), 1)
                st = jnp.where(key <= qry, st, NEG)
                m_blk = jnp.max(st, axis=0, keepdims=True)
            else:
                m_blk = mb_buf[hh]
            m_old = m_s[...]
            m_new = jnp.maximum(m_old, m_blk)
            alpha = jnp.exp2(m_old - m_new)
            pt = jnp.exp2(st - m_new).astype(BF16)
            acc[...] = alpha * acc[...] + _dot(vt_ref[kb, hh], pt)
            m_s[...] = m_new

    scores(0, 0, *bufs[0])
    step0 = 0
    for qi in range(nq):
        cur, nxt = bufs[step0 % 2], bufs[(step0 + 1) % 2]
        for acc, m_s in heads:
            acc[...] = jnp.zeros_like(acc)
            m_s[...] = jnp.full(m_s.shape, NEG, F32)

        def pair_body(i, c, qi=qi, cur=cur, nxt=nxt):
            kb = 2 * i
            scores(qi, kb + 1, *nxt)
            update(kb, *cur, False)
            scores(qi, kb + 2, *cur)
            update(kb + 1, *nxt, False)
            return c

        lax.fori_loop(0, qi // 2, pair_body, 0)
        last = cur
        if qi % 2 == 1:
            scores(qi, qi, *nxt)
            update(qi - 1, *cur, False)
            last = nxt
        if qi + 1 < nq:
            scores(qi + 1, 0, *(cur if last is nxt else nxt))
        update(qi, *last, True)

        outs = []
        for acc, _ in heads:
            a = acc[...]
            outs.append(a[:V_HEAD_DIM] / a[V_HEAD_DIM:V_HEAD_DIM + 1])
        o_ref[pl.ds(qi * tq, tq), :] = jnp.concatenate(outs, axis=0).T.astype(o_ref.dtype)
        step0 += qi + 1


def _attn_b_call(qbt, kb, vbt):
    b, s, _ = kb.shape
    tq = TQ_B
    nkb = s // tq
    qbt = qbt.reshape(b, nkb, HEAD_PAIRS, 2, LANES, tq)
    vbt = vbt.reshape(b, nkb, HEAD_PAIRS, 2, VT_ROWS, tq)
    return pl.pallas_call(
        _attn_b_kernel,
        grid=(b, HEAD_PAIRS),
        in_specs=[pl.BlockSpec((None, nkb, None, 2, LANES, tq), lambda bb, p: (bb, 0, p, 0, 0, 0)),
                  pl.BlockSpec((None, s, 2 * LANES), lambda bb, p: (bb, 0, p)),
                  pl.BlockSpec((None, nkb, None, 2, VT_ROWS, tq),
                               lambda bb, p: (bb, 0, p, 0, 0, 0))],
        out_specs=pl.BlockSpec((None, s, LANES), lambda bb, p: (bb, 0, p)),
        out_shape=jax.ShapeDtypeStruct((b, s, WIDTH_B), BF16),
        scratch_shapes=[pltpu.VMEM((VT_ROWS, tq), F32)] * 2 + [pltpu.VMEM((1, tq), F32)] * 2
        + [pltpu.VMEM((2, tq, tq), F32)] * 2 + [pltpu.VMEM((2, 1, tq), F32)] * 2,
        compiler_params=pltpu.CompilerParams(
            dimension_semantics=("parallel", "parallel"), vmem_limit_bytes=VMEM_LIMIT),
        name="attn_b",
    )(qbt, kb, vbt)


def _ffn_kernel(x_ref, a_ref, b_ref, woa_ref, wob_ref, g_ref, w1_ref, w2_ref, o_ref):
    h = x_ref[...] + _dot(a_ref[...], woa_ref[...]) + _dot(b_ref[...], wob_ref[...])
    hn = (h * lax.rsqrt(jnp.mean(h * h, axis=-1, keepdims=True) + EPS) * g_ref[...]).astype(BF16)
    mlp = None
    for c in range(D_FF // FF_CHUNK):
        sl = slice(c * FF_CHUNK, (c + 1) * FF_CHUNK)
        hid = jnp.square(jnp.maximum(_dot(hn, w1_ref[:, sl]), 0.0)).astype(BF16)
        d = _dot(hid, w2_ref[sl, :])
        mlp = d if mlp is None else mlp + d
    o_ref[...] = h + mlp


def _ffn_call(x2, a2, b2, woa, wob, g, w1, w2):
    n = x2.shape[0]
    tm = TM_FFN
    row = lambda i: (i, 0)
    consts = (woa, wob, g, w1, w2)
    return pl.pallas_call(
        _ffn_kernel,
        grid=(n // tm,),
        in_specs=[pl.BlockSpec((tm, D_MODEL), row), pl.BlockSpec((tm, WIDTH_A), row),
                  pl.BlockSpec((tm, WIDTH_B), row)] + [_const_spec(c.shape) for c in consts],
        out_specs=pl.BlockSpec((tm, D_MODEL), row),
        out_shape=jax.ShapeDtypeStruct((n, D_MODEL), F32),
        compiler_params=pltpu.CompilerParams(
            dimension_semantics=("parallel",), vmem_limit_bytes=VMEM_LIMIT),
        name="ffn",
    )(x2, a2, b2, *consts)


def _pad_heads(w, n_heads, width):
    k = w.shape[0]
    w = w.reshape(k, n_heads, width)
    return jnp.pad(w, ((0, 0), (0, 0), (0, LANES - width))).reshape(k, n_heads * LANES)


def _lane_row(g, offset=0):
    return jnp.pad(g.astype(F32), (offset, LANES - offset - g.shape[0]))[None, :]


def kernel(x, positions, norm_mix_g, w_in, qnorm_a_g, knorm_a_g, rel_bias, cq_norm_g, ckv_norm_g,
           w_uq, w_ukv, qnorm_b_g, knorm_b_g, w_o, norm_ffn_g, w_ff1, w_ff2):
    b, s, d = x.shape
    n = b * s
    c0 = 3 * WIDTH_A
    c1 = c0 + Q_LORA_RANK
    c2 = c1 + KV_LORA_RANK
    w_qkv = w_in[:, :c0].astype(BF16)
    w_cq = w_in[:, c0:c1].astype(BF16)
    w_ckv = w_in[:, c1:c2].astype(BF16)
    w_kr = jnp.pad(w_in[:, c2:], ((0, 0), (QK_NOPE_DIM, LANES - QK_HEAD_DIM_B))).astype(BF16)
    w_uqt = _pad_heads(w_uq, N_HEADS_B, QK_HEAD_DIM_B).T.astype(BF16)
    w_ukv3 = w_ukv.reshape(KV_LORA_RANK, N_HEADS_B, QK_NOPE_DIM + V_HEAD_DIM)
    w_uk_p = _pad_heads(w_ukv3[:, :, :QK_NOPE_DIM].reshape(KV_LORA_RANK, -1),
                        N_HEADS_B, QK_NOPE_DIM).astype(BF16)
    ones_rows = VT_ROWS - V_HEAD_DIM
    w_uvt = jnp.transpose(w_ukv3[:, :, QK_NOPE_DIM:], (1, 2, 0))
    w_uvt = jnp.pad(w_uvt, ((0, 0), (0, ones_rows), (0, 0)))
    w_uvt = w_uvt.reshape(N_HEADS_B * VT_ROWS, KV_LORA_RANK).astype(BF16)
    inv_freq = 1.0 / (ROPE_THETA ** (jnp.arange(0, QK_ROPE_DIM, 2, dtype=F32) / QK_ROPE_DIM))
    ones_col = jnp.tile(jnp.concatenate([jnp.zeros((V_HEAD_DIM,), F32), jnp.ones((ones_rows,), F32)]),
                        N_HEADS_B)[:, None]
    g_mix = norm_mix_g.astype(F32)[None, :]
    head_sum = jnp.asarray(np.kron(np.eye(N_HEADS_A), np.ones((HEAD_DIM_A, HEAD_DIM_A))), BF16)
    consts_a = (
        g_mix, w_qkv, head_sum,
        jnp.tile(qnorm_a_g.astype(F32), N_HEADS_A)[None, :],
        jnp.tile(knorm_a_g.astype(F32), N_HEADS_A)[None, :],
    )
    tm = TM_PROJ
    gq_col = jnp.pad(qnorm_b_g.astype(F32), (0, LANES - QK_HEAD_DIM_B))[:, None]
    consts_b = (
        g_mix, w_cq, w_ckv, w_kr,
        cq_norm_g.astype(F32)[None, :], ckv_norm_g.astype(F32)[None, :],
        w_uqt, w_uk_p, w_uvt,
        jnp.broadcast_to(gq_col, (LANES, tm)), _lane_row(knorm_b_g[:QK_NOPE_DIM]),
        _lane_row(knorm_b_g[QK_NOPE_DIM:], QK_NOPE_DIM),
        jnp.broadcast_to(inv_freq[:, None], (QK_ROPE_DIM // 2, tm)),
        ones_col,
    )
    x2 = x.reshape(n, d)
    qa, ka, va = _proj_a_call(x, consts_a)
    qbt, kb, vbt = _proj_b_call(x2, positions.reshape(n // tm, 1, tm), consts_b)

    out_a = _attn_a_call(qa, ka, va, _attn_a_bias(rel_bias))
    out_b = _attn_b_call(qbt, kb.reshape(b, s, -1), vbt)

    out = _ffn_call(x.reshape(n, d), out_a.reshape(n, -1), out_b.reshape(n, -1),
                    w_o[:WIDTH_A].astype(BF16), w_o[WIDTH_A:].astype(BF16),
                    norm_ffn_g.astype(F32)[None, :], w_ff1.astype(BF16), w_ff2.astype(BF16))
    return out.reshape(b, s, d)
```

```python
import functools
import math

import numpy as np
import jax
import jax.numpy as jnp
from jax import lax
from jax.experimental import pallas as pl
from jax.experimental.pallas import tpu as pltpu

F32 = jnp.float32
BF16 = jnp.bfloat16

D_MODEL = 1024
HEAD_DIM_A = 64
N_HEADS_A = 8
WIDTH_A = N_HEADS_A * HEAD_DIM_A
DILATED_BRANCHES = ((128, 1), (512, 4), (2048, 16))
N_HEADS_B = 8
QK_NOPE_DIM = 64
QK_ROPE_DIM = 32
V_HEAD_DIM = 64
QK_HEAD_DIM_B = QK_NOPE_DIM + QK_ROPE_DIM
Q_LORA_RANK = 768
KV_LORA_RANK = 256
WIDTH_B = N_HEADS_B * V_HEAD_DIM
ROPE_THETA = 10000.0
D_FF = 4 * D_MODEL
REL_BUCKETS = 32
REL_MAX_DIST = 2048
EPS = 1e-6

LANES = 128
HALF = LANES // 2
HEAD_PAIRS = N_HEADS_A // 2
WIN = 128
NEG = -1e30
VMEM_LIMIT = 56 * 1024 * 1024

TM_PROJ = 512
TM_FFN = 512
TQ_B = 512
TSUB_B = 256
FF_CHUNK = 1024
GROUP_A = 16
CLASSES = 4

LOG2E = math.log2(math.e)
SCALE_A = LOG2E / math.sqrt(HEAD_DIM_A)
SCALE_B = LOG2E / math.sqrt(QK_HEAD_DIM_B)

VT_ROWS = V_HEAD_DIM + 16

assert HEAD_DIM_A == HALF and V_HEAD_DIM == HALF
assert TM_PROJ == TQ_B


def _nt_dot(a, b):
    return lax.dot_general(a, b, (((1,), (1,)), ((), ())), preferred_element_type=F32)


def _dot(a, b):
    return jnp.dot(a, b, preferred_element_type=F32)


def _const_spec(shape):
    nd = len(shape)
    return pl.BlockSpec(shape, lambda *_: (0,) * nd, pipeline_mode=pl.Buffered(1))


def _rms_bf16(x, g_row):
    return (x * lax.rsqrt(jnp.mean(x * x, axis=-1, keepdims=True) + EPS) * g_row).astype(BF16)


def _proj_a_kernel(x_ref, gmix_ref, wqkv_ref, hsum_ref, gqa_ref, gka_ref, qa_ref, ka_ref, va_ref,
                   xs_ref):
    tm, d = x_ref.shape
    run = tm // CLASSES
    for c in range(d // LANES):
        xs_ref[c] = x_ref[:, c * LANES:(c + 1) * LANES]

    def head_sumsq(y):
        return _dot((y * y).astype(BF16), hsum_ref[...])

    sub = run // 2
    for h in range(2):
        x = jnp.concatenate(
            [jnp.concatenate([xs_ref[c, pl.ds(r + h * sub * CLASSES, sub, stride=CLASSES), :]
                              for r in range(CLASSES)], axis=0)
             for c in range(d // LANES)], axis=1)
        xn = _rms_bf16(x, gmix_ref[...])
        q = _dot(xn, wqkv_ref[:, :WIDTH_A])
        k = _dot(xn, wqkv_ref[:, WIDTH_A:2 * WIDTH_A])
        ssq = head_sumsq(q)
        v = _dot(xn, wqkv_ref[:, 2 * WIDTH_A:]).astype(BF16)
        ssk = head_sumsq(k)
        q = (q * lax.rsqrt(ssq / HEAD_DIM_A + EPS) * gqa_ref[...] * SCALE_A).astype(BF16)
        k = (k * lax.rsqrt(ssk / HEAD_DIM_A + EPS) * gka_ref[...]).astype(BF16)
        for r in range(CLASSES):
            src = slice(r * sub, (r + 1) * sub)
            dst = slice(h * sub, (h + 1) * sub)
            qa_ref[r, dst] = q[src]
            ka_ref[r, dst] = k[src]
            va_ref[r, dst] = v[src]


def _proj_a_call(x, consts):
    b, s, d = x.shape
    tm = TM_PROJ
    run = tm // CLASSES
    out = jax.ShapeDtypeStruct((b, CLASSES, s // CLASSES, WIDTH_A), BF16)
    outs = pl.pallas_call(
        _proj_a_kernel,
        grid=(b, s // tm),
        in_specs=[pl.BlockSpec((None, tm, d), lambda bb, i: (bb, i, 0))]
        + [_const_spec(c.shape) for c in consts],
        out_specs=[pl.BlockSpec((None, CLASSES, run, WIDTH_A), lambda bb, i: (bb, 0, i, 0))] * 3,
        out_shape=[out] * 3,
        scratch_shapes=[pltpu.VMEM((d // LANES, tm, LANES), F32)],
        compiler_params=pltpu.CompilerParams(
            dimension_semantics=("parallel", "parallel"), vmem_limit_bytes=VMEM_LIMIT),
        name="proj_a",
    )(x, *consts)
    return [o.reshape(b, s, WIDTH_A) for o in outs]


def _proj_b_kernel(x_ref, posr_ref, gmix_ref, wcq_ref, wckv_ref, wkr_ref,
                   gcq_ref, gckv_ref, wuqt_ref, wuk_ref, wuvt_ref,
                   gqbt_ref, gkbn_ref, gkbr_ref, invft_ref, ones_ref,
                   qt_ref, kb_ref, vt_ref):
    nparts = 1
    tm = x_ref.shape[0] // nparts
    half = QK_ROPE_DIM // 2
    r0, r1, r2 = QK_NOPE_DIM, QK_NOPE_DIM + half, QK_HEAD_DIM_B
    for part in range(nparts):
        toks = slice(part * tm, (part + 1) * tm)
        xn = _rms_bf16(x_ref[toks, :], gmix_ref[...])
        cq = _dot(xn, wcq_ref[...])
        ckv = _dot(xn, wckv_ref[...])
        kr = _dot(xn, wkr_ref[...])
        ang_t = invft_ref[...] * posr_ref[:, toks].astype(F32)
        cs_t = jnp.cos(ang_t)
        sn_t = jnp.sin(ang_t)

        ckvn = _rms_bf16(ckv, gckv_ref[...])
        kn = _dot(ckvn, wuk_ref[...])
        vt_ref[:, toks] = (_nt_dot(wuvt_ref[...], ckvn) + ones_ref[...]).astype(BF16)

        cqn = _rms_bf16(cq, gcq_ref[...])
        qt = _nt_dot(wuqt_ref[...], cqn)

        zt = jnp.zeros((r0, tm), F32)
        zb = jnp.zeros((LANES - r2, tm), F32)
        cs = jnp.concatenate([zt, cs_t, cs_t, zb], axis=0).T
        sn = jnp.concatenate([zt, -sn_t, sn_t, zb], axis=0).T
        ss_kr = jnp.sum(kr * kr, axis=-1, keepdims=True)
        krg = kr * gkbr_ref[...]
        kr_rot = krg * cs + (pltpu.roll(krg, LANES - half, 1) + pltpu.roll(krg, half, 1)) * sn
        for h in range(N_HEADS_B):
            sl = slice(h * LANES, (h + 1) * LANES)
            kh = kn[:, sl]
            rk = lax.rsqrt((jnp.sum(kh * kh, axis=-1, keepdims=True) + ss_kr) / QK_HEAD_DIM_B + EPS)
            kb_ref[toks, sl] = ((kh * gkbn_ref[...] + kr_rot) * rk).astype(BF16)

        for h in range(N_HEADS_B):
            qh = qt[h * LANES:(h + 1) * LANES]
            rq = lax.rsqrt(jnp.sum(qh * qh, axis=0, keepdims=True) / QK_HEAD_DIM_B + EPS)
            y = qh * rq * gqbt_ref[...]
            y1, y2 = y[r0:r1], y[r1:r2]
            out = jnp.concatenate([y[:r0], y1 * cs_t - y2 * sn_t, y2 * cs_t + y1 * sn_t, y[r2:]],
                                  axis=0)
            qt_ref[h * LANES:(h + 1) * LANES, toks] = (out * SCALE_B).astype(BF16)


def _proj_b_call(x2, pos_row, consts):
    n = x2.shape[0]
    tm = TM_PROJ
    row = lambda i: (i, 0)
    blk3 = lambda i: (i, 0, 0)
    wide = N_HEADS_B * LANES
    vt_rows = N_HEADS_B * VT_ROWS
    return pl.pallas_call(
        _proj_b_kernel,
        grid=(n // tm,),
        in_specs=[pl.BlockSpec((tm, D_MODEL), row),
                  pl.BlockSpec((None, 1, tm), blk3)] + [_const_spec(c.shape) for c in consts],
        out_specs=[pl.BlockSpec((None, wide, tm), blk3), pl.BlockSpec((tm, wide), row),
                   pl.BlockSpec((None, vt_rows, tm), blk3)],
        out_shape=[jax.ShapeDtypeStruct((n // tm, wide, tm), BF16),
                   jax.ShapeDtypeStruct((n, wide), BF16),
                   jax.ShapeDtypeStruct((n // tm, vt_rows, tm), BF16)],
        compiler_params=pltpu.CompilerParams(
            dimension_semantics=("parallel",), vmem_limit_bytes=VMEM_LIMIT),
        name="proj_b",
    )(x2, pos_row, *consts)


def _attn_a_kernel(q_ref, k_ref, v_ref, bias_ref, o_ref,
                   qf, kf, vf, acc, den, m0, m1, onat, *, seq, pad):
    lane = lax.broadcasted_iota(jnp.int32, (1, LANES), 1)
    first_head = lane < HALF
    qf[...] = q_ref[...].astype(F32)
    zeros = jnp.zeros((pad, LANES), F32)
    kf[pl.ds(0, pad), :] = zeros
    vf[pl.ds(0, pad), :] = zeros
    kf[pl.ds(pad, seq), :] = k_ref[...].astype(F32)
    vf[pl.ds(pad, seq), :] = v_ref[...].astype(F32)

    cls_rows = seq // CLASSES

    for bi, (window, dil) in enumerate(DILATED_BRANCHES):
        assert window // dil == WIN
        nchunk = max(CLASSES // dil, 1)
        clen = WIN // nchunk
        stride = max(dil // CLASSES, 1)
        step = stride * clen
        nblk = seq // (dil * WIN)
        ublk = min(GROUP_A, nblk)
        ncls = GROUP_A // ublk

        def run(start, clen=clen, stride=stride):
            return pl.ds(start, clen, stride=stride) if stride > 1 else pl.ds(start, clen)

        def gather(ref, starts, run=run):
            return jnp.concatenate([ref[run(st), :] for st in starts], axis=0)

        def group(r0, g, bi=bi, dil=dil, nchunk=nchunk, clen=clen, stride=stride, step=step,
                  ublk=ublk, ncls=ncls, run=run, gather=gather):
            plans = []
            for cc in range(ncls):
                rd = r0 + cc
                if stride == 1:
                    bases = [pl.multiple_of((dil * c + rd) * cls_rows + step * ublk * g, 8)
                             for c in range(nchunk)]
                else:
                    bases = [(rd % CLASSES) * cls_rows + rd // CLASSES + step * ublk * g]
                for u in range(ublk):
                    plans.append((bases, u))

            def scores(bases, u):
                qstarts = [b0 + step * u for b0 in bases]
                kstarts = [pad + b0 + step * (u - 1) for b0 in bases]
                q = gather(qf, qstarts)
                qq = jnp.concatenate([jnp.where(first_head, q, 0.0),
                                      jnp.where(first_head, 0.0, q)], axis=0).astype(BF16)
                kk = jnp.concatenate([gather(kf, kstarts),
                                      gather(kf, [k0 + step for k0 in kstarts])],
                                     axis=0).astype(BF16)
                vv = jnp.concatenate([gather(vf, kstarts),
                                      gather(vf, [k0 + step for k0 in kstarts])],
                                     axis=0).astype(BF16)
                first_blk = jnp.where(g == 0, 1, 0) if u == 0 else 0
                s = _nt_dot(qq, kk) + bias_ref[bi, first_blk]
                return qstarts, s, jnp.concatenate([vv, jnp.ones_like(vv)], axis=1)

            def update(qstarts, s, vext):
                ps, alphas = [], []
                for hh, m_s in enumerate((m0, m1)):
                    sh = s[hh * WIN:(hh + 1) * WIN]
                    m_blk = jnp.max(sh, axis=-1, keepdims=True)
                    if bi == 0:
                        m_new = jnp.broadcast_to(m_blk, (WIN, LANES))
                    else:
                        m_old = gather(m_s, qstarts)
                        m_new = jnp.maximum(m_old, m_blk)
                        alphas.append(jnp.exp2(m_old - m_new))
                    ps.append(jnp.exp2(sh - jnp.concatenate([m_new, m_new], axis=1)).astype(BF16))
                    for c, st in enumerate(qstarts):
                        m_s[run(st), :] = m_new[c * clen:(c + 1) * clen]
                res = _dot(jnp.concatenate(ps, axis=0), vext)
                acc_new = jnp.where(first_head, res[:WIN, :LANES], res[WIN:, :LANES])
                den_new = jnp.where(first_head, res[:WIN, LANES:], res[WIN:, LANES:])
                if bi > 0:
                    alpha = jnp.where(first_head, alphas[0], alphas[1])
                    acc_new = alpha * gather(acc, qstarts) + acc_new
                    den_new = alpha * gather(den, qstarts) + den_new
                for c, st in enumerate(qstarts):
                    acc[run(st), :] = acc_new[c * clen:(c + 1) * clen]
                    den[run(st), :] = den_new[c * clen:(c + 1) * clen]

            pending = scores(*plans[0])
            for nxt in plans[1:]:
                ahead = scores(*nxt)
                update(*pending)
                pending = ahead
            update(*pending)

        def class_body(ci, carry, ngrp=nblk // ublk, ncls=ncls, group=group):
            def grp_body(g, c):
                group(ci * ncls, g)
                return c
            return lax.fori_loop(0, ngrp, grp_body, carry)

        lax.fori_loop(0, dil // ncls, class_body, 0)

    for r in range(CLASSES):
        rows = pl.ds(r * cls_rows, cls_rows)
        onat[pl.ds(r, cls_rows, stride=CLASSES), :] = acc[rows, :] / den[rows, :]
    o_ref[...] = onat[...].astype(o_ref.dtype)


def _attn_a_call(qa, ka, va, bias):
    b, s, _ = qa.shape
    pad = WIN * DILATED_BRANCHES[-1][1] // CLASSES
    blk = pl.BlockSpec((None, s, LANES), lambda bb, p: (bb, 0, p))
    bias_spec = pl.BlockSpec((None,) + bias.shape[1:], lambda bb, p: (p, 0, 0, 0, 0))
    rows = pltpu.VMEM((s, LANES), F32)
    padded = pltpu.VMEM((pad + s, LANES), F32)
    return pl.pallas_call(
        functools.partial(_attn_a_kernel, seq=s, pad=pad),
        grid=(b, HEAD_PAIRS),
        in_specs=[blk, blk, blk, bias_spec],
        out_specs=blk,
        out_shape=jax.ShapeDtypeStruct((b, s, WIDTH_A), BF16),
        scratch_shapes=[rows, padded, padded, rows, rows, rows, rows, rows],
        compiler_params=pltpu.CompilerParams(
            dimension_semantics=("parallel", "parallel"), vmem_limit_bytes=VMEM_LIMIT),
        name="attn_a",
    )(qa, ka, va, bias)


def _t5_causal_bucket(dist):
    dist = np.asarray(dist, dtype=np.int64)
    max_exact = REL_BUCKETS // 2
    safe = np.maximum(dist, 1).astype(np.float32)
    large = max_exact + (np.log(safe / max_exact) / math.log(REL_MAX_DIST / max_exact)
                         * (REL_BUCKETS - max_exact)).astype(np.int64)
    large = np.minimum(large, REL_BUCKETS - 1)
    return np.where(dist < max_exact, dist, large).astype(np.int32)


def _bias_bucket_tables():
    i = np.arange(WIN)[:, None]
    c = np.arange(2 * WIN)[None, :]
    sub = WIN + i - c
    tables = []
    for _, dil in DILATED_BRANCHES:
        bucket = _t5_causal_bucket(np.clip(sub, 0, WIN) * dil)
        bucket = np.where((sub >= 0) & (sub <= WIN), bucket, REL_BUCKETS)
        nchunk = max(CLASSES // dil, 1)
        clen = WIN // nchunk
        bucket = bucket.reshape(clen, nchunk, 2, clen, nchunk)
        tables.append(np.transpose(bucket, (1, 0, 2, 4, 3)).reshape(WIN, 2 * WIN))
    return np.stack(tables).astype(np.int32)


def _bias_kernel(bias_ref, bucket_ref, o_ref):
    bucket = bucket_ref[...]
    first_blk_cols = lax.broadcasted_iota(jnp.int32, bucket.shape, 1) >= WIN
    for h in range(N_HEADS_A):
        t = jnp.full(bucket.shape, NEG, F32)
        for bkt in range(REL_BUCKETS):
            t = jnp.where(bucket == bkt, bias_ref[bkt, h], t)
        rows = slice((h % 2) * WIN, (h % 2 + 1) * WIN)
        o_ref[h // 2, 0, rows, :] = t
        o_ref[h // 2, 1, rows, :] = jnp.where(first_blk_cols, t, NEG)


def _attn_a_bias(rel_bias):
    buckets = jnp.asarray(_bias_bucket_tables())
    nbr = len(DILATED_BRANCHES)
    return pl.pallas_call(
        _bias_kernel,
        grid=(nbr,),
        in_specs=[pl.BlockSpec(memory_space=pltpu.SMEM),
                  pl.BlockSpec((None, WIN, 2 * WIN), lambda i: (i, 0, 0))],
        out_specs=pl.BlockSpec((HEAD_PAIRS, None, 2, 2 * WIN, 2 * WIN), lambda i: (0, i, 0, 0, 0)),
        out_shape=jax.ShapeDtypeStruct((HEAD_PAIRS, nbr, 2, 2 * WIN, 2 * WIN), F32),
        compiler_params=pltpu.CompilerParams(dimension_semantics=("parallel",)),
        name="bias_a",
    )(rel_bias.astype(F32) * LOG2E, buckets)


def _attn_b_kernel(qt_ref, k_ref, vt_ref, o_ref, acc0, acc1, m0, m1, s_even, s_odd, mb_even, mb_odd):
    tq = TQ_B
    nq = qt_ref.shape[0]
    heads = ((acc0, m0), (acc1, m1))
    bufs = ((s_even, mb_even), (s_odd, mb_odd))

    def scores(qi, kb, s_buf, mb_buf):
        krows = pl.ds(pl.multiple_of(kb * tq, tq), tq)
        for hh in range(2):
            sl = slice(hh * LANES, (hh + 1) * LANES)
            st = _dot(k_ref[krows, sl], qt_ref[qi, hh])
            s_buf[hh] = st
            mb_buf[hh] = jnp.max(st, axis=0, keepdims=True)

    def update(kb, s_buf, mb_buf):
        for hh, (acc, m_s) in enumerate(heads):
            m_old = m_s[...]
            m_new = jnp.maximum(m_old, mb_buf[hh])
            alpha = jnp.exp2(m_old - m_new)
            pt = jnp.exp2(s_buf[hh] - m_new).astype(BF16)
            acc[...] = alpha * acc[...] + _dot(vt_ref[kb, hh], pt)
            m_s[...] = m_new

    hq = tq // 2

    def scores_diag(qi, s_buf, mb_buf):
        k0 = qi * tq
        for hh in range(2):
            sl = slice(hh * LANES, (hh + 1) * LANES)
            s_buf[hh, :hq, :] = _dot(k_ref[pl.ds(k0, hq), sl], qt_ref[qi, hh])
            s_buf[hh, hq:, :hq] = _dot(k_ref[pl.ds(k0 + hq, hq), sl],
                                       qt_ref[qi, hh, :, hq:])

    def update_diag(qi, s_buf, mb_buf):
        def causal(shape):
            return (lax.broadcasted_iota(jnp.int32, shape, 0)
                    <= lax.broadcasted_iota(jnp.int32, shape, 1))

        for hh, (acc, m_s) in enumerate(heads):
            top = jnp.where(causal((hq, tq)), s_buf[hh, :hq, :], NEG)
            bot = jnp.where(causal((hq, hq)), s_buf[hh, hq:, :hq], NEG)
            m_top = jnp.max(top, axis=0, keepdims=True)
            m_bot = jnp.max(jnp.concatenate([jnp.full((hq, hq), NEG, F32), bot], axis=1),
                            axis=0, keepdims=True)
            m_old = m_s[...]
            m_new = jnp.maximum(m_old, jnp.maximum(m_top, m_bot))
            alpha = jnp.exp2(m_old - m_new)
            m_s[...] = m_new
            pt_top = jnp.exp2(top - m_new).astype(BF16)
            pt_bot = jnp.exp2(bot - m_s[:, hq:]).astype(BF16)
            pv = _dot(vt_ref[qi, hh, :, :hq], pt_top)
            pv_bot = _dot(vt_ref[qi, hh, :, hq:], pt_bot)
            acc[...] = alpha * acc[...] + pv + jnp.concatenate(
                [jnp.zeros((VT_ROWS, hq), F32), pv_bot], axis=1)

    scores_diag(0, *bufs[0])
    step0 = 0
    for qi in range(nq):
        cur, nxt = bufs[step0 % 2], bufs[(step0 + 1) % 2]
        for acc, m_s in heads:
            acc[...] = jnp.zeros_like(acc)
            m_s[...] = jnp.full(m_s.shape, NEG, F32)

        def pair_body(i, c, qi=qi, cur=cur, nxt=nxt):
            kb = 2 * i
            scores(qi, kb + 1, *nxt)
            update(kb, *cur)
            scores(qi, kb + 2, *cur)
            update(kb + 1, *nxt)
            return c

        npairs = qi // 2
        peel = qi % 2 == 0 and npairs > 0
        lax.fori_loop(0, npairs - 1 if peel else npairs, pair_body, 0)
        if peel:
            scores(qi, qi - 1, *nxt)
            update(qi - 2, *cur)
            scores_diag(qi, *cur)
            update(qi - 1, *nxt)
        last = cur
        if qi % 2 == 1:
            scores_diag(qi, *nxt)
            update(qi - 1, *cur)
            last = nxt
        if qi + 1 < nq:
            scores(qi + 1, 0, *(cur if last is nxt else nxt))
        update_diag(qi, *last)

        outs = []
        for acc, _ in heads:
            a = acc[...]
            outs.append(a[:V_HEAD_DIM] / a[V_HEAD_DIM:V_HEAD_DIM + 1])
        o_ref[pl.ds(qi * tq, tq), :] = jnp.concatenate(outs, axis=0).T.astype(o_ref.dtype)
        step0 += qi + 1


def _attn_b_call(qbt, kb, vbt):
    b, s, _ = kb.shape
    tq = TQ_B
    nkb = s // tq
    qbt = qbt.reshape(b, nkb, HEAD_PAIRS, 2, LANES, tq)
    vbt = vbt.reshape(b, nkb, HEAD_PAIRS, 2, VT_ROWS, tq)
    return pl.pallas_call(
        _attn_b_kernel,
        grid=(b, HEAD_PAIRS),
        in_specs=[pl.BlockSpec((None, nkb, None, 2, LANES, tq), lambda bb, p: (bb, 0, p, 0, 0, 0)),
                  pl.BlockSpec((None, s, 2 * LANES), lambda bb, p: (bb, 0, p)),
                  pl.BlockSpec((None, nkb, None, 2, VT_ROWS, tq),
                               lambda bb, p: (bb, 0, p, 0, 0, 0))],
        out_specs=pl.BlockSpec((None, s, LANES), lambda bb, p: (bb, 0, p)),
        out_shape=jax.ShapeDtypeStruct((b, s, WIDTH_B), BF16),
        scratch_shapes=[pltpu.VMEM((VT_ROWS, tq), F32)] * 2 + [pltpu.VMEM((1, tq), F32)] * 2
        + [pltpu.VMEM((2, tq, tq), F32)] * 2 + [pltpu.VMEM((2, 1, tq), F32)] * 2,
        compiler_params=pltpu.CompilerParams(
            dimension_semantics=("parallel", "parallel"), vmem_limit_bytes=VMEM_LIMIT),
        name="attn_b",
    )(qbt, kb, vbt)


def _ffn_kernel(x_ref, a_ref, b_ref, woa_ref, wob_ref, g_ref, w1_ref, w2_ref, o_ref):
    h = x_ref[...] + _dot(a_ref[...], woa_ref[...]) + _dot(b_ref[...], wob_ref[...])
    hn = (h * lax.rsqrt(jnp.mean(h * h, axis=-1, keepdims=True) + EPS) * g_ref[...]).astype(BF16)
    mlp = None
    for c in range(D_FF // FF_CHUNK):
        sl = slice(c * FF_CHUNK, (c + 1) * FF_CHUNK)
        hid = jnp.square(jnp.maximum(_dot(hn, w1_ref[:, sl]), 0.0)).astype(BF16)
        d = _dot(hid, w2_ref[sl, :])
        mlp = d if mlp is None else mlp + d
    o_ref[...] = h + mlp


def _ffn_call(x2, a2, b2, woa, wob, g, w1, w2):
    n = x2.shape[0]
    tm = TM_FFN
    row = lambda i: (i, 0)
    consts = (woa, wob, g, w1, w2)
    return pl.pallas_call(
        _ffn_kernel,
        grid=(n // tm,),
        in_specs=[pl.BlockSpec((tm, D_MODEL), row), pl.BlockSpec((tm, WIDTH_A), row),
                  pl.BlockSpec((tm, WIDTH_B), row)] + [_const_spec(c.shape) for c in consts],
        out_specs=pl.BlockSpec((tm, D_MODEL), row),
        out_shape=jax.ShapeDtypeStruct((n, D_MODEL), F32),
        compiler_params=pltpu.CompilerParams(
            dimension_semantics=("parallel",), vmem_limit_bytes=VMEM_LIMIT),
        name="ffn",
    )(x2, a2, b2, *consts)


def _pad_heads(w, n_heads, width):
    k = w.shape[0]
    w = w.reshape(k, n_heads, width)
    return jnp.pad(w, ((0, 0), (0, 0), (0, LANES - width))).reshape(k, n_heads * LANES)


def _lane_row(g, offset=0):
    return jnp.pad(g.astype(F32), (offset, LANES - offset - g.shape[0]))[None, :]


def kernel(x, positions, norm_mix_g, w_in, qnorm_a_g, knorm_a_g, rel_bias, cq_norm_g, ckv_norm_g,
           w_uq, w_ukv, qnorm_b_g, knorm_b_g, w_o, norm_ffn_g, w_ff1, w_ff2):
    b, s, d = x.shape
    n = b * s
    c0 = 3 * WIDTH_A
    c1 = c0 + Q_LORA_RANK
    c2 = c1 + KV_LORA_RANK
    w_qkv = w_in[:, :c0].astype(BF16)
    w_cq = w_in[:, c0:c1].astype(BF16)
    w_ckv = w_in[:, c1:c2].astype(BF16)
    w_kr = jnp.pad(w_in[:, c2:], ((0, 0), (QK_NOPE_DIM, LANES - QK_HEAD_DIM_B))).astype(BF16)
    w_uqt = _pad_heads(w_uq, N_HEADS_B, QK_HEAD_DIM_B).T.astype(BF16)
    w_ukv3 = w_ukv.reshape(KV_LORA_RANK, N_HEADS_B, QK_NOPE_DIM + V_HEAD_DIM)
    w_uk_p = _pad_heads(w_ukv3[:, :, :QK_NOPE_DIM].reshape(KV_LORA_RANK, -1),
                        N_HEADS_B, QK_NOPE_DIM).astype(BF16)
    ones_rows = VT_ROWS - V_HEAD_DIM
    w_uvt = jnp.transpose(w_ukv3[:, :, QK_NOPE_DIM:], (1, 2, 0))
    w_uvt = jnp.pad(w_uvt, ((0, 0), (0, ones_rows), (0, 0)))
    w_uvt = w_uvt.reshape(N_HEADS_B * VT_ROWS, KV_LORA_RANK).astype(BF16)
    inv_freq = 1.0 / (ROPE_THETA ** (jnp.arange(0, QK_ROPE_DIM, 2, dtype=F32) / QK_ROPE_DIM))
    ones_col = jnp.tile(jnp.concatenate([jnp.zeros((V_HEAD_DIM,), F32), jnp.ones((ones_rows,), F32)]),
                        N_HEADS_B)[:, None]
    g_mix = norm_mix_g.astype(F32)[None, :]
    head_sum = jnp.asarray(np.kron(np.eye(N_HEADS_A), np.ones((HEAD_DIM_A, HEAD_DIM_A))), BF16)
    consts_a = (
        g_mix, w_qkv, head_sum,
        jnp.tile(qnorm_a_g.astype(F32), N_HEADS_A)[None, :],
        jnp.tile(knorm_a_g.astype(F32), N_HEADS_A)[None, :],
    )
    tm = TM_PROJ
    gq_col = jnp.pad(qnorm_b_g.astype(F32), (0, LANES - QK_HEAD_DIM_B))[:, None]
    consts_b = (
        g_mix, w_cq, w_ckv, w_kr,
        cq_norm_g.astype(F32)[None, :], ckv_norm_g.astype(F32)[None, :],
        w_uqt, w_uk_p, w_uvt,
        jnp.broadcast_to(gq_col, (LANES, tm)), _lane_row(knorm_b_g[:QK_NOPE_DIM]),
        _lane_row(knorm_b_g[QK_NOPE_DIM:], QK_NOPE_DIM),
        jnp.broadcast_to(inv_freq[:, None], (QK_ROPE_DIM // 2, tm)),
        ones_col,
    )
    x2 = x.reshape(n, d)
    qa, ka, va = _proj_a_call(x, consts_a)
    qbt, kb, vbt = _proj_b_call(x2, positions.reshape(n // tm, 1, tm), consts_b)

    out_a = _attn_a_call(qa, ka, va, _attn_a_bias(rel_bias))
    out_b = _attn_b_call(qbt, kb.reshape(b, s, -1), vbt)

    out = _ffn_call(x.reshape(n, d), out_a.reshape(n, -1), out_b.reshape(n, -1),
                    w_o[:WIDTH_A].astype(BF16), w_o[WIDTH_A:].astype(BF16),
                    norm_ffn_g.astype(F32)[None, :], w_ff1.astype(BF16), w_ff2.astype(BF16))
    return out.reshape(b, s, d)
```

```python
import functools
import math

import numpy as np
import jax
import jax.numpy as jnp
from jax import lax
from jax.experimental import pallas as pl
from jax.experimental.pallas import tpu as pltpu

F32 = jnp.float32
BF16 = jnp.bfloat16

D_MODEL = 1024
HEAD_DIM_A = 64
N_HEADS_A = 8
WIDTH_A = N_HEADS_A * HEAD_DIM_A
DILATED_BRANCHES = ((128, 1), (512, 4), (2048, 16))
N_HEADS_B = 8
QK_NOPE_DIM = 64
QK_ROPE_DIM = 32
V_HEAD_DIM = 64
QK_HEAD_DIM_B = QK_NOPE_DIM + QK_ROPE_DIM
Q_LORA_RANK = 768
KV_LORA_RANK = 256
WIDTH_B = N_HEADS_B * V_HEAD_DIM
ROPE_THETA = 10000.0
D_FF = 4 * D_MODEL
REL_BUCKETS = 32
REL_MAX_DIST = 2048
EPS = 1e-6

LANES = 128
HALF = LANES // 2
HEAD_PAIRS = N_HEADS_A // 2
WIN = 128
NEG = -1e30
VMEM_LIMIT = 56 * 1024 * 1024

TM_PROJ = 512
TM_FFN = 512
TQ_B = 512
TSUB_B = 256
FF_CHUNK = 1024
GROUP_A = 32
CLASSES = 4

LOG2E = math.log2(math.e)
SCALE_A = LOG2E / math.sqrt(HEAD_DIM_A)
SCALE_B = LOG2E / math.sqrt(QK_HEAD_DIM_B)

VT_ROWS = V_HEAD_DIM + 16

assert HEAD_DIM_A == HALF and V_HEAD_DIM == HALF
assert TM_PROJ == TQ_B


def _nt_dot(a, b):
    return lax.dot_general(a, b, (((1,), (1,)), ((), ())), preferred_element_type=F32)


def _dot(a, b):
    return jnp.dot(a, b, preferred_element_type=F32)


def _const_spec(shape):
    nd = len(shape)
    return pl.BlockSpec(shape, lambda *_: (0,) * nd, pipeline_mode=pl.Buffered(1))


def _rms_bf16(x, g_row):
    return (x * lax.rsqrt(jnp.mean(x * x, axis=-1, keepdims=True) + EPS) * g_row).astype(BF16)


def _proj_a_kernel(x_ref, gmix_ref, wqkv_ref, hsum_ref, gqa_ref, gka_ref, qa_ref, ka_ref, va_ref,
                   xs_ref):
    tm, d = x_ref.shape
    run = tm // CLASSES
    for c in range(d // LANES):
        xs_ref[c] = x_ref[:, c * LANES:(c + 1) * LANES]

    def head_sumsq(y):
        return _dot((y * y).astype(BF16), hsum_ref[...])

    sub = run // 2
    for h in range(2):
        x = jnp.concatenate(
            [jnp.concatenate([xs_ref[c, pl.ds(r + h * sub * CLASSES, sub, stride=CLASSES), :]
                              for r in range(CLASSES)], axis=0)
             for c in range(d // LANES)], axis=1)
        xn = _rms_bf16(x, gmix_ref[...])
        q = _dot(xn, wqkv_ref[:, :WIDTH_A])
        k = _dot(xn, wqkv_ref[:, WIDTH_A:2 * WIDTH_A])
        ssq = head_sumsq(q)
        v = _dot(xn, wqkv_ref[:, 2 * WIDTH_A:]).astype(BF16)
        ssk = head_sumsq(k)
        q = (q * lax.rsqrt(ssq / HEAD_DIM_A + EPS) * gqa_ref[...] * SCALE_A).astype(BF16)
        k = (k * lax.rsqrt(ssk / HEAD_DIM_A + EPS) * gka_ref[...]).astype(BF16)
        for r in range(CLASSES):
            src = slice(r * sub, (r + 1) * sub)
            dst = slice(h * sub, (h + 1) * sub)
            qa_ref[r, dst] = q[src]
            ka_ref[r, dst] = k[src]
            va_ref[r, dst] = v[src]


def _proj_a_call(x, consts):
    b, s, d = x.shape
    tm = TM_PROJ
    run = tm // CLASSES
    out = jax.ShapeDtypeStruct((b, CLASSES, s // CLASSES, WIDTH_A), BF16)
    outs = pl.pallas_call(
        _proj_a_kernel,
        grid=(b, s // tm),
        in_specs=[pl.BlockSpec((None, tm, d), lambda bb, i: (bb, i, 0))]
        + [_const_spec(c.shape) for c in consts],
        out_specs=[pl.BlockSpec((None, CLASSES, run, WIDTH_A), lambda bb, i: (bb, 0, i, 0))] * 3,
        out_shape=[out] * 3,
        scratch_shapes=[pltpu.VMEM((d // LANES, tm, LANES), F32)],
        compiler_params=pltpu.CompilerParams(
            dimension_semantics=("parallel", "parallel"), vmem_limit_bytes=VMEM_LIMIT),
        name="proj_a",
    )(x, *consts)
    return [o.reshape(b, s, WIDTH_A) for o in outs]


def _proj_b_kernel(x_ref, posr_ref, gmix_ref, wcq_ref, wckv_ref, wkr_ref,
                   gcq_ref, gckv_ref, wuqt_ref, wuk_ref, wuvt_ref,
                   gqbt_ref, gkbn_ref, gkbr_ref, invft_ref, ones_ref,
                   qt_ref, kb_ref, vt_ref):
    tm = x_ref.shape[0]
    half = QK_ROPE_DIM // 2
    r0, r1, r2 = QK_NOPE_DIM, QK_NOPE_DIM + half, QK_HEAD_DIM_B
    xn = _rms_bf16(x_ref[...], gmix_ref[...])
    cq = _dot(xn, wcq_ref[...])
    ckv = _dot(xn, wckv_ref[...])
    kr = _dot(xn, wkr_ref[...])
    ang_t = invft_ref[...] * posr_ref[...].astype(F32)
    cs_t = jnp.cos(ang_t)
    sn_t = jnp.sin(ang_t)

    ckvn = _rms_bf16(ckv, gckv_ref[...])
    kn = _dot(ckvn, wuk_ref[...])
    vt_ref[...] = (_nt_dot(wuvt_ref[...], ckvn) + ones_ref[...]).astype(BF16)

    cqn = _rms_bf16(cq, gcq_ref[...])
    qt = _nt_dot(wuqt_ref[...], cqn)

    zt = jnp.zeros((r0, tm), F32)
    zb = jnp.zeros((LANES - r2, tm), F32)
    cs = jnp.concatenate([zt, cs_t, cs_t, zb], axis=0).T
    sn = jnp.concatenate([zt, -sn_t, sn_t, zb], axis=0).T
    ss_kr = jnp.sum(kr * kr, axis=-1, keepdims=True)
    krg = kr * gkbr_ref[...]
    kr_rot = krg * cs + (pltpu.roll(krg, LANES - half, 1) + pltpu.roll(krg, half, 1)) * sn
    for h in range(N_HEADS_B):
        sl = slice(h * LANES, (h + 1) * LANES)
        kh = kn[:, sl]
        rk = lax.rsqrt((jnp.sum(kh * kh, axis=-1, keepdims=True) + ss_kr) / QK_HEAD_DIM_B + EPS)
        kb_ref[:, sl] = ((kh * gkbn_ref[...] + kr_rot) * rk).astype(BF16)

    for h in range(N_HEADS_B):
        qh = qt[h * LANES:(h + 1) * LANES]
        rq = lax.rsqrt(jnp.sum(qh * qh, axis=0, keepdims=True) / QK_HEAD_DIM_B + EPS)
        y = qh * rq * gqbt_ref[...]
        y1, y2 = y[r0:r1], y[r1:r2]
        out = jnp.concatenate([y[:r0], y1 * cs_t - y2 * sn_t, y2 * cs_t + y1 * sn_t, y[r2:]], axis=0)
        qt_ref[h * LANES:(h + 1) * LANES, :] = (out * SCALE_B).astype(BF16)


def _proj_b_call(x2, pos_row, consts):
    n = x2.shape[0]
    tm = TM_PROJ
    row = lambda i: (i, 0)
    blk3 = lambda i: (i, 0, 0)
    wide = N_HEADS_B * LANES
    vt_rows = N_HEADS_B * VT_ROWS
    return pl.pallas_call(
        _proj_b_kernel,
        grid=(n // tm,),
        in_specs=[pl.BlockSpec((tm, D_MODEL), row),
                  pl.BlockSpec((None, 1, tm), blk3)] + [_const_spec(c.shape) for c in consts],
        out_specs=[pl.BlockSpec((None, wide, tm), blk3), pl.BlockSpec((tm, wide), row),
                   pl.BlockSpec((None, vt_rows, tm), blk3)],
        out_shape=[jax.ShapeDtypeStruct((n // tm, wide, tm), BF16),
                   jax.ShapeDtypeStruct((n, wide), BF16),
                   jax.ShapeDtypeStruct((n // tm, vt_rows, tm), BF16)],
        compiler_params=pltpu.CompilerParams(
            dimension_semantics=("parallel",), vmem_limit_bytes=VMEM_LIMIT),
        name="proj_b",
    )(x2, pos_row, *consts)


def _attn_a_kernel(q_ref, k_ref, v_ref, bias_ref, o_ref,
                   qf, kf, vf, acc, den, m0, m1, onat, *, seq, pad):
    lane = lax.broadcasted_iota(jnp.int32, (1, LANES), 1)
    first_head = lane < HALF
    qf[...] = q_ref[...].astype(F32)
    zeros = jnp.zeros((pad, LANES), F32)
    kf[pl.ds(0, pad), :] = zeros
    vf[pl.ds(0, pad), :] = zeros
    kf[pl.ds(pad, seq), :] = k_ref[...].astype(F32)
    vf[pl.ds(pad, seq), :] = v_ref[...].astype(F32)

    cls_rows = seq // CLASSES

    for bi, (window, dil) in enumerate(DILATED_BRANCHES):
        assert window // dil == WIN
        nchunk = max(CLASSES // dil, 1)
        clen = WIN // nchunk
        stride = max(dil // CLASSES, 1)
        step = stride * clen
        nblk = seq // (dil * WIN)
        ublk = min(GROUP_A, nblk)
        ncls = GROUP_A // ublk

        def run(start, clen=clen, stride=stride):
            return pl.ds(start, clen, stride=stride) if stride > 1 else pl.ds(start, clen)

        def gather(ref, starts, run=run):
            return jnp.concatenate([ref[run(st), :] for st in starts], axis=0)

        def group(r0, g, bi=bi, dil=dil, nchunk=nchunk, clen=clen, stride=stride, step=step,
                  ublk=ublk, ncls=ncls, run=run, gather=gather):
            plans = []
            for cc in range(ncls):
                rd = r0 + cc
                if stride == 1:
                    bases = [pl.multiple_of((dil * c + rd) * cls_rows + step * ublk * g, 8)
                             for c in range(nchunk)]
                else:
                    bases = [(rd % CLASSES) * cls_rows + rd // CLASSES + step * ublk * g]
                for u in range(ublk):
                    plans.append((bases, u))

            def scores(bases, u):
                qstarts = [b0 + step * u for b0 in bases]
                kstarts = [pad + b0 + step * (u - 1) for b0 in bases]
                q = gather(qf, qstarts)
                qq = jnp.concatenate([jnp.where(first_head, q, 0.0),
                                      jnp.where(first_head, 0.0, q)], axis=0).astype(BF16)
                kk = jnp.concatenate([gather(kf, kstarts),
                                      gather(kf, [k0 + step for k0 in kstarts])],
                                     axis=0).astype(BF16)
                vv = jnp.concatenate([gather(vf, kstarts),
                                      gather(vf, [k0 + step for k0 in kstarts])],
                                     axis=0).astype(BF16)
                first_blk = jnp.where(g == 0, 1, 0) if u == 0 else 0
                s = _nt_dot(qq, kk) + bias_ref[bi, first_blk]
                return qstarts, s, jnp.concatenate([vv, jnp.ones_like(vv)], axis=1)

            def update(qstarts, s, vext):
                ps, alphas = [], []
                for hh, m_s in enumerate((m0, m1)):
                    sh = s[hh * WIN:(hh + 1) * WIN]
                    m_blk = jnp.max(sh, axis=-1, keepdims=True)
                    if bi == 0:
                        m_new = jnp.broadcast_to(m_blk, (WIN, LANES))
                    else:
                        m_old = gather(m_s, qstarts)
                        m_new = jnp.maximum(m_old, m_blk)
                        alphas.append(jnp.exp2(m_old - m_new))
                    ps.append(jnp.exp2(sh - jnp.concatenate([m_new, m_new], axis=1)).astype(BF16))
                    for c, st in enumerate(qstarts):
                        m_s[run(st), :] = m_new[c * clen:(c + 1) * clen]
                res = _dot(jnp.concatenate(ps, axis=0), vext)
                acc_new = jnp.where(first_head, res[:WIN, :LANES], res[WIN:, :LANES])
                den_new = jnp.where(first_head, res[:WIN, LANES:], res[WIN:, LANES:])
                if bi > 0:
                    alpha = jnp.where(first_head, alphas[0], alphas[1])
                    acc_new = alpha * gather(acc, qstarts) + acc_new
                    den_new = alpha * gather(den, qstarts) + den_new
                for c, st in enumerate(qstarts):
                    acc[run(st), :] = acc_new[c * clen:(c + 1) * clen]
                    den[run(st), :] = den_new[c * clen:(c + 1) * clen]

            pending = scores(*plans[0])
            for nxt in plans[1:]:
                ahead = scores(*nxt)
                update(*pending)
                pending = ahead
            update(*pending)

        def class_body(ci, carry, ngrp=nblk // ublk, ncls=ncls, group=group):
            def grp_body(g, c):
                group(ci * ncls, g)
                return c
            return lax.fori_loop(0, ngrp, grp_body, carry)

        lax.fori_loop(0, dil // ncls, class_body, 0)

    for r in range(CLASSES):
        rows = pl.ds(r * cls_rows, cls_rows)
        onat[pl.ds(r, cls_rows, stride=CLASSES), :] = acc[rows, :] / den[rows, :]
    o_ref[...] = onat[...].astype(o_ref.dtype)


def _attn_a_call(qa, ka, va, bias):
    b, s, _ = qa.shape
    pad = WIN * DILATED_BRANCHES[-1][1] // CLASSES
    blk = pl.BlockSpec((None, s, LANES), lambda bb, p: (bb, 0, p))
    bias_spec = pl.BlockSpec((None,) + bias.shape[1:], lambda bb, p: (p, 0, 0, 0, 0))
    rows = pltpu.VMEM((s, LANES), F32)
    padded = pltpu.VMEM((pad + s, LANES), F32)
    return pl.pallas_call(
        functools.partial(_attn_a_kernel, seq=s, pad=pad),
        grid=(b, HEAD_PAIRS),
        in_specs=[blk, blk, blk, bias_spec],
        out_specs=blk,
        out_shape=jax.ShapeDtypeStruct((b, s, WIDTH_A), BF16),
        scratch_shapes=[rows, padded, padded, rows, rows, rows, rows, rows],
        compiler_params=pltpu.CompilerParams(
            dimension_semantics=("parallel", "parallel"), vmem_limit_bytes=VMEM_LIMIT),
        name="attn_a",
    )(qa, ka, va, bias)


def _t5_causal_bucket(dist):
    dist = np.asarray(dist, dtype=np.int64)
    max_exact = REL_BUCKETS // 2
    safe = np.maximum(dist, 1).astype(np.float32)
    large = max_exact + (np.log(safe / max_exact) / math.log(REL_MAX_DIST / max_exact)
                         * (REL_BUCKETS - max_exact)).astype(np.int64)
    large = np.minimum(large, REL_BUCKETS - 1)
    return np.where(dist < max_exact, dist, large).astype(np.int32)


def _bias_bucket_tables():
    i = np.arange(WIN)[:, None]
    c = np.arange(2 * WIN)[None, :]
    sub = WIN + i - c
    tables = []
    for _, dil in DILATED_BRANCHES:
        bucket = _t5_causal_bucket(np.clip(sub, 0, WIN) * dil)
        bucket = np.where((sub >= 0) & (sub <= WIN), bucket, REL_BUCKETS)
        nchunk = max(CLASSES // dil, 1)
        clen = WIN // nchunk
        bucket = bucket.reshape(clen, nchunk, 2, clen, nchunk)
        tables.append(np.transpose(bucket, (1, 0, 2, 4, 3)).reshape(WIN, 2 * WIN))
    return np.stack(tables).astype(np.int32)


def _bias_kernel(bias_ref, bucket_ref, o_ref):
    bucket = bucket_ref[...]
    first_blk_cols = lax.broadcasted_iota(jnp.int32, bucket.shape, 1) >= WIN
    for h in range(N_HEADS_A):
        t = jnp.full(bucket.shape, NEG, F32)
        for bkt in range(REL_BUCKETS):
            t = jnp.where(bucket == bkt, bias_ref[bkt, h], t)
        rows = slice((h % 2) * WIN, (h % 2 + 1) * WIN)
        o_ref[h // 2, 0, rows, :] = t
        o_ref[h // 2, 1, rows, :] = jnp.where(first_blk_cols, t, NEG)


def _attn_a_bias(rel_bias):
    buckets = jnp.asarray(_bias_bucket_tables())
    nbr = len(DILATED_BRANCHES)
    return pl.pallas_call(
        _bias_kernel,
        grid=(nbr,),
        in_specs=[pl.BlockSpec(memory_space=pltpu.SMEM),
                  pl.BlockSpec((None, WIN, 2 * WIN), lambda i: (i, 0, 0))],
        out_specs=pl.BlockSpec((HEAD_PAIRS, None, 2, 2 * WIN, 2 * WIN), lambda i: (0, i, 0, 0, 0)),
        out_shape=jax.ShapeDtypeStruct((HEAD_PAIRS, nbr, 2, 2 * WIN, 2 * WIN), F32),
        compiler_params=pltpu.CompilerParams(dimension_semantics=("parallel",)),
        name="bias_a",
    )(rel_bias.astype(F32) * LOG2E, buckets)


def _attn_b_kernel(qt_ref, k_ref, vt_ref, o_ref, acc0, acc1, m0, m1, s_even, s_odd, mb_even, mb_odd):
    tq = TQ_B
    nq = qt_ref.shape[0]
    heads = ((acc0, m0), (acc1, m1))
    bufs = ((s_even, mb_even), (s_odd, mb_odd))

    def scores(qi, kb, s_buf, mb_buf):
        krows = pl.ds(pl.multiple_of(kb * tq, tq), tq)
        for hh in range(2):
            sl = slice(hh * LANES, (hh + 1) * LANES)
            st = _dot(k_ref[krows, sl], qt_ref[qi, hh])
            s_buf[hh] = st
            mb_buf[hh] = jnp.max(st, axis=0, keepdims=True)

    def update(kb, s_buf, mb_buf):
        for hh, (acc, m_s) in enumerate(heads):
            m_old = m_s[...]
            m_new = jnp.maximum(m_old, mb_buf[hh])
            alpha = jnp.exp2(m_old - m_new)
            pt = jnp.exp2(s_buf[hh] - m_new).astype(BF16)
            acc[...] = alpha * acc[...] + _dot(vt_ref[kb, hh], pt)
            m_s[...] = m_new

    hq = tq // 2

    def scores_diag(qi, s_buf):
        k0 = qi * tq
        for hh in range(2):
            sl = slice(hh * LANES, (hh + 1) * LANES)
            s_buf[hh, :hq, :] = _dot(k_ref[pl.ds(k0, hq), sl], qt_ref[qi, hh])
            s_buf[hh, hq:, :hq] = _dot(k_ref[pl.ds(k0 + hq, hq), sl],
                                       qt_ref[qi, hh, :, hq:])

    def update_diag(qi, s_buf):
        def causal(shape):
            return (lax.broadcasted_iota(jnp.int32, shape, 0)
                    <= lax.broadcasted_iota(jnp.int32, shape, 1))

        for hh, (acc, m_s) in enumerate(heads):
            top = jnp.where(causal((hq, tq)), s_buf[hh, :hq, :], NEG)
            bot = jnp.where(causal((hq, hq)), s_buf[hh, hq:, :hq], NEG)
            m_top = jnp.max(top, axis=0, keepdims=True)
            m_bot = jnp.max(jnp.concatenate([jnp.full((hq, hq), NEG, F32), bot], axis=1),
                            axis=0, keepdims=True)
            m_old = m_s[...]
            m_new = jnp.maximum(m_old, jnp.maximum(m_top, m_bot))
            alpha = jnp.exp2(m_old - m_new)
            m_s[...] = m_new
            pt_top = jnp.exp2(top - m_new).astype(BF16)
            pt_bot = jnp.exp2(bot - m_s[:, hq:]).astype(BF16)
            pv = _dot(vt_ref[qi, hh, :, :hq], pt_top)
            pv_bot = _dot(vt_ref[qi, hh, :, hq:], pt_bot)
            acc[...] = alpha * acc[...] + pv + jnp.concatenate(
                [jnp.zeros((VT_ROWS, hq), F32), pv_bot], axis=1)

    scores_diag(0, bufs[0][0])
    step0 = 0
    for qi in range(nq):
        cur, nxt = bufs[step0 % 2], bufs[(step0 + 1) % 2]
        for acc, m_s in heads:
            acc[...] = jnp.zeros_like(acc)
            m_s[...] = jnp.full(m_s.shape, NEG, F32)

        def pair_body(i, c, qi=qi, cur=cur, nxt=nxt):
            kb = 2 * i
            scores(qi, kb + 1, *nxt)
            update(kb, *cur)
            scores(qi, kb + 2, *cur)
            update(kb + 1, *nxt)
            return c

        npairs = qi // 2
        peel = qi % 2 == 0 and npairs > 0
        lax.fori_loop(0, npairs - 1 if peel else npairs, pair_body, 0)
        if peel:
            scores(qi, qi - 1, *nxt)
            update(qi - 2, *cur)
            scores_diag(qi, cur[0])
            update(qi - 1, *nxt)
        last = cur
        if qi % 2 == 1:
            scores_diag(qi, nxt[0])
            update(qi - 1, *cur)
            last = nxt
        if qi + 1 < nq:
            scores(qi + 1, 0, *(cur if last is nxt else nxt))
        update_diag(qi, last[0])

        outs = []
        for acc, _ in heads:
            a = acc[...]
            outs.append(a[:V_HEAD_DIM] / a[V_HEAD_DIM:V_HEAD_DIM + 1])
        o_ref[pl.ds(qi * tq, tq), :] = jnp.concatenate(outs, axis=0).T.astype(o_ref.dtype)
        step0 += qi + 1


def _attn_b_call(qbt, kb, vbt):
    b, s, _ = kb.shape
    tq = TQ_B
    nkb = s // tq
    qbt = qbt.reshape(b, nkb, HEAD_PAIRS, 2, LANES, tq)
    vbt = vbt.reshape(b, nkb, HEAD_PAIRS, 2, VT_ROWS, tq)
    return pl.pallas_call(
        _attn_b_kernel,
        grid=(b, HEAD_PAIRS),
        in_specs=[pl.BlockSpec((None, nkb, None, 2, LANES, tq), lambda bb, p: (bb, 0, p, 0, 0, 0)),
                  pl.BlockSpec((None, s, 2 * LANES), lambda bb, p: (bb, 0, p)),
                  pl.BlockSpec((None, nkb, None, 2, VT_ROWS, tq),
                               lambda bb, p: (bb, 0, p, 0, 0, 0))],
        out_specs=pl.BlockSpec((None, s, LANES), lambda bb, p: (bb, 0, p)),
        out_shape=jax.ShapeDtypeStruct((b, s, WIDTH_B), BF16),
        scratch_shapes=[pltpu.VMEM((VT_ROWS, tq), F32)] * 2 + [pltpu.VMEM((1, tq), F32)] * 2
        + [pltpu.VMEM((2, tq, tq), F32)] * 2 + [pltpu.VMEM((2, 1, tq), F32)] * 2,
        compiler_params=pltpu.CompilerParams(
            dimension_semantics=("parallel", "parallel"), vmem_limit_bytes=VMEM_LIMIT),
        name="attn_b",
    )(qbt, kb, vbt)


def _ffn_kernel(x_ref, a_ref, b_ref, woa_ref, wob_ref, g_ref, w1_ref, w2_ref, o_ref):
    h = x_ref[...] + _dot(a_ref[...], woa_ref[...]) + _dot(b_ref[...], wob_ref[...])
    hn = (h * lax.rsqrt(jnp.mean(h * h, axis=-1, keepdims=True) + EPS) * g_ref[...]).astype(BF16)
    mlp = None
    for c in range(D_FF // FF_CHUNK):
        sl = slice(c * FF_CHUNK, (c + 1) * FF_CHUNK)
        hid = jnp.square(jnp.maximum(_dot(hn, w1_ref[:, sl]), 0.0)).astype(BF16)
        d = _dot(hid, w2_ref[sl, :])
        mlp = d if mlp is None else mlp + d
    o_ref[...] = h + mlp


def _ffn_call(x2, a2, b2, woa, wob, g, w1, w2):
    n = x2.shape[0]
    tm = TM_FFN
    row = lambda i: (i, 0)
    consts = (woa, wob, g, w1, w2)
    return pl.pallas_call(
        _ffn_kernel,
        grid=(n // tm,),
        in_specs=[pl.BlockSpec((tm, D_MODEL), row), pl.BlockSpec((tm, WIDTH_A), row),
                  pl.BlockSpec((tm, WIDTH_B), row)] + [_const_spec(c.shape) for c in consts],
        out_specs=pl.BlockSpec((tm, D_MODEL), row),
        out_shape=jax.ShapeDtypeStruct((n, D_MODEL), F32),
        compiler_params=pltpu.CompilerParams(
            dimension_semantics=("parallel",), vmem_limit_bytes=VMEM_LIMIT),
        name="ffn",
    )(x2, a2, b2, *consts)


def _pad_heads(w, n_heads, width):
    k = w.shape[0]
    w = w.reshape(k, n_heads, width)
    return jnp.pad(w, ((0, 0), (0, 0), (0, LANES - width))).reshape(k, n_heads * LANES)


def _lane_row(g, offset=0):
    return jnp.pad(g.astype(F32), (offset, LANES - offset - g.shape[0]))[None, :]


def kernel(x, positions, norm_mix_g, w_in, qnorm_a_g, knorm_a_g, rel_bias, cq_norm_g, ckv_norm_g,
           w_uq, w_ukv, qnorm_b_g, knorm_b_g, w_o, norm_ffn_g, w_ff1, w_ff2):
    b, s, d = x.shape
    n = b * s
    c0 = 3 * WIDTH_A
    c1 = c0 + Q_LORA_RANK
    c2 = c1 + KV_LORA_RANK
    w_qkv = w_in[:, :c0].astype(BF16)
    w_cq = w_in[:, c0:c1].astype(BF16)
    w_ckv = w_in[:, c1:c2].astype(BF16)
    w_kr = jnp.pad(w_in[:, c2:], ((0, 0), (QK_NOPE_DIM, LANES - QK_HEAD_DIM_B))).astype(BF16)
    w_uqt = _pad_heads(w_uq, N_HEADS_B, QK_HEAD_DIM_B).T.astype(BF16)
    w_ukv3 = w_ukv.reshape(KV_LORA_RANK, N_HEADS_B, QK_NOPE_DIM + V_HEAD_DIM)
    w_uk_p = _pad_heads(w_ukv3[:, :, :QK_NOPE_DIM].reshape(KV_LORA_RANK, -1),
                        N_HEADS_B, QK_NOPE_DIM).astype(BF16)
    ones_rows = VT_ROWS - V_HEAD_DIM
    w_uvt = jnp.transpose(w_ukv3[:, :, QK_NOPE_DIM:], (1, 2, 0))
    w_uvt = jnp.pad(w_uvt, ((0, 0), (0, ones_rows), (0, 0)))
    w_uvt = w_uvt.reshape(N_HEADS_B * VT_ROWS, KV_LORA_RANK).astype(BF16)
    inv_freq = 1.0 / (ROPE_THETA ** (jnp.arange(0, QK_ROPE_DIM, 2, dtype=F32) / QK_ROPE_DIM))
    ones_col = jnp.tile(jnp.concatenate([jnp.zeros((V_HEAD_DIM,), F32), jnp.ones((ones_rows,), F32)]),
                        N_HEADS_B)[:, None]
    g_mix = norm_mix_g.astype(F32)[None, :]
    head_sum = jnp.asarray(np.kron(np.eye(N_HEADS_A), np.ones((HEAD_DIM_A, HEAD_DIM_A))), BF16)
    consts_a = (
        g_mix, w_qkv, head_sum,
        jnp.tile(qnorm_a_g.astype(F32), N_HEADS_A)[None, :],
        jnp.tile(knorm_a_g.astype(F32), N_HEADS_A)[None, :],
    )
    tm = TM_PROJ
    gq_col = jnp.pad(qnorm_b_g.astype(F32), (0, LANES - QK_HEAD_DIM_B))[:, None]
    consts_b = (
        g_mix, w_cq, w_ckv, w_kr,
        cq_norm_g.astype(F32)[None, :], ckv_norm_g.astype(F32)[None, :],
        w_uqt, w_uk_p, w_uvt,
        jnp.broadcast_to(gq_col, (LANES, tm)), _lane_row(knorm_b_g[:QK_NOPE_DIM]),
        _lane_row(knorm_b_g[QK_NOPE_DIM:], QK_NOPE_DIM),
        jnp.broadcast_to(inv_freq[:, None], (QK_ROPE_DIM // 2, tm)),
        ones_col,
    )
    x2 = x.reshape(n, d)
    qa, ka, va = _proj_a_call(x, consts_a)
    qbt, kb, vbt = _proj_b_call(x2, positions.reshape(n // tm, 1, tm), consts_b)

    out_a = _attn_a_call(qa, ka, va, _attn_a_bias(rel_bias))
    out_b = _attn_b_call(qbt, kb.reshape(b, s, -1), vbt)

    out = _ffn_call(x.reshape(n, d), out_a.reshape(n, -1), out_b.reshape(n, -1),
                    w_o[:WIDTH_A].astype(BF16), w_o[WIDTH_A:].astype(BF16),
                    norm_ffn_g.astype(F32)[None, :], w_ff1.astype(BF16), w_ff2.astype(BF16))
    return out.reshape(b, s, d)
```

```python
import functools
import math

import numpy as np
import jax
import jax.numpy as jnp
from jax import lax
from jax.experimental import pallas as pl
from jax.experimental.pallas import tpu as pltpu

F32 = jnp.float32
BF16 = jnp.bfloat16

D_MODEL = 1024
HEAD_DIM_A = 64
N_HEADS_A = 8
WIDTH_A = N_HEADS_A * HEAD_DIM_A
DILATED_BRANCHES = ((128, 1), (512, 4), (2048, 16))
N_HEADS_B = 8
QK_NOPE_DIM = 64
QK_ROPE_DIM = 32
V_HEAD_DIM = 64
QK_HEAD_DIM_B = QK_NOPE_DIM + QK_ROPE_DIM
Q_LORA_RANK = 768
KV_LORA_RANK = 256
WIDTH_B = N_HEADS_B * V_HEAD_DIM
ROPE_THETA = 10000.0
D_FF = 4 * D_MODEL
REL_BUCKETS = 32
REL_MAX_DIST = 2048
EPS = 1e-6

LANES = 128
HALF = LANES // 2
HEAD_PAIRS = N_HEADS_A // 2
WIN = 128
NEG = -1e30
VMEM_LIMIT = 56 * 1024 * 1024

TM_PROJ = 512
TM_FFN = 512
TQ_B = 512
FF_CHUNK = 1024
GROUP_A = 32
CLASSES = 4

LOG2E = math.log2(math.e)
SCALE_A = LOG2E / math.sqrt(HEAD_DIM_A)
SCALE_B = LOG2E / math.sqrt(QK_HEAD_DIM_B)

BF16_ROWS = 16
VT_ROWS = V_HEAD_DIM + BF16_ROWS

assert HEAD_DIM_A == HALF and V_HEAD_DIM == HALF
assert TM_PROJ == TQ_B


def _nt_dot(a, b):
    return lax.dot_general(a, b, (((1,), (1,)), ((), ())), preferred_element_type=F32)


def _dot(a, b):
    return jnp.dot(a, b, preferred_element_type=F32)


def _const_spec(shape):
    nd = len(shape)
    return pl.BlockSpec(shape, lambda *_: (0,) * nd, pipeline_mode=pl.Buffered(1))


def _rms_bf16(x, g_row):
    return (x * lax.rsqrt(jnp.mean(x * x, axis=-1, keepdims=True) + EPS) * g_row).astype(BF16)


def _proj_a_kernel(x_ref, gmix_ref, wqkv_ref, hsum_ref, gqa_ref, gka_ref, qa_ref, ka_ref, va_ref,
                   xs_ref):
    tm, d = x_ref.shape
    run = tm // CLASSES
    for c in range(d // LANES):
        xs_ref[c] = x_ref[:, c * LANES:(c + 1) * LANES]

    def head_sumsq(y):
        return _dot((y * y).astype(BF16), hsum_ref[...])

    sub = run // 2
    for h in range(2):
        x = jnp.concatenate(
            [jnp.concatenate([xs_ref[c, pl.ds(r + h * sub * CLASSES, sub, stride=CLASSES), :]
                              for r in range(CLASSES)], axis=0)
             for c in range(d // LANES)], axis=1)
        xn = _rms_bf16(x, gmix_ref[...])
        q = _dot(xn, wqkv_ref[:, :WIDTH_A])
        k = _dot(xn, wqkv_ref[:, WIDTH_A:2 * WIDTH_A])
        ssq = head_sumsq(q)
        v = _dot(xn, wqkv_ref[:, 2 * WIDTH_A:]).astype(BF16)
        ssk = head_sumsq(k)
        q = (q * lax.rsqrt(ssq / HEAD_DIM_A + EPS) * gqa_ref[...] * SCALE_A).astype(BF16)
        k = (k * lax.rsqrt(ssk / HEAD_DIM_A + EPS) * gka_ref[...]).astype(BF16)
        for r in range(CLASSES):
            src = slice(r * sub, (r + 1) * sub)
            dst = slice(h * sub, (h + 1) * sub)
            qa_ref[r, dst] = q[src]
            ka_ref[r, dst] = k[src]
            va_ref[r, dst] = v[src]


def _proj_a_call(x, consts):
    b, s, d = x.shape
    tm = TM_PROJ
    run = tm // CLASSES
    out = jax.ShapeDtypeStruct((b, CLASSES, s // CLASSES, WIDTH_A), BF16)
    outs = pl.pallas_call(
        _proj_a_kernel,
        grid=(b, s // tm),
        in_specs=[pl.BlockSpec((None, tm, d), lambda bb, i: (bb, i, 0))]
        + [_const_spec(c.shape) for c in consts],
        out_specs=[pl.BlockSpec((None, CLASSES, run, WIDTH_A), lambda bb, i: (bb, 0, i, 0))] * 3,
        out_shape=[out] * 3,
        scratch_shapes=[pltpu.VMEM((d // LANES, tm, LANES), F32)],
        compiler_params=pltpu.CompilerParams(
            dimension_semantics=("parallel", "parallel"), vmem_limit_bytes=VMEM_LIMIT),
        name="proj_a",
    )(x, *consts)
    return [o.reshape(b, s, WIDTH_A) for o in outs]


def _proj_b_kernel(x_ref, posr_ref, gmix_ref, wcq_ref, wckv_ref, wkr_ref,
                   gcq_ref, gckv_ref, wuqt_ref, wuk_ref, wuvt_ref,
                   gqbt_ref, gkbn_ref, gkbr_ref, invft_ref, ones_ref,
                   qt_ref, kb_ref, vt_ref):
    tm = x_ref.shape[0]
    half = QK_ROPE_DIM // 2
    r0, r1, r2 = QK_NOPE_DIM, QK_NOPE_DIM + half, QK_HEAD_DIM_B
    xn = _rms_bf16(x_ref[...], gmix_ref[...])
    cq = _dot(xn, wcq_ref[...])
    ckv = _dot(xn, wckv_ref[...])
    kr = _dot(xn, wkr_ref[...])
    ang_t = invft_ref[...] * posr_ref[...].astype(F32)
    cs_t = jnp.cos(ang_t)
    sn_t = jnp.sin(ang_t)

    ckvn = _rms_bf16(ckv, gckv_ref[...])
    kn = _dot(ckvn, wuk_ref[...])
    vt_ref[...] = (_nt_dot(wuvt_ref[...], ckvn) + ones_ref[...]).astype(BF16)

    cqn = _rms_bf16(cq, gcq_ref[...])
    qt = _nt_dot(wuqt_ref[...], cqn)

    zt = jnp.zeros((r0, tm), F32)
    zb = jnp.zeros((LANES - r2, tm), F32)
    cs = jnp.concatenate([zt, cs_t, cs_t, zb], axis=0).T
    sn = jnp.concatenate([zt, -sn_t, sn_t, zb], axis=0).T
    ss_kr = jnp.sum(kr * kr, axis=-1, keepdims=True)
    krg = kr * gkbr_ref[...]
    kr_rot = krg * cs + (pltpu.roll(krg, LANES - half, 1) + pltpu.roll(krg, half, 1)) * sn
    for h in range(N_HEADS_B):
        sl = slice(h * LANES, (h + 1) * LANES)
        kh = kn[:, sl]
        rk = lax.rsqrt((jnp.sum(kh * kh, axis=-1, keepdims=True) + ss_kr) / QK_HEAD_DIM_B + EPS)
        kb_ref[:, sl] = ((kh * gkbn_ref[...] + kr_rot) * rk).astype(BF16)

    for h in range(N_HEADS_B):
        qh = qt[h * LANES:(h + 1) * LANES]
        rq = lax.rsqrt(jnp.sum(qh * qh, axis=0, keepdims=True) / QK_HEAD_DIM_B + EPS)
        y = qh * rq * gqbt_ref[...]
        y1, y2 = y[r0:r1], y[r1:r2]
        out = jnp.concatenate([y[:r0], y1 * cs_t - y2 * sn_t, y2 * cs_t + y1 * sn_t, y[r2:]], axis=0)
        qt_ref[h * LANES:(h + 1) * LANES, :] = (out * SCALE_B).astype(BF16)


def _proj_b_call(x2, pos_row, consts):
    n = x2.shape[0]
    tm = TM_PROJ
    row = lambda i: (i, 0)
    blk3 = lambda i: (i, 0, 0)
    wide = N_HEADS_B * LANES
    vt_rows = N_HEADS_B * VT_ROWS
    return pl.pallas_call(
        _proj_b_kernel,
        grid=(n // tm,),
        in_specs=[pl.BlockSpec((tm, D_MODEL), row),
                  pl.BlockSpec((None, 1, tm), blk3)] + [_const_spec(c.shape) for c in consts],
        out_specs=[pl.BlockSpec((None, wide, tm), blk3), pl.BlockSpec((tm, wide), row),
                   pl.BlockSpec((None, vt_rows, tm), blk3)],
        out_shape=[jax.ShapeDtypeStruct((n // tm, wide, tm), BF16),
                   jax.ShapeDtypeStruct((n, wide), BF16),
                   jax.ShapeDtypeStruct((n // tm, vt_rows, tm), BF16)],
        compiler_params=pltpu.CompilerParams(
            dimension_semantics=("parallel",), vmem_limit_bytes=VMEM_LIMIT),
        name="proj_b",
    )(x2, pos_row, *consts)


def _attn_a_kernel(q_ref, k_ref, v_ref, bias_ref, o_ref,
                   qf, kf, vf, acc, den, m0, m1, onat, *, seq, pad):
    lane = lax.broadcasted_iota(jnp.int32, (1, LANES), 1)
    first_head = lane < HALF
    qf[...] = q_ref[...].astype(F32)
    zeros = jnp.zeros((pad, LANES), F32)
    kf[pl.ds(0, pad), :] = zeros
    vf[pl.ds(0, pad), :] = zeros
    kf[pl.ds(pad, seq), :] = k_ref[...].astype(F32)
    vf[pl.ds(pad, seq), :] = v_ref[...].astype(F32)

    cls_rows = seq // CLASSES

    for bi, (window, dil) in enumerate(DILATED_BRANCHES):
        assert window // dil == WIN
        nchunk = max(CLASSES // dil, 1)
        clen = WIN // nchunk
        stride = max(dil // CLASSES, 1)
        step = stride * clen
        nblk = seq // (dil * WIN)
        ublk = min(GROUP_A, nblk)
        ncls = GROUP_A // ublk

        def run(start, clen=clen, stride=stride):
            return pl.ds(start, clen, stride=stride) if stride > 1 else pl.ds(start, clen)

        def gather(ref, starts, run=run):
            return jnp.concatenate([ref[run(st), :] for st in starts], axis=0)

        def group(r0, g, bi=bi, dil=dil, nchunk=nchunk, clen=clen, stride=stride, step=step,
                  ublk=ublk, ncls=ncls, run=run, gather=gather):
            plans = []
            for cc in range(ncls):
                rd = r0 + cc
                if stride == 1:
                    bases = [pl.multiple_of((dil * c + rd) * cls_rows + step * ublk * g, 8)
                             for c in range(nchunk)]
                else:
                    bases = [(rd % CLASSES) * cls_rows + rd // CLASSES + step * ublk * g]
                for u in range(ublk):
                    plans.append((bases, u))

            def scores(bases, u):
                qstarts = [b0 + step * u for b0 in bases]
                kstarts = [pad + b0 + step * (u - 1) for b0 in bases]
                q = gather(qf, qstarts)
                qq = jnp.concatenate([jnp.where(first_head, q, 0.0),
                                      jnp.where(first_head, 0.0, q)], axis=0).astype(BF16)
                kk = jnp.concatenate([gather(kf, kstarts),
                                      gather(kf, [k0 + step for k0 in kstarts])],
                                     axis=0).astype(BF16)
                vv = jnp.concatenate([gather(vf, kstarts),
                                      gather(vf, [k0 + step for k0 in kstarts])],
                                     axis=0).astype(BF16)
                first_blk = jnp.where(g == 0, 1, 0) if u == 0 else 0
                s = _nt_dot(qq, kk) + bias_ref[bi, first_blk]
                return qstarts, s, jnp.concatenate([vv, jnp.ones_like(vv)], axis=1)

            def update(qstarts, s, vext):
                ps, alphas = [], []
                for hh, m_s in enumerate((m0, m1)):
                    sh = s[hh * WIN:(hh + 1) * WIN]
                    m_blk = jnp.max(sh, axis=-1, keepdims=True)
                    if bi == 0:
                        m_new = jnp.broadcast_to(m_blk, (WIN, LANES))
                    else:
                        m_old = gather(m_s, qstarts)
                        m_new = jnp.maximum(m_old, m_blk)
                        alphas.append(jnp.exp2(m_old - m_new))
                    ps.append(jnp.exp2(sh - jnp.concatenate([m_new, m_new], axis=1)).astype(BF16))
                    for c, st in enumerate(qstarts):
                        m_s[run(st), :] = m_new[c * clen:(c + 1) * clen]
                res = _dot(jnp.concatenate(ps, axis=0), vext)
                acc_new = jnp.where(first_head, res[:WIN, :LANES], res[WIN:, :LANES])
                den_new = jnp.where(first_head, res[:WIN, LANES:], res[WIN:, LANES:])
                if bi > 0:
                    alpha = jnp.where(first_head, alphas[0], alphas[1])
                    acc_new = alpha * gather(acc, qstarts) + acc_new
                    den_new = alpha * gather(den, qstarts) + den_new
                for c, st in enumerate(qstarts):
                    acc[run(st), :] = acc_new[c * clen:(c + 1) * clen]
                    den[run(st), :] = den_new[c * clen:(c + 1) * clen]

            pending = scores(*plans[0])
            for nxt in plans[1:]:
                ahead = scores(*nxt)
                update(*pending)
                pending = ahead
            update(*pending)

        def class_body(ci, carry, ngrp=nblk // ublk, ncls=ncls, group=group):
            def grp_body(g, c):
                group(ci * ncls, g)
                return c
            return lax.fori_loop(0, ngrp, grp_body, carry)

        lax.fori_loop(0, dil // ncls, class_body, 0)

    for r in range(CLASSES):
        rows = pl.ds(r * cls_rows, cls_rows)
        onat[pl.ds(r, cls_rows, stride=CLASSES), :] = acc[rows, :] / den[rows, :]
    o_ref[...] = onat[...].astype(o_ref.dtype)


def _attn_a_call(qa, ka, va, bias):
    b, s, _ = qa.shape
    pad = WIN * DILATED_BRANCHES[-1][1] // CLASSES
    blk = pl.BlockSpec((None, s, LANES), lambda bb, p: (bb, 0, p))
    bias_spec = pl.BlockSpec((None,) + bias.shape[1:], lambda bb, p: (p, 0, 0, 0, 0))
    rows = pltpu.VMEM((s, LANES), F32)
    padded = pltpu.VMEM((pad + s, LANES), F32)
    return pl.pallas_call(
        functools.partial(_attn_a_kernel, seq=s, pad=pad),
        grid=(b, HEAD_PAIRS),
        in_specs=[blk, blk, blk, bias_spec],
        out_specs=blk,
        out_shape=jax.ShapeDtypeStruct((b, s, WIDTH_A), BF16),
        scratch_shapes=[rows, padded, padded, rows, rows, rows, rows, rows],
        compiler_params=pltpu.CompilerParams(
            dimension_semantics=("parallel", "parallel"), vmem_limit_bytes=VMEM_LIMIT),
        name="attn_a",
    )(qa, ka, va, bias)


def _t5_causal_bucket(dist):
    dist = np.asarray(dist, dtype=np.int64)
    max_exact = REL_BUCKETS // 2
    safe = np.maximum(dist, 1).astype(np.float32)
    large = max_exact + (np.log(safe / max_exact) / math.log(REL_MAX_DIST / max_exact)
                         * (REL_BUCKETS - max_exact)).astype(np.int64)
    large = np.minimum(large, REL_BUCKETS - 1)
    return np.where(dist < max_exact, dist, large).astype(np.int32)


def _bias_bucket_tables():
    i = np.arange(WIN)[:, None]
    c = np.arange(2 * WIN)[None, :]
    sub = WIN + i - c
    tables = []
    for _, dil in DILATED_BRANCHES:
        bucket = _t5_causal_bucket(np.clip(sub, 0, WIN) * dil)
        bucket = np.where((sub >= 0) & (sub <= WIN), bucket, REL_BUCKETS)
        nchunk = max(CLASSES // dil, 1)
        clen = WIN // nchunk
        bucket = bucket.reshape(clen, nchunk, 2, clen, nchunk)
        tables.append(np.transpose(bucket, (1, 0, 2, 4, 3)).reshape(WIN, 2 * WIN))
    return np.stack(tables).astype(np.int32)


def _bias_kernel(bias_ref, bucket_ref, o_ref):
    bucket = bucket_ref[...]
    first_blk_cols = lax.broadcasted_iota(jnp.int32, bucket.shape, 1) >= WIN
    for h in range(N_HEADS_A):
        t = jnp.full(bucket.shape, NEG, F32)
        for bkt in range(REL_BUCKETS):
            t = jnp.where(bucket == bkt, bias_ref[bkt, h], t)
        rows = slice((h % 2) * WIN, (h % 2 + 1) * WIN)
        o_ref[h // 2, 0, rows, :] = t
        o_ref[h // 2, 1, rows, :] = jnp.where(first_blk_cols, t, NEG)


def _attn_a_bias(rel_bias):
    buckets = jnp.asarray(_bias_bucket_tables())
    nbr = len(DILATED_BRANCHES)
    return pl.pallas_call(
        _bias_kernel,
        grid=(nbr,),
        in_specs=[pl.BlockSpec(memory_space=pltpu.SMEM),
                  pl.BlockSpec((None, WIN, 2 * WIN), lambda i: (i, 0, 0))],
        out_specs=pl.BlockSpec((HEAD_PAIRS, None, 2, 2 * WIN, 2 * WIN), lambda i: (0, i, 0, 0, 0)),
        out_shape=jax.ShapeDtypeStruct((HEAD_PAIRS, nbr, 2, 2 * WIN, 2 * WIN), F32),
        compiler_params=pltpu.CompilerParams(dimension_semantics=("parallel",)),
        name="bias_a",
    )(rel_bias.astype(F32) * LOG2E, buckets)


def _attn_b_kernel(qt_ref, k_ref, vt_ref, o_ref, acc0, acc1, m0, m1, s_even, s_odd, mb_even, mb_odd):
    tq = TQ_B
    nq = qt_ref.shape[0]
    heads = ((acc0, m0), (acc1, m1))
    bufs = ((s_even, mb_even), (s_odd, mb_odd))

    hq = tq // 2

    def scores(qi, kb, s_buf, mb_buf):
        krows = pl.ds(pl.multiple_of(kb * tq, tq), tq)
        for hh in range(2):
            st = _dot(k_ref[krows, hh * LANES:(hh + 1) * LANES], qt_ref[qi, hh])
            s_buf[hh] = st
            mb_buf[hh] = jnp.max(st, axis=0, keepdims=True)

    def update(kb, s_buf, mb_buf):
        for hh, (acc, m_s) in enumerate(heads):
            m_old = m_s[...]
            m_new = jnp.maximum(m_old, mb_buf[hh])
            alpha = jnp.exp2(m_old - m_new)
            pt = jnp.exp2(s_buf[hh] - m_new).astype(BF16)
            acc[...] = alpha * acc[...] + _dot(vt_ref[kb, hh], pt)
            m_s[...] = m_new

    def scores_and_update(qi, kb_ahead, ahead, kb, cur):
        (s_cur, mb_cur), (s_ahead, mb_ahead) = cur, ahead
        krows = pl.ds(pl.multiple_of(kb_ahead * tq, tq), tq)
        for hh, (acc, m_s) in enumerate(heads):
            m_old = m_s[...]
            m_new = jnp.maximum(m_old, mb_cur[hh])
            alpha = jnp.exp2(m_old - m_new)
            k_blk = k_ref[krows, hh * LANES:(hh + 1) * LANES]
            pts = []
            for part in range(2):
                half = slice(part * hq, (part + 1) * hq)
                pts.append(jnp.exp2(s_cur[hh, half, :] - m_new).astype(BF16))
                st = _dot(k_blk, qt_ref[qi, hh, :, half])
                s_ahead[hh, :, half] = st
                mb_ahead[hh, :, half] = jnp.max(st, axis=0, keepdims=True)
            acc[...] = alpha * acc[...] + _dot(vt_ref[kb, hh], jnp.concatenate(pts, axis=0))
            m_s[...] = m_new


    def scores_diag(qi, s_buf):
        k0 = qi * tq
        for hh in range(2):
            sl = slice(hh * LANES, (hh + 1) * LANES)
            s_buf[hh, :hq, :] = _dot(k_ref[pl.ds(k0, hq), sl], qt_ref[qi, hh])
            s_buf[hh, hq:, :hq] = _dot(k_ref[pl.ds(k0 + hq, hq), sl],
                                       qt_ref[qi, hh, :, hq:])

    def update_diag(qi, s_buf):
        def causal(shape):
            return (lax.broadcasted_iota(jnp.int32, shape, 0)
                    <= lax.broadcasted_iota(jnp.int32, shape, 1))

        for hh, (acc, m_s) in enumerate(heads):
            top = jnp.where(causal((hq, tq)), s_buf[hh, :hq, :], NEG)
            bot = jnp.where(causal((hq, hq)), s_buf[hh, hq:, :hq], NEG)
            m_top = jnp.max(top, axis=0, keepdims=True)
            m_bot = jnp.max(jnp.concatenate([jnp.full((hq, hq), NEG, F32), bot], axis=1),
                            axis=0, keepdims=True)
            m_old = m_s[...]
            m_new = jnp.maximum(m_old, jnp.maximum(m_top, m_bot))
            alpha = jnp.exp2(m_old - m_new)
            m_s[...] = m_new
            pt_top = jnp.exp2(top - m_new).astype(BF16)
            pt_bot = jnp.exp2(bot - m_s[:, hq:]).astype(BF16)
            pv = _dot(vt_ref[qi, hh, :, :hq], pt_top)
            pv_bot = _dot(vt_ref[qi, hh, :, hq:], pt_bot)
            acc[...] = alpha * acc[...] + pv + jnp.concatenate(
                [jnp.zeros((VT_ROWS, hq), F32), pv_bot], axis=1)

    scores_diag(0, bufs[0][0])
    step0 = 0
    for qi in range(nq):
        cur, nxt = bufs[step0 % 2], bufs[(step0 + 1) % 2]
        for acc, m_s in heads:
            acc[...] = jnp.zeros_like(acc)
            m_s[...] = jnp.full(m_s.shape, NEG, F32)

        def pair_body(i, c, qi=qi, cur=cur, nxt=nxt):
            kb = 2 * i
            scores_and_update(qi, kb + 1, nxt, kb, cur)
            scores_and_update(qi, kb + 2, cur, kb + 1, nxt)
            return c

        npairs = qi // 2
        peel = qi % 2 == 0 and npairs > 0
        lax.fori_loop(0, npairs - 1 if peel else npairs, pair_body, 0)
        if peel:
            scores_and_update(qi, qi - 1, nxt, qi - 2, cur)
            scores_diag(qi, cur[0])
            update(qi - 1, *nxt)
        last = cur
        if qi % 2 == 1:
            scores_diag(qi, nxt[0])
            update(qi - 1, *cur)
            last = nxt
        if qi + 1 < nq:
            scores(qi + 1, 0, *(cur if last is nxt else nxt))
        update_diag(qi, last[0])

        outs = []
        for acc, _ in heads:
            a = acc[...]
            outs.append(a[:V_HEAD_DIM] / a[V_HEAD_DIM:V_HEAD_DIM + 1])
        o_ref[pl.ds(qi * tq, tq), :] = jnp.concatenate(outs, axis=0).T.astype(o_ref.dtype)
        step0 += qi + 1


def _attn_b_call(qbt, kb, vbt):
    b, s, _ = kb.shape
    tq = TQ_B
    nkb = s // tq
    qbt = qbt.reshape(b, nkb, HEAD_PAIRS, 2, LANES, tq)
    vbt = vbt.reshape(b, nkb, HEAD_PAIRS, 2, VT_ROWS, tq)
    return pl.pallas_call(
        _attn_b_kernel,
        grid=(b, HEAD_PAIRS),
        in_specs=[pl.BlockSpec((None, nkb, None, 2, LANES, tq), lambda bb, p: (bb, 0, p, 0, 0, 0)),
                  pl.BlockSpec((None, s, 2 * LANES), lambda bb, p: (bb, 0, p)),
                  pl.BlockSpec((None, nkb, None, 2, VT_ROWS, tq),
                               lambda bb, p: (bb, 0, p, 0, 0, 0))],
        out_specs=pl.BlockSpec((None, s, LANES), lambda bb, p: (bb, 0, p)),
        out_shape=jax.ShapeDtypeStruct((b, s, WIDTH_B), BF16),
        scratch_shapes=[pltpu.VMEM((VT_ROWS, tq), F32)] * 2 + [pltpu.VMEM((1, tq), F32)] * 2
        + [pltpu.VMEM((2, tq, tq), F32)] * 2 + [pltpu.VMEM((2, 1, tq), F32)] * 2,
        compiler_params=pltpu.CompilerParams(
            dimension_semantics=("parallel", "parallel"), vmem_limit_bytes=VMEM_LIMIT),
        name="attn_b",
    )(qbt, kb, vbt)


def _ffn_kernel(x_ref, a_ref, b_ref, woa_ref, wob_ref, g_ref, w1_ref, w2_ref, o_ref):
    h = x_ref[...] + _dot(a_ref[...], woa_ref[...]) + _dot(b_ref[...], wob_ref[...])
    hn = (h * lax.rsqrt(jnp.mean(h * h, axis=-1, keepdims=True) + EPS) * g_ref[...]).astype(BF16)
    mlp = None
    for c in range(D_FF // FF_CHUNK):
        sl = slice(c * FF_CHUNK, (c + 1) * FF_CHUNK)
        hid = jnp.square(jnp.maximum(_dot(hn, w1_ref[:, sl]), 0.0)).astype(BF16)
        d = _dot(hid, w2_ref[sl, :])
        mlp = d if mlp is None else mlp + d
    o_ref[...] = h + mlp


def _ffn_call(x2, a2, b2, woa, wob, g, w1, w2):
    n = x2.shape[0]
    tm = TM_FFN
    row = lambda i: (i, 0)
    consts = (woa, wob, g, w1, w2)
    return pl.pallas_call(
        _ffn_kernel,
        grid=(n // tm,),
        in_specs=[pl.BlockSpec((tm, D_MODEL), row), pl.BlockSpec((tm, WIDTH_A), row),
                  pl.BlockSpec((tm, WIDTH_B), row)] + [_const_spec(c.shape) for c in consts],
        out_specs=pl.BlockSpec((tm, D_MODEL), row),
        out_shape=jax.ShapeDtypeStruct((n, D_MODEL), F32),
        compiler_params=pltpu.CompilerParams(
            dimension_semantics=("parallel",), vmem_limit_bytes=VMEM_LIMIT),
        name="ffn",
    )(x2, a2, b2, *consts)


def _pad_heads(w, n_heads, width):
    k = w.shape[0]
    w = w.reshape(k, n_heads, width)
    return jnp.pad(w, ((0, 0), (0, 0), (0, LANES - width))).reshape(k, n_heads * LANES)


def _lane_row(g, offset=0):
    return jnp.pad(g.astype(F32), (offset, LANES - offset - g.shape[0]))[None, :]


def kernel(x, positions, norm_mix_g, w_in, qnorm_a_g, knorm_a_g, rel_bias, cq_norm_g, ckv_norm_g,
           w_uq, w_ukv, qnorm_b_g, knorm_b_g, w_o, norm_ffn_g, w_ff1, w_ff2):
    b, s, d = x.shape
    n = b * s
    c0 = 3 * WIDTH_A
    c1 = c0 + Q_LORA_RANK
    c2 = c1 + KV_LORA_RANK
    w_qkv = w_in[:, :c0].astype(BF16)
    w_cq = w_in[:, c0:c1].astype(BF16)
    w_ckv = w_in[:, c1:c2].astype(BF16)
    w_kr = jnp.pad(w_in[:, c2:], ((0, 0), (QK_NOPE_DIM, LANES - QK_HEAD_DIM_B))).astype(BF16)
    w_uqt = _pad_heads(w_uq, N_HEADS_B, QK_HEAD_DIM_B).T.astype(BF16)
    w_ukv3 = w_ukv.reshape(KV_LORA_RANK, N_HEADS_B, QK_NOPE_DIM + V_HEAD_DIM)
    w_uk_p = _pad_heads(w_ukv3[:, :, :QK_NOPE_DIM].reshape(KV_LORA_RANK, -1),
                        N_HEADS_B, QK_NOPE_DIM).astype(BF16)
    ones_rows = VT_ROWS - V_HEAD_DIM
    w_uvt = jnp.transpose(w_ukv3[:, :, QK_NOPE_DIM:], (1, 2, 0))
    w_uvt = jnp.pad(w_uvt, ((0, 0), (0, ones_rows), (0, 0)))
    w_uvt = w_uvt.reshape(N_HEADS_B * VT_ROWS, KV_LORA_RANK).astype(BF16)
    inv_freq = 1.0 / (ROPE_THETA ** (jnp.arange(0, QK_ROPE_DIM, 2, dtype=F32) / QK_ROPE_DIM))
    ones_col = jnp.tile(jnp.concatenate([jnp.zeros((V_HEAD_DIM,), F32), jnp.ones((ones_rows,), F32)]),
                        N_HEADS_B)[:, None]
    g_mix = norm_mix_g.astype(F32)[None, :]
    head_sum = jnp.asarray(np.kron(np.eye(N_HEADS_A), np.ones((HEAD_DIM_A, HEAD_DIM_A))), BF16)
    consts_a = (
        g_mix, w_qkv, head_sum,
        jnp.tile(qnorm_a_g.astype(F32), N_HEADS_A)[None, :],
        jnp.tile(knorm_a_g.astype(F32), N_HEADS_A)[None, :],
    )
    tm = TM_PROJ
    gq_col = jnp.pad(qnorm_b_g.astype(F32), (0, LANES - QK_HEAD_DIM_B))[:, None]
    consts_b = (
        g_mix, w_cq, w_ckv, w_kr,
        cq_norm_g.astype(F32)[None, :], ckv_norm_g.astype(F32)[None, :],
        w_uqt, w_uk_p, w_uvt,
        jnp.broadcast_to(gq_col, (LANES, tm)), _lane_row(knorm_b_g[:QK_NOPE_DIM]),
        _lane_row(knorm_b_g[QK_NOPE_DIM:], QK_NOPE_DIM),
        jnp.broadcast_to(inv_freq[:, None], (QK_ROPE_DIM // 2, tm)),
        ones_col,
    )
    x2 = x.reshape(n, d)
    qa, ka, va = _proj_a_call(x, consts_a)
    qbt, kb, vbt = _proj_b_call(x2, positions.reshape(n // tm, 1, tm), consts_b)

    out_a = _attn_a_call(qa, ka, va, _attn_a_bias(rel_bias))
    out_b = _attn_b_call(qbt, kb.reshape(b, s, -1), vbt)

    out = _ffn_call(x.reshape(n, d), out_a.reshape(n, -1), out_b.reshape(n, -1),
                    w_o[:WIDTH_A].astype(BF16), w_o[WIDTH_A:].astype(BF16),
                    norm_ffn_g.astype(F32)[None, :], w_ff1.astype(BF16), w_ff2.astype(BF16))
    return out.reshape(b, s, d)
```

```python
import functools
import math

import numpy as np
import jax
import jax.numpy as jnp
from jax import lax
from jax.experimental import pallas as pl
from jax.experimental.pallas import tpu as pltpu

F32 = jnp.float32
BF16 = jnp.bfloat16

D_MODEL = 1024
HEAD_DIM_A = 64
N_HEADS_A = 8
WIDTH_A = N_HEADS_A * HEAD_DIM_A
DILATED_BRANCHES = ((128, 1), (512, 4), (2048, 16))
N_HEADS_B = 8
QK_NOPE_DIM = 64
QK_ROPE_DIM = 32
V_HEAD_DIM = 64
QK_HEAD_DIM_B = QK_NOPE_DIM + QK_ROPE_DIM
Q_LORA_RANK = 768
KV_LORA_RANK = 256
WIDTH_B = N_HEADS_B * V_HEAD_DIM
ROPE_THETA = 10000.0
D_FF = 4 * D_MODEL
REL_BUCKETS = 32
REL_MAX_DIST = 2048
EPS = 1e-6

LANES = 128
HALF = LANES // 2
HEAD_PAIRS = N_HEADS_A // 2
WIN = 128
NEG = -1e30
VMEM_LIMIT = 56 * 1024 * 1024

TM_PROJ = 512
TM_FFN = 512
TQ_B = 512
FF_CHUNK = 1024
GROUP_A = 32
CLASSES = 4

LOG2E = math.log2(math.e)
SCALE_A = LOG2E / math.sqrt(HEAD_DIM_A)
SCALE_B = LOG2E / math.sqrt(QK_HEAD_DIM_B)

BF16_ROWS = 16
VT_ROWS = V_HEAD_DIM + BF16_ROWS

assert HEAD_DIM_A == HALF and V_HEAD_DIM == HALF
assert TM_PROJ == TQ_B


def _nt_dot(a, b):
    return lax.dot_general(a, b, (((1,), (1,)), ((), ())), preferred_element_type=F32)


def _dot(a, b):
    return jnp.dot(a, b, preferred_element_type=F32)


def _const_spec(shape):
    nd = len(shape)
    return pl.BlockSpec(shape, lambda *_: (0,) * nd, pipeline_mode=pl.Buffered(1))


def _rms_bf16(x, g_row):
    return (x * lax.rsqrt(jnp.mean(x * x, axis=-1, keepdims=True) + EPS) * g_row).astype(BF16)


def _proj_a_kernel(x_ref, gmix_ref, wqkv_ref, hsum_ref, gqa_ref, gka_ref, qa_ref, ka_ref, va_ref,
                   xs_ref):
    tm, d = x_ref.shape
    run = tm // CLASSES
    for c in range(d // LANES):
        xs_ref[c] = x_ref[:, c * LANES:(c + 1) * LANES]

    def head_sumsq(y):
        return _dot((y * y).astype(BF16), hsum_ref[...])

    sub = run // 2
    for h in range(2):
        x = jnp.concatenate(
            [jnp.concatenate([xs_ref[c, pl.ds(r + h * sub * CLASSES, sub, stride=CLASSES), :]
                              for r in range(CLASSES)], axis=0)
             for c in range(d // LANES)], axis=1)
        xn = _rms_bf16(x, gmix_ref[...])
        q = _dot(xn, wqkv_ref[:, :WIDTH_A])
        k = _dot(xn, wqkv_ref[:, WIDTH_A:2 * WIDTH_A])
        ssq = head_sumsq(q)
        v = _dot(xn, wqkv_ref[:, 2 * WIDTH_A:]).astype(BF16)
        ssk = head_sumsq(k)
        q = (q * lax.rsqrt(ssq / HEAD_DIM_A + EPS) * gqa_ref[...] * SCALE_A).astype(BF16)
        k = (k * lax.rsqrt(ssk / HEAD_DIM_A + EPS) * gka_ref[...]).astype(BF16)
        for r in range(CLASSES):
            src = slice(r * sub, (r + 1) * sub)
            dst = slice(h * sub, (h + 1) * sub)
            qa_ref[r, dst] = q[src]
            ka_ref[r, dst] = k[src]
            va_ref[r, dst] = v[src]


def _proj_a_call(x, consts):
    b, s, d = x.shape
    tm = TM_PROJ
    run = tm // CLASSES
    out = jax.ShapeDtypeStruct((b, CLASSES, s // CLASSES, WIDTH_A), BF16)
    outs = pl.pallas_call(
        _proj_a_kernel,
        grid=(b, s // tm),
        in_specs=[pl.BlockSpec((None, tm, d), lambda bb, i: (bb, i, 0))]
        + [_const_spec(c.shape) for c in consts],
        out_specs=[pl.BlockSpec((None, CLASSES, run, WIDTH_A), lambda bb, i: (bb, 0, i, 0))] * 3,
        out_shape=[out] * 3,
        scratch_shapes=[pltpu.VMEM((d // LANES, tm, LANES), F32)],
        compiler_params=pltpu.CompilerParams(
            dimension_semantics=("parallel", "parallel"), vmem_limit_bytes=VMEM_LIMIT),
        name="proj_a",
    )(x, *consts)
    return [o.reshape(b, s, WIDTH_A) for o in outs]


def _proj_b_kernel(x_ref, posr_ref, gmix_ref, wcq_ref, wckv_ref, wkr_ref,
                   gcq_ref, gckv_ref, wuqt_ref, wuk_ref, wuvt_ref,
                   gqbt_ref, gkbn_ref, gkbr_ref, invft_ref, ones_ref,
                   qt_ref, kb_ref, vt_ref):
    tm = x_ref.shape[0]
    half = QK_ROPE_DIM // 2
    r0, r1, r2 = QK_NOPE_DIM, QK_NOPE_DIM + half, QK_HEAD_DIM_B
    xn = _rms_bf16(x_ref[...], gmix_ref[...])
    cq = _dot(xn, wcq_ref[...])
    ckv = _dot(xn, wckv_ref[...])
    kr = _dot(xn, wkr_ref[...])
    ang_t = invft_ref[...] * posr_ref[...].astype(F32)
    cs_t = jnp.cos(ang_t)
    sn_t = jnp.sin(ang_t)

    ckvn = _rms_bf16(ckv, gckv_ref[...])
    kn = _dot(ckvn, wuk_ref[...])
    vt_ref[...] = (_nt_dot(wuvt_ref[...], ckvn) + ones_ref[...]).astype(BF16)

    cqn = _rms_bf16(cq, gcq_ref[...])
    qt = _nt_dot(wuqt_ref[...], cqn)

    zt = jnp.zeros((r0, tm), F32)
    zb = jnp.zeros((LANES - r2, tm), F32)
    cs = jnp.concatenate([zt, cs_t, cs_t, zb], axis=0).T
    sn = jnp.concatenate([zt, -sn_t, sn_t, zb], axis=0).T
    ss_kr = jnp.sum(kr * kr, axis=-1, keepdims=True)
    krg = kr * gkbr_ref[...]
    kr_rot = krg * cs + (pltpu.roll(krg, LANES - half, 1) + pltpu.roll(krg, half, 1)) * sn
    for h in range(N_HEADS_B):
        sl = slice(h * LANES, (h + 1) * LANES)
        kh = kn[:, sl]
        rk = lax.rsqrt((jnp.sum(kh * kh, axis=-1, keepdims=True) + ss_kr) / QK_HEAD_DIM_B + EPS)
        kb_ref[:, sl] = ((kh * gkbn_ref[...] + kr_rot) * rk).astype(BF16)

    for h in range(N_HEADS_B):
        qh = qt[h * LANES:(h + 1) * LANES]
        rq = lax.rsqrt(jnp.sum(qh * qh, axis=0, keepdims=True) / QK_HEAD_DIM_B + EPS)
        y = qh * rq * gqbt_ref[...]
        y1, y2 = y[r0:r1], y[r1:r2]
        out = jnp.concatenate([y[:r0], y1 * cs_t - y2 * sn_t, y2 * cs_t + y1 * sn_t, y[r2:]], axis=0)
        qt_ref[h * LANES:(h + 1) * LANES, :] = (out * SCALE_B).astype(BF16)


def _proj_b_call(x2, pos_row, consts):
    n = x2.shape[0]
    tm = TM_PROJ
    row = lambda i: (i, 0)
    blk3 = lambda i: (i, 0, 0)
    wide = N_HEADS_B * LANES
    vt_rows = N_HEADS_B * VT_ROWS
    return pl.pallas_call(
        _proj_b_kernel,
        grid=(n // tm,),
        in_specs=[pl.BlockSpec((tm, D_MODEL), row),
                  pl.BlockSpec((None, 1, tm), blk3)] + [_const_spec(c.shape) for c in consts],
        out_specs=[pl.BlockSpec((None, wide, tm), blk3), pl.BlockSpec((tm, wide), row),
                   pl.BlockSpec((None, vt_rows, tm), blk3)],
        out_shape=[jax.ShapeDtypeStruct((n // tm, wide, tm), BF16),
                   jax.ShapeDtypeStruct((n, wide), BF16),
                   jax.ShapeDtypeStruct((n // tm, vt_rows, tm), BF16)],
        compiler_params=pltpu.CompilerParams(
            dimension_semantics=("parallel",), vmem_limit_bytes=VMEM_LIMIT),
        name="proj_b",
    )(x2, pos_row, *consts)


def _attn_a_kernel(q_ref, k_ref, v_ref, bias_ref, o_ref,
                   qf, kf, vf, acc, den, m0, m1, onat, *, seq, pad):
    lane = lax.broadcasted_iota(jnp.int32, (1, LANES), 1)
    first_head = lane < HALF
    qf[...] = q_ref[...].astype(F32)
    zeros = jnp.zeros((pad, LANES), F32)
    kf[pl.ds(0, pad), :] = zeros
    vf[pl.ds(0, pad), :] = zeros
    kf[pl.ds(pad, seq), :] = k_ref[...].astype(F32)
    vf[pl.ds(pad, seq), :] = v_ref[...].astype(F32)

    cls_rows = seq // CLASSES

    for bi, (window, dil) in enumerate(DILATED_BRANCHES):
        assert window // dil == WIN
        nchunk = max(CLASSES // dil, 1)
        clen = WIN // nchunk
        stride = max(dil // CLASSES, 1)
        step = stride * clen
        nblk = seq // (dil * WIN)
        ublk = min(GROUP_A, nblk)
        ncls = GROUP_A // ublk

        def run(start, clen=clen, stride=stride):
            return pl.ds(start, clen, stride=stride) if stride > 1 else pl.ds(start, clen)

        def gather(ref, starts, run=run):
            return jnp.concatenate([ref[run(st), :] for st in starts], axis=0)

        def group(r0, g, bi=bi, dil=dil, nchunk=nchunk, clen=clen, stride=stride, step=step,
                  ublk=ublk, ncls=ncls, run=run, gather=gather):
            plans = []
            for cc in range(ncls):
                rd = r0 + cc
                if stride == 1:
                    bases = [pl.multiple_of((dil * c + rd) * cls_rows + step * ublk * g, 8)
                             for c in range(nchunk)]
                else:
                    bases = [(rd % CLASSES) * cls_rows + rd // CLASSES + step * ublk * g]
                for u in range(ublk):
                    plans.append((bases, u))

            def scores(bases, u):
                qstarts = [b0 + step * u for b0 in bases]
                kstarts = [pad + b0 + step * (u - 1) for b0 in bases]
                q = gather(qf, qstarts)
                qq = jnp.concatenate([jnp.where(first_head, q, 0.0),
                                      jnp.where(first_head, 0.0, q)], axis=0).astype(BF16)
                kk = jnp.concatenate([gather(kf, kstarts),
                                      gather(kf, [k0 + step for k0 in kstarts])],
                                     axis=0).astype(BF16)
                vv = jnp.concatenate([gather(vf, kstarts),
                                      gather(vf, [k0 + step for k0 in kstarts])],
                                     axis=0).astype(BF16)
                first_blk = jnp.where(g == 0, 1, 0) if u == 0 else 0
                s = _nt_dot(qq, kk) + bias_ref[bi, first_blk]
                return qstarts, s, jnp.concatenate([vv, jnp.ones_like(vv)], axis=1)

            def update(qstarts, s, vext):
                ps, alphas = [], []
                for hh, m_s in enumerate((m0, m1)):
                    sh = s[hh * WIN:(hh + 1) * WIN]
                    m_blk = jnp.max(sh, axis=-1, keepdims=True)
                    if bi == 0:
                        m_new = jnp.broadcast_to(m_blk, (WIN, LANES))
                    else:
                        m_old = gather(m_s, qstarts)
                        m_new = jnp.maximum(m_old, m_blk)
                        alphas.append(jnp.exp2(m_old - m_new))
                    ps.append(jnp.exp2(sh - jnp.concatenate([m_new, m_new], axis=1)).astype(BF16))
                    for c, st in enumerate(qstarts):
                        m_s[run(st), :] = m_new[c * clen:(c + 1) * clen]
                res = _dot(jnp.concatenate(ps, axis=0), vext)
                acc_new = jnp.where(first_head, res[:WIN, :LANES], res[WIN:, :LANES])
                den_new = jnp.where(first_head, res[:WIN, LANES:], res[WIN:, LANES:])
                if bi > 0:
                    alpha = jnp.where(first_head, alphas[0], alphas[1])
                    acc_new = alpha * gather(acc, qstarts) + acc_new
                    den_new = alpha * gather(den, qstarts) + den_new
                for c, st in enumerate(qstarts):
                    acc[run(st), :] = acc_new[c * clen:(c + 1) * clen]
                    den[run(st), :] = den_new[c * clen:(c + 1) * clen]

            pending = scores(*plans[0])
            for nxt in plans[1:]:
                ahead = scores(*nxt)
                update(*pending)
                pending = ahead
            update(*pending)

        def class_body(ci, carry, ngrp=nblk // ublk, ncls=ncls, group=group):
            def grp_body(g, c):
                group(ci * ncls, g)
                return c
            return lax.fori_loop(0, ngrp, grp_body, carry)

        lax.fori_loop(0, dil // ncls, class_body, 0)

    for r in range(CLASSES):
        rows = pl.ds(r * cls_rows, cls_rows)
        onat[pl.ds(r, cls_rows, stride=CLASSES), :] = acc[rows, :] / den[rows, :]
    o_ref[...] = onat[...].astype(o_ref.dtype)


def _attn_a_call(qa, ka, va, bias):
    b, s, _ = qa.shape
    pad = WIN * DILATED_BRANCHES[-1][1] // CLASSES
    blk = pl.BlockSpec((None, s, LANES), lambda bb, p: (bb, 0, p))
    bias_spec = pl.BlockSpec((None,) + bias.shape[1:], lambda bb, p: (p, 0, 0, 0, 0))
    rows = pltpu.VMEM((s, LANES), F32)
    padded = pltpu.VMEM((pad + s, LANES), F32)
    return pl.pallas_call(
        functools.partial(_attn_a_kernel, seq=s, pad=pad),
        grid=(b, HEAD_PAIRS),
        in_specs=[blk, blk, blk, bias_spec],
        out_specs=blk,
        out_shape=jax.ShapeDtypeStruct((b, s, WIDTH_A), BF16),
        scratch_shapes=[rows, padded, padded, rows, rows, rows, rows, rows],
        compiler_params=pltpu.CompilerParams(
            dimension_semantics=("parallel", "parallel"), vmem_limit_bytes=VMEM_LIMIT),
        name="attn_a",
    )(qa, ka, va, bias)


def _t5_causal_bucket(dist):
    dist = np.asarray(dist, dtype=np.int64)
    max_exact = REL_BUCKETS // 2
    safe = np.maximum(dist, 1).astype(np.float32)
    large = max_exact + (np.log(safe / max_exact) / math.log(REL_MAX_DIST / max_exact)
                         * (REL_BUCKETS - max_exact)).astype(np.int64)
    large = np.minimum(large, REL_BUCKETS - 1)
    return np.where(dist < max_exact, dist, large).astype(np.int32)


def _bias_bucket_tables():
    i = np.arange(WIN)[:, None]
    c = np.arange(2 * WIN)[None, :]
    sub = WIN + i - c
    tables = []
    for _, dil in DILATED_BRANCHES:
        bucket = _t5_causal_bucket(np.clip(sub, 0, WIN) * dil)
        bucket = np.where((sub >= 0) & (sub <= WIN), bucket, REL_BUCKETS)
        nchunk = max(CLASSES // dil, 1)
        clen = WIN // nchunk
        bucket = bucket.reshape(clen, nchunk, 2, clen, nchunk)
        tables.append(np.transpose(bucket, (1, 0, 2, 4, 3)).reshape(WIN, 2 * WIN))
    return np.stack(tables).astype(np.int32)


def _bias_kernel(bias_ref, bucket_ref, o_ref):
    bucket = bucket_ref[...]
    first_blk_cols = lax.broadcasted_iota(jnp.int32, bucket.shape, 1) >= WIN
    for h in range(N_HEADS_A):
        t = jnp.full(bucket.shape, NEG, F32)
        for bkt in range(REL_BUCKETS):
            t = jnp.where(bucket == bkt, bias_ref[bkt, h], t)
        rows = slice((h % 2) * WIN, (h % 2 + 1) * WIN)
        o_ref[h // 2, 0, rows, :] = t
        o_ref[h // 2, 1, rows, :] = jnp.where(first_blk_cols, t, NEG)


def _attn_a_bias(rel_bias):
    buckets = jnp.asarray(_bias_bucket_tables())
    nbr = len(DILATED_BRANCHES)
    return pl.pallas_call(
        _bias_kernel,
        grid=(nbr,),
        in_specs=[pl.BlockSpec(memory_space=pltpu.SMEM),
                  pl.BlockSpec((None, WIN, 2 * WIN), lambda i: (i, 0, 0))],
        out_specs=pl.BlockSpec((HEAD_PAIRS, None, 2, 2 * WIN, 2 * WIN), lambda i: (0, i, 0, 0, 0)),
        out_shape=jax.ShapeDtypeStruct((HEAD_PAIRS, nbr, 2, 2 * WIN, 2 * WIN), F32),
        compiler_params=pltpu.CompilerParams(dimension_semantics=("parallel",)),
        name="bias_a",
    )(rel_bias.astype(F32) * LOG2E, buckets)


def _attn_b_kernel(qt_ref, k_ref, vt_ref, o_ref, acc0, acc1, m0, m1, s_even, s_odd, mb_even, mb_odd):
    tq = TQ_B
    nq = qt_ref.shape[0]
    heads = ((acc0, m0), (acc1, m1))
    bufs = ((s_even, mb_even), (s_odd, mb_odd))

    hq = tq // 2

    def causal(shape):
        return (lax.broadcasted_iota(jnp.int32, shape, 0)
                <= lax.broadcasted_iota(jnp.int32, shape, 1))

    def step(fold=None, make=None):
        for hh, (acc, m_s) in enumerate(heads):
            sl = slice(hh * LANES, (hh + 1) * LANES)
            if make is None:
                mats = [lambda: None, lambda: None]
            elif make[0] == "full":
                _, qi_a, kb_a, (s_a, mb_a) = make
                k_blk = k_ref[pl.ds(pl.multiple_of(kb_a * tq, tq), tq), sl]

                def mat(part, qi_a=qi_a, s_a=s_a, mb_a=mb_a, k_blk=k_blk):
                    half = slice(part * hq, (part + 1) * hq)
                    st = _dot(k_blk, qt_ref[qi_a, hh, :, half])
                    s_a[hh, :, half] = st
                    mb_a[hh, :, half] = jnp.max(st, axis=0, keepdims=True)
                mats = [functools.partial(mat, 0), functools.partial(mat, 1)]
            else:
                _, qi_a, s_a = make
                k0 = qi_a * tq

                def top_mat(qi_a=qi_a, s_a=s_a, k0=k0):
                    s_a[hh, :hq, :] = _dot(k_ref[pl.ds(k0, hq), sl], qt_ref[qi_a, hh])
                def bot_mat(qi_a=qi_a, s_a=s_a, k0=k0):
                    s_a[hh, hq:, :hq] = _dot(k_ref[pl.ds(k0 + hq, hq), sl], qt_ref[qi_a, hh, :, hq:])
                mats = [top_mat, bot_mat]
            if fold is None:
                for mat_piece in mats:
                    mat_piece()
                continue

            m_old = m_s[...]
            if fold[0] == "full":
                _, kb, (s_c, mb_c) = fold
                m_new = jnp.maximum(m_old, mb_c[hh])
                m_s[...] = m_new
                p0 = jnp.exp2(s_c[hh, :hq, :] - m_new).astype(BF16)
                mats[0]()
                p1 = jnp.exp2(s_c[hh, hq:, :] - m_new).astype(BF16)
                mats[1]()
                pv = _dot(vt_ref[kb, hh], jnp.concatenate([p0, p1], axis=0))
            else:
                _, kb, s_c = fold
                top = jnp.where(causal((hq, tq)), s_c[hh, :hq, :], NEG)
                bot = jnp.where(causal((hq, hq)), s_c[hh, hq:, :hq], NEG)
                m_top = jnp.max(top, axis=0, keepdims=True)
                m_bot = jnp.max(jnp.concatenate([jnp.full((hq, hq), NEG, F32), bot], axis=1),
                                axis=0, keepdims=True)
                m_new = jnp.maximum(m_old, jnp.maximum(m_top, m_bot))
                m_s[...] = m_new
                p0 = jnp.exp2(top - m_new).astype(BF16)
                mats[0]()
                p1 = jnp.exp2(bot - m_s[:, hq:]).astype(BF16)
                mats[1]()
                pv = _dot(vt_ref[kb, hh, :, :hq], p0) + jnp.concatenate(
                    [jnp.zeros((VT_ROWS, hq), F32), _dot(vt_ref[kb, hh, :, hq:], p1)], axis=1)
            acc[...] = jnp.exp2(m_old - m_new) * acc[...] + pv

    step(make=("diag", 0, bufs[0][0]))
    step0 = 0
    for qi in range(nq):
        cur, nxt = bufs[step0 % 2], bufs[(step0 + 1) % 2]
        for acc, m_s in heads:
            acc[...] = jnp.zeros_like(acc)
            m_s[...] = jnp.full(m_s.shape, NEG, F32)

        def pair_body(i, c, qi=qi, cur=cur, nxt=nxt):
            kb = 2 * i
            step(("full", kb, cur), ("full", qi, kb + 1, nxt))
            step(("full", kb + 1, nxt), ("full", qi, kb + 2, cur))
            return c

        npairs = qi // 2
        peel = qi % 2 == 0 and npairs > 0
        lax.fori_loop(0, npairs - 1 if peel else npairs, pair_body, 0)
        if peel:
            step(("full", qi - 2, cur), ("full", qi, qi - 1, nxt))
            step(("full", qi - 1, nxt), ("diag", qi, cur[0]))
        last, other = cur, nxt
        if qi % 2 == 1:
            step(("full", qi - 1, cur), ("diag", qi, nxt[0]))
            last, other = nxt, cur
        step(("diag", qi, last[0]), ("full", qi + 1, 0, other) if qi + 1 < nq else None)

        outs = []
        for acc, _ in heads:
            a = acc[...]
            outs.append(a[:V_HEAD_DIM] / a[V_HEAD_DIM:V_HEAD_DIM + 1])
        o_ref[pl.ds(qi * tq, tq), :] = jnp.concatenate(outs, axis=0).T.astype(o_ref.dtype)
        step0 += qi + 1


def _attn_b_call(qbt, kb, vbt):
    b, s, _ = kb.shape
    tq = TQ_B
    nkb = s // tq
    qbt = qbt.reshape(b, nkb, HEAD_PAIRS, 2, LANES, tq)
    vbt = vbt.reshape(b, nkb, HEAD_PAIRS, 2, VT_ROWS, tq)
    return pl.pallas_call(
        _attn_b_kernel,
        grid=(b, HEAD_PAIRS),
        in_specs=[pl.BlockSpec((None, nkb, None, 2, LANES, tq), lambda bb, p: (bb, 0, p, 0, 0, 0)),
                  pl.BlockSpec((None, s, 2 * LANES), lambda bb, p: (bb, 0, p)),
                  pl.BlockSpec((None, nkb, None, 2, VT_ROWS, tq),
                               lambda bb, p: (bb, 0, p, 0, 0, 0))],
        out_specs=pl.BlockSpec((None, s, LANES), lambda bb, p: (bb, 0, p)),
        out_shape=jax.ShapeDtypeStruct((b, s, WIDTH_B), BF16),
        scratch_shapes=[pltpu.VMEM((VT_ROWS, tq), F32)] * 2 + [pltpu.VMEM((1, tq), F32)] * 2
        + [pltpu.VMEM((2, tq, tq), F32)] * 2 + [pltpu.VMEM((2, 1, tq), F32)] * 2,
        compiler_params=pltpu.CompilerParams(
            dimension_semantics=("parallel", "parallel"), vmem_limit_bytes=VMEM_LIMIT),
        name="attn_b",
    )(qbt, kb, vbt)


def _ffn_kernel(x_ref, a_ref, b_ref, woa_ref, wob_ref, g_ref, w1_ref, w2_ref, o_ref):
    h = x_ref[...] + _dot(a_ref[...], woa_ref[...]) + _dot(b_ref[...], wob_ref[...])
    hn = (h * lax.rsqrt(jnp.mean(h * h, axis=-1, keepdims=True) + EPS) * g_ref[...]).astype(BF16)
    mlp = None
    for c in range(D_FF // FF_CHUNK):
        sl = slice(c * FF_CHUNK, (c + 1) * FF_CHUNK)
        hid = jnp.square(jnp.maximum(_dot(hn, w1_ref[:, sl]), 0.0)).astype(BF16)
        d = _dot(hid, w2_ref[sl, :])
        mlp = d if mlp is None else mlp + d
    o_ref[...] = h + mlp


def _ffn_call(x2, a2, b2, woa, wob, g, w1, w2):
    n = x2.shape[0]
    tm = TM_FFN
    row = lambda i: (i, 0)
    consts = (woa, wob, g, w1, w2)
    return pl.pallas_call(
        _ffn_kernel,
        grid=(n // tm,),
        in_specs=[pl.BlockSpec((tm, D_MODEL), row), pl.BlockSpec((tm, WIDTH_A), row),
                  pl.BlockSpec((tm, WIDTH_B), row)] + [_const_spec(c.shape) for c in consts],
        out_specs=pl.BlockSpec((tm, D_MODEL), row),
        out_shape=jax.ShapeDtypeStruct((n, D_MODEL), F32),
        compiler_params=pltpu.CompilerParams(
            dimension_semantics=("parallel",), vmem_limit_bytes=VMEM_LIMIT),
        name="ffn",
    )(x2, a2, b2, *consts)


def _pad_heads(w, n_heads, width):
    k = w.shape[0]
    w = w.reshape(k, n_heads, width)
    return jnp.pad(w, ((0, 0), (0, 0), (0, LANES - width))).reshape(k, n_heads * LANES)


def _lane_row(g, offset=0):
    return jnp.pad(g.astype(F32), (offset, LANES - offset - g.shape[0]))[None, :]


def kernel(x, positions, norm_mix_g, w_in, qnorm_a_g, knorm_a_g, rel_bias, cq_norm_g, ckv_norm_g,
           w_uq, w_ukv, qnorm_b_g, knorm_b_g, w_o, norm_ffn_g, w_ff1, w_ff2):
    b, s, d = x.shape
    n = b * s
    c0 = 3 * WIDTH_A
    c1 = c0 + Q_LORA_RANK
    c2 = c1 + KV_LORA_RANK
    w_qkv = w_in[:, :c0].astype(BF16)
    w_cq = w_in[:, c0:c1].astype(BF16)
    w_ckv = w_in[:, c1:c2].astype(BF16)
    w_kr = jnp.pad(w_in[:, c2:], ((0, 0), (QK_NOPE_DIM, LANES - QK_HEAD_DIM_B))).astype(BF16)
    w_uqt = _pad_heads(w_uq, N_HEADS_B, QK_HEAD_DIM_B).T.astype(BF16)
    w_ukv3 = w_ukv.reshape(KV_LORA_RANK, N_HEADS_B, QK_NOPE_DIM + V_HEAD_DIM)
    w_uk_p = _pad_heads(w_ukv3[:, :, :QK_NOPE_DIM].reshape(KV_LORA_RANK, -1),
                        N_HEADS_B, QK_NOPE_DIM).astype(BF16)
    ones_rows = VT_ROWS - V_HEAD_DIM
    w_uvt = jnp.transpose(w_ukv3[:, :, QK_NOPE_DIM:], (1, 2, 0))
    w_uvt = jnp.pad(w_uvt, ((0, 0), (0, ones_rows), (0, 0)))
    w_uvt = w_uvt.reshape(N_HEADS_B * VT_ROWS, KV_LORA_RANK).astype(BF16)
    inv_freq = 1.0 / (ROPE_THETA ** (jnp.arange(0, QK_ROPE_DIM, 2, dtype=F32) / QK_ROPE_DIM))
    ones_col = jnp.tile(jnp.concatenate([jnp.zeros((V_HEAD_DIM,), F32), jnp.ones((ones_rows,), F32)]),
                        N_HEADS_B)[:, None]
    g_mix = norm_mix_g.astype(F32)[None, :]
    head_sum = jnp.asarray(np.kron(np.eye(N_HEADS_A), np.ones((HEAD_DIM_A, HEAD_DIM_A))), BF16)
    consts_a = (
        g_mix, w_qkv, head_sum,
        jnp.tile(qnorm_a_g.astype(F32), N_HEADS_A)[None, :],
        jnp.tile(knorm_a_g.astype(F32), N_HEADS_A)[None, :],
    )
    tm = TM_PROJ
    gq_col = jnp.pad(qnorm_b_g.astype(F32), (0, LANES - QK_HEAD_DIM_B))[:, None]
    consts_b = (
        g_mix, w_cq, w_ckv, w_kr,
        cq_norm_g.astype(F32)[None, :], ckv_norm_g.astype(F32)[None, :],
        w_uqt, w_uk_p, w_uvt,
        jnp.broadcast_to(gq_col, (LANES, tm)), _lane_row(knorm_b_g[:QK_NOPE_DIM]),
        _lane_row(knorm_b_g[QK_NOPE_DIM:], QK_NOPE_DIM),
        jnp.broadcast_to(inv_freq[:, None], (QK_ROPE_DIM // 2, tm)),
        ones_col,
    )
    x2 = x.reshape(n, d)
    qa, ka, va = _proj_a_call(x, consts_a)
    qbt, kb, vbt = _proj_b_call(x2, positions.reshape(n // tm, 1, tm), consts_b)

    out_a = _attn_a_call(qa, ka, va, _attn_a_bias(rel_bias))
    out_b = _attn_b_call(qbt, kb.reshape(b, s, -1), vbt)

    out = _ffn_call(x.reshape(n, d), out_a.reshape(n, -1), out_b.reshape(n, -1),
                    w_o[:WIDTH_A].astype(BF16), w_o[WIDTH_A:].astype(BF16),
                    norm_ffn_g.astype(F32)[None, :], w_ff1.astype(BF16), w_ff2.astype(BF16))
    return out.reshape(b, s, d)
```

```python
import functools
import math

import numpy as np
import jax
import jax.numpy as jnp
from jax import lax
from jax.experimental import pallas as pl
from jax.experimental.pallas import tpu as pltpu

F32 = jnp.float32
BF16 = jnp.bfloat16

D_MODEL = 1024
HEAD_DIM_A = 64
N_HEADS_A = 8
WIDTH_A = N_HEADS_A * HEAD_DIM_A
DILATED_BRANCHES = ((128, 1), (512, 4), (2048, 16))
N_HEADS_B = 8
QK_NOPE_DIM = 64
QK_ROPE_DIM = 32
V_HEAD_DIM = 64
QK_HEAD_DIM_B = QK_NOPE_DIM + QK_ROPE_DIM
Q_LORA_RANK = 768
KV_LORA_RANK = 256
WIDTH_B = N_HEADS_B * V_HEAD_DIM
ROPE_THETA = 10000.0
D_FF = 4 * D_MODEL
REL_BUCKETS = 32
REL_MAX_DIST = 2048
EPS = 1e-6

LANES = 128
HALF = LANES // 2
HEAD_PAIRS = N_HEADS_A // 2
WIN = 128
NEG = -1e30
VMEM_LIMIT = 56 * 1024 * 1024

TM_PROJ = 512
TM_PROJ_A = 1024
SUB_A = 256
TM_FFN = 512
TQ_B = 512
FF_CHUNK = 1024
GROUP_A = 32
CLASSES = 4

LOG2E = math.log2(math.e)
SCALE_A = LOG2E / math.sqrt(HEAD_DIM_A)
SCALE_B = LOG2E / math.sqrt(QK_HEAD_DIM_B)

BF16_ROWS = 16
VT_ROWS = V_HEAD_DIM + BF16_ROWS

assert HEAD_DIM_A == HALF and V_HEAD_DIM == HALF
assert TM_PROJ == TQ_B


def _nt_dot(a, b):
    return lax.dot_general(a, b, (((1,), (1,)), ((), ())), preferred_element_type=F32)


def _dot(a, b):
    return jnp.dot(a, b, preferred_element_type=F32)


def _const_spec(shape):
    nd = len(shape)
    return pl.BlockSpec(shape, lambda *_: (0,) * nd, pipeline_mode=pl.Buffered(1))


def _rms_bf16(x, g_row):
    return (x * lax.rsqrt(jnp.mean(x * x, axis=-1, keepdims=True) + EPS) * g_row).astype(BF16)


def _proj_a_kernel(x_ref, gmix_ref, wqkv_ref, hsum_ref, gqa_ref, gka_ref, qa_ref, ka_ref, va_ref,
                   xs_ref):
    tm, d = x_ref.shape

    def head_sumsq(y):
        return _dot((y * y).astype(BF16), hsum_ref[...])

    sub = SUB_A // CLASSES
    for h in range(tm // SUB_A):
        toks = slice(h * SUB_A, (h + 1) * SUB_A)
        for c in range(d // LANES):
            xs_ref[c, toks, :] = x_ref[toks, c * LANES:(c + 1) * LANES]
        x = jnp.concatenate(
            [jnp.concatenate([xs_ref[c, pl.ds(r + h * SUB_A, sub, stride=CLASSES), :]
                              for r in range(CLASSES)], axis=0)
             for c in range(d // LANES)], axis=1)
        xn = _rms_bf16(x, gmix_ref[...])
        q = _dot(xn, wqkv_ref[:, :WIDTH_A])
        k = _dot(xn, wqkv_ref[:, WIDTH_A:2 * WIDTH_A])
        ssq = head_sumsq(q)
        v = _dot(xn, wqkv_ref[:, 2 * WIDTH_A:]).astype(BF16)
        ssk = head_sumsq(k)
        q = (q * lax.rsqrt(ssq / HEAD_DIM_A + EPS) * gqa_ref[...] * SCALE_A).astype(BF16)
        k = (k * lax.rsqrt(ssk / HEAD_DIM_A + EPS) * gka_ref[...]).astype(BF16)
        for r in range(CLASSES):
            src = slice(r * sub, (r + 1) * sub)
            dst = slice(h * sub, (h + 1) * sub)
            qa_ref[r, dst] = q[src]
            ka_ref[r, dst] = k[src]
            va_ref[r, dst] = v[src]


def _proj_a_call(x, consts):
    b, s, d = x.shape
    tm = TM_PROJ_A
    run = tm // CLASSES
    out = jax.ShapeDtypeStruct((b, CLASSES, s // CLASSES, WIDTH_A), BF16)
    outs = pl.pallas_call(
        _proj_a_kernel,
        grid=(b, s // tm),
        in_specs=[pl.BlockSpec((None, tm, d), lambda bb, i: (bb, i, 0))]
        + [_const_spec(c.shape) for c in consts],
        out_specs=[pl.BlockSpec((None, CLASSES, run, WIDTH_A), lambda bb, i: (bb, 0, i, 0))] * 3,
        out_shape=[out] * 3,
        scratch_shapes=[pltpu.VMEM((d // LANES, tm, LANES), F32)],
        compiler_params=pltpu.CompilerParams(
            dimension_semantics=("parallel", "parallel"), vmem_limit_bytes=VMEM_LIMIT),
        name="proj_a",
    )(x, *consts)
    return [o.reshape(b, s, WIDTH_A) for o in outs]


def _proj_b_kernel(x_ref, posr_ref, gmix_ref, wcq_ref, wckv_ref, wkr_ref,
                   gcq_ref, gckv_ref, wuqt_ref, wuk_ref, wuvt_ref,
                   gqbt_ref, gkbn_ref, gkbr_ref, invft_ref, ones_ref,
                   qt_ref, kb_ref, vt_ref):
    tm = x_ref.shape[0]
    half = QK_ROPE_DIM // 2
    r0, r1, r2 = QK_NOPE_DIM, QK_NOPE_DIM + half, QK_HEAD_DIM_B
    xn = _rms_bf16(x_ref[...], gmix_ref[...])
    cq = _dot(xn, wcq_ref[...])
    ckv = _dot(xn, wckv_ref[...])
    kr = _dot(xn, wkr_ref[...])
    ang_t = invft_ref[...] * posr_ref[...].astype(F32)
    cs_t = jnp.cos(ang_t)
    sn_t = jnp.sin(ang_t)

    ckvn = _rms_bf16(ckv, gckv_ref[...])
    kn = _dot(ckvn, wuk_ref[...])
    vt_ref[...] = (_nt_dot(wuvt_ref[...], ckvn) + ones_ref[...]).astype(BF16)

    cqn = _rms_bf16(cq, gcq_ref[...])
    qt = _nt_dot(wuqt_ref[...], cqn)

    zt = jnp.zeros((r0, tm), F32)
    zb = jnp.zeros((LANES - r2, tm), F32)
    cs = jnp.concatenate([zt, cs_t, cs_t, zb], axis=0).T
    sn = jnp.concatenate([zt, -sn_t, sn_t, zb], axis=0).T
    ss_kr = jnp.sum(kr * kr, axis=-1, keepdims=True)
    krg = kr * gkbr_ref[...]
    kr_rot = krg * cs + (pltpu.roll(krg, LANES - half, 1) + pltpu.roll(krg, half, 1)) * sn
    for h in range(N_HEADS_B):
        sl = slice(h * LANES, (h + 1) * LANES)
        kh = kn[:, sl]
        rk = lax.rsqrt((jnp.sum(kh * kh, axis=-1, keepdims=True) + ss_kr) / QK_HEAD_DIM_B + EPS)
        kb_ref[:, sl] = ((kh * gkbn_ref[...] + kr_rot) * rk).astype(BF16)

    for h in range(N_HEADS_B):
        qh = qt[h * LANES:(h + 1) * LANES]
        rq = lax.rsqrt(jnp.sum(qh * qh, axis=0, keepdims=True) / QK_HEAD_DIM_B + EPS)
        y = qh * rq * gqbt_ref[...]
        y1, y2 = y[r0:r1], y[r1:r2]
        out = jnp.concatenate([y[:r0], y1 * cs_t - y2 * sn_t, y2 * cs_t + y1 * sn_t, y[r2:]], axis=0)
        qt_ref[h * LANES:(h + 1) * LANES, :] = (out * SCALE_B).astype(BF16)


def _proj_b_call(x2, pos_row, consts):
    n = x2.shape[0]
    tm = TM_PROJ
    row = lambda i: (i, 0)
    blk3 = lambda i: (i, 0, 0)
    wide = N_HEADS_B * LANES
    vt_rows = N_HEADS_B * VT_ROWS
    return pl.pallas_call(
        _proj_b_kernel,
        grid=(n // tm,),
        in_specs=[pl.BlockSpec((tm, D_MODEL), row),
                  pl.BlockSpec((None, 1, tm), blk3)] + [_const_spec(c.shape) for c in consts],
        out_specs=[pl.BlockSpec((None, wide, tm), blk3), pl.BlockSpec((tm, wide), row),
                   pl.BlockSpec((None, vt_rows, tm), blk3)],
        out_shape=[jax.ShapeDtypeStruct((n // tm, wide, tm), BF16),
                   jax.ShapeDtypeStruct((n, wide), BF16),
                   jax.ShapeDtypeStruct((n // tm, vt_rows, tm), BF16)],
        compiler_params=pltpu.CompilerParams(
            dimension_semantics=("parallel",), vmem_limit_bytes=VMEM_LIMIT),
        name="proj_b",
    )(x2, pos_row, *consts)


def _attn_a_kernel(q_ref, k_ref, v_ref, bias_ref, o_ref,
                   qf, kf, vf, acc, den, m0, m1, onat, *, seq, pad):
    lane = lax.broadcasted_iota(jnp.int32, (1, LANES), 1)
    first_head = lane < HALF
    qf[...] = q_ref[...].astype(F32)
    zeros = jnp.zeros((pad, LANES), F32)
    kf[pl.ds(0, pad), :] = zeros
    vf[pl.ds(0, pad), :] = zeros
    kf[pl.ds(pad, seq), :] = k_ref[...].astype(F32)
    vf[pl.ds(pad, seq), :] = v_ref[...].astype(F32)

    cls_rows = seq // CLASSES

    for bi, (window, dil) in enumerate(DILATED_BRANCHES):
        assert window // dil == WIN
        nchunk = max(CLASSES // dil, 1)
        clen = WIN // nchunk
        stride = max(dil // CLASSES, 1)
        step = stride * clen
        nblk = seq // (dil * WIN)
        ublk = min(GROUP_A, nblk)
        ncls = GROUP_A // ublk

        def run(start, clen=clen, stride=stride):
            return pl.ds(start, clen, stride=stride) if stride > 1 else pl.ds(start, clen)

        def gather(ref, starts, run=run):
            return jnp.concatenate([ref[run(st), :] for st in starts], axis=0)

        def group(r0, g, bi=bi, dil=dil, nchunk=nchunk, clen=clen, stride=stride, step=step,
                  ublk=ublk, ncls=ncls, run=run, gather=gather):
            plans = []
            for cc in range(ncls):
                rd = r0 + cc
                if stride == 1:
                    bases = [pl.multiple_of((dil * c + rd) * cls_rows + step * ublk * g, 8)
                             for c in range(nchunk)]
                else:
                    bases = [(rd % CLASSES) * cls_rows + rd // CLASSES + step * ublk * g]
                for u in range(ublk):
                    plans.append((bases, u))

            def scores(bases, u):
                qstarts = [b0 + step * u for b0 in bases]
                kstarts = [pad + b0 + step * (u - 1) for b0 in bases]
                q = gather(qf, qstarts)
                qq = jnp.concatenate([jnp.where(first_head, q, 0.0),
                                      jnp.where(first_head, 0.0, q)], axis=0).astype(BF16)
                kk = jnp.concatenate([gather(kf, kstarts),
                                      gather(kf, [k0 + step for k0 in kstarts])],
                                     axis=0).astype(BF16)
                vv = jnp.concatenate([gather(vf, kstarts),
                                      gather(vf, [k0 + step for k0 in kstarts])],
                                     axis=0).astype(BF16)
                first_blk = jnp.where(g == 0, 1, 0) if u == 0 else 0
                s = _nt_dot(qq, kk) + bias_ref[bi, first_blk]
                return qstarts, s, jnp.concatenate([vv, jnp.ones_like(vv)], axis=1)

            def update(qstarts, s, vext):
                ps, alphas = [], []
                for hh, m_s in enumerate((m0, m1)):
                    sh = s[hh * WIN:(hh + 1) * WIN]
                    m_blk = jnp.max(sh, axis=-1, keepdims=True)
                    if bi == 0:
                        m_new = jnp.broadcast_to(m_blk, (WIN, LANES))
                    else:
                        m_old = gather(m_s, qstarts)
                        m_new = jnp.maximum(m_old, m_blk)
                        alphas.append(jnp.exp2(m_old - m_new))
                    ps.append(jnp.exp2(sh - jnp.concatenate([m_new, m_new], axis=1)).astype(BF16))
                    for c, st in enumerate(qstarts):
                        m_s[run(st), :] = m_new[c * clen:(c + 1) * clen]
                res = _dot(jnp.concatenate(ps, axis=0), vext)
                acc_new = jnp.where(first_head, res[:WIN, :LANES], res[WIN:, :LANES])
                den_new = jnp.where(first_head, res[:WIN, LANES:], res[WIN:, LANES:])
                if bi > 0:
                    alpha = jnp.where(first_head, alphas[0], alphas[1])
                    acc_new = alpha * gather(acc, qstarts) + acc_new
                    den_new = alpha * gather(den, qstarts) + den_new
                for c, st in enumerate(qstarts):
                    acc[run(st), :] = acc_new[c * clen:(c + 1) * clen]
                    den[run(st), :] = den_new[c * clen:(c + 1) * clen]

            pending = scores(*plans[0])
            for nxt in plans[1:]:
                ahead = scores(*nxt)
                update(*pending)
                pending = ahead
            update(*pending)

        def class_body(ci, carry, ngrp=nblk // ublk, ncls=ncls, group=group):
            def grp_body(g, c):
                group(ci * ncls, g)
                return c
            return lax.fori_loop(0, ngrp, grp_body, carry)

        lax.fori_loop(0, dil // ncls, class_body, 0)

    for r in range(CLASSES):
        rows = pl.ds(r * cls_rows, cls_rows)
        onat[pl.ds(r, cls_rows, stride=CLASSES), :] = acc[rows, :] / den[rows, :]
    o_ref[...] = onat[...].astype(o_ref.dtype)


def _attn_a_call(qa, ka, va, bias):
    b, s, _ = qa.shape
    pad = WIN * DILATED_BRANCHES[-1][1] // CLASSES
    blk = pl.BlockSpec((None, s, LANES), lambda bb, p: (bb, 0, p))
    bias_spec = pl.BlockSpec((None,) + bias.shape[1:], lambda bb, p: (p, 0, 0, 0, 0))
    rows = pltpu.VMEM((s, LANES), F32)
    padded = pltpu.VMEM((pad + s, LANES), F32)
    return pl.pallas_call(
        functools.partial(_attn_a_kernel, seq=s, pad=pad),
        grid=(b, HEAD_PAIRS),
        in_specs=[blk, blk, blk, bias_spec],
        out_specs=blk,
        out_shape=jax.ShapeDtypeStruct((b, s, WIDTH_A), BF16),
        scratch_shapes=[rows, padded, padded, rows, rows, rows, rows, rows],
        compiler_params=pltpu.CompilerParams(
            dimension_semantics=("parallel", "parallel"), vmem_limit_bytes=VMEM_LIMIT),
        name="attn_a",
    )(qa, ka, va, bias)


def _t5_causal_bucket(dist):
    dist = np.asarray(dist, dtype=np.int64)
    max_exact = REL_BUCKETS // 2
    safe = np.maximum(dist, 1).astype(np.float32)
    large = max_exact + (np.log(safe / max_exact) / math.log(REL_MAX_DIST / max_exact)
                         * (REL_BUCKETS - max_exact)).astype(np.int64)
    large = np.minimum(large, REL_BUCKETS - 1)
    return np.where(dist < max_exact, dist, large).astype(np.int32)


def _bias_bucket_tables():
    i = np.arange(WIN)[:, None]
    c = np.arange(2 * WIN)[None, :]
    sub = WIN + i - c
    tables = []
    for _, dil in DILATED_BRANCHES:
        bucket = _t5_causal_bucket(np.clip(sub, 0, WIN) * dil)
        bucket = np.where((sub >= 0) & (sub <= WIN), bucket, REL_BUCKETS)
        nchunk = max(CLASSES // dil, 1)
        clen = WIN // nchunk
        bucket = bucket.reshape(clen, nchunk, 2, clen, nchunk)
        tables.append(np.transpose(bucket, (1, 0, 2, 4, 3)).reshape(WIN, 2 * WIN))
    return np.stack(tables).astype(np.int32)


def _bias_kernel(bias_ref, bucket_ref, o_ref):
    bucket = bucket_ref[...]
    first_blk_cols = lax.broadcasted_iota(jnp.int32, bucket.shape, 1) >= WIN
    for h in range(N_HEADS_A):
        t = jnp.full(bucket.shape, NEG, F32)
        for bkt in range(REL_BUCKETS):
            t = jnp.where(bucket == bkt, bias_ref[bkt, h], t)
        rows = slice((h % 2) * WIN, (h % 2 + 1) * WIN)
        o_ref[h // 2, 0, rows, :] = t
        o_ref[h // 2, 1, rows, :] = jnp.where(first_blk_cols, t, NEG)


def _attn_a_bias(rel_bias):
    buckets = jnp.asarray(_bias_bucket_tables())
    nbr = len(DILATED_BRANCHES)
    return pl.pallas_call(
        _bias_kernel,
        grid=(nbr,),
        in_specs=[pl.BlockSpec(memory_space=pltpu.SMEM),
                  pl.BlockSpec((None, WIN, 2 * WIN), lambda i: (i, 0, 0))],
        out_specs=pl.BlockSpec((HEAD_PAIRS, None, 2, 2 * WIN, 2 * WIN), lambda i: (0, i, 0, 0, 0)),
        out_shape=jax.ShapeDtypeStruct((HEAD_PAIRS, nbr, 2, 2 * WIN, 2 * WIN), F32),
        compiler_params=pltpu.CompilerParams(dimension_semantics=("parallel",)),
        name="bias_a",
    )(rel_bias.astype(F32) * LOG2E, buckets)


def _attn_b_kernel(qt_ref, k_ref, vt_ref, o_ref, acc0, acc1, m0, m1, s_even, s_odd, mb_even, mb_odd):
    tq = TQ_B
    nq = qt_ref.shape[0]
    heads = ((acc0, m0), (acc1, m1))
    bufs = ((s_even, mb_even), (s_odd, mb_odd))

    hq = tq // 2

    def causal(shape):
        return (lax.broadcasted_iota(jnp.int32, shape, 0)
                <= lax.broadcasted_iota(jnp.int32, shape, 1))

    def step(fold=None, make=None):
        for hh, (acc, m_s) in enumerate(heads):
            sl = slice(hh * LANES, (hh + 1) * LANES)
            if make is None:
                mats = [lambda: None, lambda: None]
            elif make[0] == "full":
                _, qi_a, kb_a, (s_a, mb_a) = make
                k_blk = k_ref[pl.ds(pl.multiple_of(kb_a * tq, tq), tq), sl]

                def mat(part, qi_a=qi_a, s_a=s_a, mb_a=mb_a, k_blk=k_blk):
                    half = slice(part * hq, (part + 1) * hq)
                    st = _dot(k_blk, qt_ref[qi_a, hh, :, half])
                    s_a[hh, :, half] = st
                    mb_a[hh, :, half] = jnp.max(st, axis=0, keepdims=True)
                mats = [functools.partial(mat, 0), functools.partial(mat, 1)]
            else:
                _, qi_a, s_a = make
                k0 = qi_a * tq

                def top_mat(qi_a=qi_a, s_a=s_a, k0=k0):
                    s_a[hh, :hq, :] = _dot(k_ref[pl.ds(k0, hq), sl], qt_ref[qi_a, hh])
                def bot_mat(qi_a=qi_a, s_a=s_a, k0=k0):
                    s_a[hh, hq:, :hq] = _dot(k_ref[pl.ds(k0 + hq, hq), sl], qt_ref[qi_a, hh, :, hq:])
                mats = [top_mat, bot_mat]
            if fold is None:
                for mat_piece in mats:
                    mat_piece()
                continue

            m_old = m_s[...]
            if fold[0] == "full":
                _, kb, (s_c, mb_c) = fold
                m_new = jnp.maximum(m_old, mb_c[hh])
                m_s[...] = m_new
                p0 = jnp.exp2(s_c[hh, :hq, :] - m_new).astype(BF16)
                mats[0]()
                p1 = jnp.exp2(s_c[hh, hq:, :] - m_new).astype(BF16)
                mats[1]()
                pv = _dot(vt_ref[kb, hh], jnp.concatenate([p0, p1], axis=0))
            else:
                _, kb, s_c = fold
                top = jnp.where(causal((hq, tq)), s_c[hh, :hq, :], NEG)
                bot = jnp.where(causal((hq, hq)), s_c[hh, hq:, :hq], NEG)
                m_top = jnp.max(top, axis=0, keepdims=True)
                m_bot = jnp.max(jnp.concatenate([jnp.full((hq, hq), NEG, F32), bot], axis=1),
                                axis=0, keepdims=True)
                m_new = jnp.maximum(m_old, jnp.maximum(m_top, m_bot))
                m_s[...] = m_new
                p0 = jnp.exp2(top - m_new).astype(BF16)
                mats[0]()
                p1 = jnp.exp2(bot - m_s[:, hq:]).astype(BF16)
                mats[1]()
                pv = _dot(vt_ref[kb, hh, :, :hq], p0) + jnp.concatenate(
                    [jnp.zeros((VT_ROWS, hq), F32), _dot(vt_ref[kb, hh, :, hq:], p1)], axis=1)
            acc[...] = jnp.exp2(m_old - m_new) * acc[...] + pv

    step(make=("diag", 0, bufs[0][0]))
    step0 = 0
    for qi in range(nq):
        cur, nxt = bufs[step0 % 2], bufs[(step0 + 1) % 2]
        for acc, m_s in heads:
            acc[...] = jnp.zeros_like(acc)
            m_s[...] = jnp.full(m_s.shape, NEG, F32)

        def pair_body(i, c, qi=qi, cur=cur, nxt=nxt):
            kb = 2 * i
            step(("full", kb, cur), ("full", qi, kb + 1, nxt))
            step(("full", kb + 1, nxt), ("full", qi, kb + 2, cur))
            return c

        npairs = qi // 2
        peel = qi % 2 == 0 and npairs > 0
        lax.fori_loop(0, npairs - 1 if peel else npairs, pair_body, 0)
        if peel:
            step(("full", qi - 2, cur), ("full", qi, qi - 1, nxt))
            step(("full", qi - 1, nxt), ("diag", qi, cur[0]))
        last, other = cur, nxt
        if qi % 2 == 1:
            step(("full", qi - 1, cur), ("diag", qi, nxt[0]))
            last, other = nxt, cur
        step(("diag", qi, last[0]), ("full", qi + 1, 0, other) if qi + 1 < nq else None)

        outs = []
        for acc, _ in heads:
            a = acc[...]
            outs.append(a[:V_HEAD_DIM] / a[V_HEAD_DIM:V_HEAD_DIM + 1])
        o_ref[pl.ds(qi * tq, tq), :] = jnp.concatenate(outs, axis=0).T.astype(o_ref.dtype)
        step0 += qi + 1


def _attn_b_call(qbt, kb, vbt):
    b, s, _ = kb.shape
    tq = TQ_B
    nkb = s // tq
    qbt = qbt.reshape(b, nkb, HEAD_PAIRS, 2, LANES, tq)
    vbt = vbt.reshape(b, nkb, HEAD_PAIRS, 2, VT_ROWS, tq)
    return pl.pallas_call(
        _attn_b_kernel,
        grid=(b, HEAD_PAIRS),
        in_specs=[pl.BlockSpec((None, nkb, None, 2, LANES, tq), lambda bb, p: (bb, 0, p, 0, 0, 0)),
                  pl.BlockSpec((None, s, 2 * LANES), lambda bb, p: (bb, 0, p)),
                  pl.BlockSpec((None, nkb, None, 2, VT_ROWS, tq),
                               lambda bb, p: (bb, 0, p, 0, 0, 0))],
        out_specs=pl.BlockSpec((None, s, LANES), lambda bb, p: (bb, 0, p)),
        out_shape=jax.ShapeDtypeStruct((b, s, WIDTH_B), BF16),
        scratch_shapes=[pltpu.VMEM((VT_ROWS, tq), F32)] * 2 + [pltpu.VMEM((1, tq), F32)] * 2
        + [pltpu.VMEM((2, tq, tq), F32)] * 2 + [pltpu.VMEM((2, 1, tq), F32)] * 2,
        compiler_params=pltpu.CompilerParams(
            dimension_semantics=("parallel", "parallel"), vmem_limit_bytes=VMEM_LIMIT),
        name="attn_b",
    )(qbt, kb, vbt)


def _ffn_kernel(x_ref, a_ref, b_ref, woa_ref, wob_ref, g_ref, w1_ref, w2_ref, o_ref):
    h = x_ref[...] + _dot(a_ref[...], woa_ref[...]) + _dot(b_ref[...], wob_ref[...])
    hn = (h * lax.rsqrt(jnp.mean(h * h, axis=-1, keepdims=True) + EPS) * g_ref[...]).astype(BF16)
    mlp = None
    for c in range(D_FF // FF_CHUNK):
        sl = slice(c * FF_CHUNK, (c + 1) * FF_CHUNK)
        hid = jnp.square(jnp.maximum(_dot(hn, w1_ref[:, sl]), 0.0)).astype(BF16)
        d = _dot(hid, w2_ref[sl, :])
        mlp = d if mlp is None else mlp + d
    o_ref[...] = h + mlp


def _ffn_call(x2, a2, b2, woa, wob, g, w1, w2):
    n = x2.shape[0]
    tm = TM_FFN
    row = lambda i: (i, 0)
    consts = (woa, wob, g, w1, w2)
    return pl.pallas_call(
        _ffn_kernel,
        grid=(n // tm,),
        in_specs=[pl.BlockSpec((tm, D_MODEL), row), pl.BlockSpec((tm, WIDTH_A), row),
                  pl.BlockSpec((tm, WIDTH_B), row)] + [_const_spec(c.shape) for c in consts],
        out_specs=pl.BlockSpec((tm, D_MODEL), row),
        out_shape=jax.ShapeDtypeStruct((n, D_MODEL), F32),
        compiler_params=pltpu.CompilerParams(
            dimension_semantics=("parallel",), vmem_limit_bytes=VMEM_LIMIT),
        name="ffn",
    )(x2, a2, b2, *consts)


def _pad_heads(w, n_heads, width):
    k = w.shape[0]
    w = w.reshape(k, n_heads, width)
    return jnp.pad(w, ((0, 0), (0, 0), (0, LANES - width))).reshape(k, n_heads * LANES)


def _lane_row(g, offset=0):
    return jnp.pad(g.astype(F32), (offset, LANES - offset - g.shape[0]))[None, :]


def kernel(x, positions, norm_mix_g, w_in, qnorm_a_g, knorm_a_g, rel_bias, cq_norm_g, ckv_norm_g,
           w_uq, w_ukv, qnorm_b_g, knorm_b_g, w_o, norm_ffn_g, w_ff1, w_ff2):
    b, s, d = x.shape
    n = b * s
    c0 = 3 * WIDTH_A
    c1 = c0 + Q_LORA_RANK
    c2 = c1 + KV_LORA_RANK
    w_qkv = w_in[:, :c0].astype(BF16)
    w_cq = w_in[:, c0:c1].astype(BF16)
    w_ckv = w_in[:, c1:c2].astype(BF16)
    w_kr = jnp.pad(w_in[:, c2:], ((0, 0), (QK_NOPE_DIM, LANES - QK_HEAD_DIM_B))).astype(BF16)
    w_uqt = _pad_heads(w_uq, N_HEADS_B, QK_HEAD_DIM_B).T.astype(BF16)
    w_ukv3 = w_ukv.reshape(KV_LORA_RANK, N_HEADS_B, QK_NOPE_DIM + V_HEAD_DIM)
    w_uk_p = _pad_heads(w_ukv3[:, :, :QK_NOPE_DIM].reshape(KV_LORA_RANK, -1),
                        N_HEADS_B, QK_NOPE_DIM).astype(BF16)
    ones_rows = VT_ROWS - V_HEAD_DIM
    w_uvt = jnp.transpose(w_ukv3[:, :, QK_NOPE_DIM:], (1, 2, 0))
    w_uvt = jnp.pad(w_uvt, ((0, 0), (0, ones_rows), (0, 0)))
    w_uvt = w_uvt.reshape(N_HEADS_B * VT_ROWS, KV_LORA_RANK).astype(BF16)
    inv_freq = 1.0 / (ROPE_THETA ** (jnp.arange(0, QK_ROPE_DIM, 2, dtype=F32) / QK_ROPE_DIM))
    ones_col = jnp.tile(jnp.concatenate([jnp.zeros((V_HEAD_DIM,), F32), jnp.ones((ones_rows,), F32)]),
                        N_HEADS_B)[:, None]
    g_mix = norm_mix_g.astype(F32)[None, :]
    head_sum = jnp.asarray(np.kron(np.eye(N_HEADS_A), np.ones((HEAD_DIM_A, HEAD_DIM_A))), BF16)
    consts_a = (
        g_mix, w_qkv, head_sum,
        jnp.tile(qnorm_a_g.astype(F32), N_HEADS_A)[None, :],
        jnp.tile(knorm_a_g.astype(F32), N_HEADS_A)[None, :],
    )
    tm = TM_PROJ
    gq_col = jnp.pad(qnorm_b_g.astype(F32), (0, LANES - QK_HEAD_DIM_B))[:, None]
    consts_b = (
        g_mix, w_cq, w_ckv, w_kr,
        cq_norm_g.astype(F32)[None, :], ckv_norm_g.astype(F32)[None, :],
        w_uqt, w_uk_p, w_uvt,
        jnp.broadcast_to(gq_col, (LANES, tm)), _lane_row(knorm_b_g[:QK_NOPE_DIM]),
        _lane_row(knorm_b_g[QK_NOPE_DIM:], QK_NOPE_DIM),
        jnp.broadcast_to(inv_freq[:, None], (QK_ROPE_DIM // 2, tm)),
        ones_col,
    )
    x2 = x.reshape(n, d)
    qa, ka, va = _proj_a_call(x, consts_a)
    qbt, kb, vbt = _proj_b_call(x2, positions.reshape(n // tm, 1, tm), consts_b)

    out_a = _attn_a_call(qa, ka, va, _attn_a_bias(rel_bias))
    out_b = _attn_b_call(qbt, kb.reshape(b, s, -1), vbt)

    out = _ffn_call(x.reshape(n, d), out_a.reshape(n, -1), out_b.reshape(n, -1),
                    w_o[:WIDTH_A].astype(BF16), w_o[WIDTH_A:].astype(BF16),
                    norm_ffn_g.astype(F32)[None, :], w_ff1.astype(BF16), w_ff2.astype(BF16))
    return out.reshape(b, s, d)
```

```python
import functools
import math

import numpy as np
import jax
import jax.numpy as jnp
from jax import lax
from jax.experimental import pallas as pl
from jax.experimental.pallas import tpu as pltpu

F32 = jnp.float32
BF16 = jnp.bfloat16

D_MODEL = 1024
HEAD_DIM_A = 64
N_HEADS_A = 8
WIDTH_A = N_HEADS_A * HEAD_DIM_A
DILATED_BRANCHES = ((128, 1), (512, 4), (2048, 16))
N_HEADS_B = 8
QK_NOPE_DIM = 64
QK_ROPE_DIM = 32
V_HEAD_DIM = 64
QK_HEAD_DIM_B = QK_NOPE_DIM + QK_ROPE_DIM
Q_LORA_RANK = 768
KV_LORA_RANK = 256
WIDTH_B = N_HEADS_B * V_HEAD_DIM
ROPE_THETA = 10000.0
D_FF = 4 * D_MODEL
REL_BUCKETS = 32
REL_MAX_DIST = 2048
EPS = 1e-6

LANES = 128
HALF = LANES // 2
HEAD_PAIRS = N_HEADS_A // 2
WIN = 128
NEG = -1e30
VMEM_LIMIT = 56 * 1024 * 1024

TM_PROJ = 512
TM_PROJ_A = 1024
SUB_A = 256
TM_FFN = 512
TQ_B = 512
FF_CHUNK = 1024
GROUP_A = 32
CLASSES = 4

LOG2E = math.log2(math.e)
SCALE_A = LOG2E / math.sqrt(HEAD_DIM_A)
SCALE_B = LOG2E / math.sqrt(QK_HEAD_DIM_B)

BF16_ROWS = 16
VT_ROWS = V_HEAD_DIM + BF16_ROWS

assert HEAD_DIM_A == HALF and V_HEAD_DIM == HALF
assert TM_PROJ == TQ_B


def _nt_dot(a, b):
    return lax.dot_general(a, b, (((1,), (1,)), ((), ())), preferred_element_type=F32)


def _dot(a, b):
    return jnp.dot(a, b, preferred_element_type=F32)


def _const_spec(shape):
    nd = len(shape)
    return pl.BlockSpec(shape, lambda *_: (0,) * nd, pipeline_mode=pl.Buffered(1))


def _rms_bf16(x, g_row):
    return (x * lax.rsqrt(jnp.mean(x * x, axis=-1, keepdims=True) + EPS) * g_row).astype(BF16)


def _proj_a_kernel(x_ref, gmix_ref, wqkv_ref, hsum_ref, gqa_ref, gka_ref, qa_ref, ka_ref, va_ref,
                   xs_ref):
    tm, d = x_ref.shape

    def head_sumsq(y):
        return _dot((y * y).astype(BF16), hsum_ref[...])

    sub = SUB_A // CLASSES
    for h in range(tm // SUB_A):
        toks = slice(h * SUB_A, (h + 1) * SUB_A)
        for c in range(d // LANES):
            xs_ref[c, toks, :] = x_ref[toks, c * LANES:(c + 1) * LANES]
        x = jnp.concatenate(
            [jnp.concatenate([xs_ref[c, pl.ds(r + h * SUB_A, sub, stride=CLASSES), :]
                              for r in range(CLASSES)], axis=0)
             for c in range(d // LANES)], axis=1)
        xn = _rms_bf16(x, gmix_ref[...])
        q = _dot(xn, wqkv_ref[:, :WIDTH_A])
        k = _dot(xn, wqkv_ref[:, WIDTH_A:2 * WIDTH_A])
        ssq = head_sumsq(q)
        v = _dot(xn, wqkv_ref[:, 2 * WIDTH_A:]).astype(BF16)
        ssk = head_sumsq(k)
        q = (q * lax.rsqrt(ssq / HEAD_DIM_A + EPS) * gqa_ref[...]).astype(BF16)
        k = (k * lax.rsqrt(ssk / HEAD_DIM_A + EPS) * gka_ref[...]).astype(BF16)
        for r in range(CLASSES):
            src = slice(r * sub, (r + 1) * sub)
            dst = slice(h * sub, (h + 1) * sub)
            qa_ref[r, dst] = q[src]
            ka_ref[r, dst] = k[src]
            va_ref[r, dst] = v[src]


def _proj_a_call(x, consts):
    b, s, d = x.shape
    tm = TM_PROJ_A
    run = tm // CLASSES
    out = jax.ShapeDtypeStruct((b, CLASSES, s // CLASSES, WIDTH_A), BF16)
    outs = pl.pallas_call(
        _proj_a_kernel,
        grid=(b, s // tm),
        in_specs=[pl.BlockSpec((None, tm, d), lambda bb, i: (bb, i, 0))]
        + [_const_spec(c.shape) for c in consts],
        out_specs=[pl.BlockSpec((None, CLASSES, run, WIDTH_A), lambda bb, i: (bb, 0, i, 0))] * 3,
        out_shape=[out] * 3,
        scratch_shapes=[pltpu.VMEM((d // LANES, tm, LANES), F32)],
        compiler_params=pltpu.CompilerParams(
            dimension_semantics=("parallel", "parallel"), vmem_limit_bytes=VMEM_LIMIT),
        name="proj_a",
    )(x, *consts)
    return [o.reshape(b, s, WIDTH_A) for o in outs]


def _proj_b_kernel(x_ref, posr_ref, gmix_ref, wcq_ref, wckv_ref, wkr_ref,
                   gcq_ref, gckv_ref, wuqt_ref, wuk_ref, wuvt_ref,
                   gqbt_ref, gkbn_ref, gkbr_ref, invft_ref, ones_ref,
                   qt_ref, kb_ref, vt_ref):
    tm = x_ref.shape[0]
    half = QK_ROPE_DIM // 2
    r0, r1, r2 = QK_NOPE_DIM, QK_NOPE_DIM + half, QK_HEAD_DIM_B
    xn = _rms_bf16(x_ref[...], gmix_ref[...])
    cq = _dot(xn, wcq_ref[...])
    ckv = _dot(xn, wckv_ref[...])
    kr = _dot(xn, wkr_ref[...])
    ang_t = invft_ref[...] * posr_ref[...].astype(F32)
    cs_t = jnp.cos(ang_t)
    sn_t = jnp.sin(ang_t)

    ckvn = _rms_bf16(ckv, gckv_ref[...])
    kn = _dot(ckvn, wuk_ref[...])
    vt_ref[...] = (_nt_dot(wuvt_ref[...], ckvn) + ones_ref[...]).astype(BF16)

    cqn = _rms_bf16(cq, gcq_ref[...])
    qt = _nt_dot(wuqt_ref[...], cqn)

    zt = jnp.zeros((r0, tm), F32)
    zb = jnp.zeros((LANES - r2, tm), F32)
    cs = jnp.concatenate([zt, cs_t, cs_t, zb], axis=0).T
    sn = jnp.concatenate([zt, -sn_t, sn_t, zb], axis=0).T
    ss_kr = jnp.sum(kr * kr, axis=-1, keepdims=True)
    krg = kr * gkbr_ref[...]
    kr_rot = krg * cs + (pltpu.roll(krg, LANES - half, 1) + pltpu.roll(krg, half, 1)) * sn
    for h in range(N_HEADS_B):
        sl = slice(h * LANES, (h + 1) * LANES)
        kh = kn[:, sl]
        rk = lax.rsqrt((jnp.sum(kh * kh, axis=-1, keepdims=True) + ss_kr) / QK_HEAD_DIM_B + EPS)
        kb_ref[:, sl] = ((kh * gkbn_ref[...] + kr_rot) * rk).astype(BF16)

    zpad = jnp.zeros((LANES - r2, tm), BF16)
    for h in range(N_HEADS_B):
        qh = qt[h * r2:(h + 1) * r2]
        rq = lax.rsqrt(jnp.sum(qh * qh, axis=0, keepdims=True) / QK_HEAD_DIM_B + EPS)
        y = qh * rq * gqbt_ref[...]
        y1, y2 = y[r0:r1], y[r1:r2]
        out = jnp.concatenate([y[:r0], y1 * cs_t - y2 * sn_t, y2 * cs_t + y1 * sn_t], axis=0)
        qt_ref[h * LANES:(h + 1) * LANES, :] = jnp.concatenate([out.astype(BF16), zpad], axis=0)


def _proj_b_call(x2, pos_row, consts):
    n = x2.shape[0]
    tm = TM_PROJ
    row = lambda i: (i, 0)
    blk3 = lambda i: (i, 0, 0)
    wide = N_HEADS_B * LANES
    vt_rows = N_HEADS_B * VT_ROWS
    return pl.pallas_call(
        _proj_b_kernel,
        grid=(n // tm,),
        in_specs=[pl.BlockSpec((tm, D_MODEL), row),
                  pl.BlockSpec((None, 1, tm), blk3)] + [_const_spec(c.shape) for c in consts],
        out_specs=[pl.BlockSpec((None, wide, tm), blk3), pl.BlockSpec((tm, wide), row),
                   pl.BlockSpec((None, vt_rows, tm), blk3)],
        out_shape=[jax.ShapeDtypeStruct((n // tm, wide, tm), BF16),
                   jax.ShapeDtypeStruct((n, wide), BF16),
                   jax.ShapeDtypeStruct((n // tm, vt_rows, tm), BF16)],
        compiler_params=pltpu.CompilerParams(
            dimension_semantics=("parallel",), vmem_limit_bytes=VMEM_LIMIT),
        name="proj_b",
    )(x2, pos_row, *consts)


def _attn_a_kernel(q_ref, k_ref, v_ref, bias_ref, o_ref,
                   qf, kf, vf, acc, den, m0, m1, onat, *, seq, pad):
    lane = lax.broadcasted_iota(jnp.int32, (1, LANES), 1)
    first_head = lane < HALF
    qf[...] = q_ref[...].astype(F32)
    zeros = jnp.zeros((pad, LANES), F32)
    kf[pl.ds(0, pad), :] = zeros
    vf[pl.ds(0, pad), :] = zeros
    kf[pl.ds(pad, seq), :] = k_ref[...].astype(F32)
    vf[pl.ds(pad, seq), :] = v_ref[...].astype(F32)

    cls_rows = seq // CLASSES

    for bi, (window, dil) in enumerate(DILATED_BRANCHES):
        assert window // dil == WIN
        nchunk = max(CLASSES // dil, 1)
        clen = WIN // nchunk
        stride = max(dil // CLASSES, 1)
        step = stride * clen
        nblk = seq // (dil * WIN)
        ublk = min(GROUP_A, nblk)
        ncls = GROUP_A // ublk

        def run(start, clen=clen, stride=stride):
            return pl.ds(start, clen, stride=stride) if stride > 1 else pl.ds(start, clen)

        def gather(ref, starts, run=run):
            return jnp.concatenate([ref[run(st), :] for st in starts], axis=0)

        def group(r0, g, bi=bi, dil=dil, nchunk=nchunk, clen=clen, stride=stride, step=step,
                  ublk=ublk, ncls=ncls, run=run, gather=gather):
            plans = []
            for cc in range(ncls):
                rd = r0 + cc
                if stride == 1:
                    bases = [pl.multiple_of((dil * c + rd) * cls_rows + step * ublk * g, 8)
                             for c in range(nchunk)]
                else:
                    bases = [(rd % CLASSES) * cls_rows + rd // CLASSES + step * ublk * g]
                for u in range(ublk):
                    plans.append((bases, u))

            def scores(bases, u):
                qstarts = [b0 + step * u for b0 in bases]
                kstarts = [pad + b0 + step * (u - 1) for b0 in bases]
                q = gather(qf, qstarts)
                qq = jnp.concatenate([jnp.where(first_head, q, 0.0),
                                      jnp.where(first_head, 0.0, q)], axis=0).astype(BF16)
                kk = jnp.concatenate([gather(kf, kstarts),
                                      gather(kf, [k0 + step for k0 in kstarts])],
                                     axis=0).astype(BF16)
                vv = jnp.concatenate([gather(vf, kstarts),
                                      gather(vf, [k0 + step for k0 in kstarts])],
                                     axis=0).astype(BF16)
                first_blk = jnp.where(g == 0, 1, 0) if u == 0 else 0
                s = _nt_dot(qq, kk) + bias_ref[bi, first_blk]
                return qstarts, s, jnp.concatenate([vv, jnp.ones_like(vv)], axis=1)

            def update(qstarts, s, vext):
                ps, alphas = [], []
                for hh, m_s in enumerate((m0, m1)):
                    sh = s[hh * WIN:(hh + 1) * WIN]
                    m_blk = jnp.max(sh, axis=-1, keepdims=True)
                    if bi == 0:
                        m_new = jnp.broadcast_to(m_blk, (WIN, LANES))
                    else:
                        m_old = gather(m_s, qstarts)
                        m_new = jnp.maximum(m_old, m_blk)
                        alphas.append(jnp.exp2(m_old - m_new))
                    ps.append(jnp.exp2(sh - jnp.concatenate([m_new, m_new], axis=1)).astype(BF16))
                    for c, st in enumerate(qstarts):
                        m_s[run(st), :] = m_new[c * clen:(c + 1) * clen]
                res = _dot(jnp.concatenate(ps, axis=0), vext)
                acc_new = jnp.where(first_head, res[:WIN, :LANES], res[WIN:, :LANES])
                den_new = jnp.where(first_head, res[:WIN, LANES:], res[WIN:, LANES:])
                if bi > 0:
                    alpha = jnp.where(first_head, alphas[0], alphas[1])
                    acc_new = alpha * gather(acc, qstarts) + acc_new
                    den_new = alpha * gather(den, qstarts) + den_new
                for c, st in enumerate(qstarts):
                    acc[run(st), :] = acc_new[c * clen:(c + 1) * clen]
                    den[run(st), :] = den_new[c * clen:(c + 1) * clen]

            pending = scores(*plans[0])
            for nxt in plans[1:]:
                ahead = scores(*nxt)
                update(*pending)
                pending = ahead
            update(*pending)

        def class_body(ci, carry, ngrp=nblk // ublk, ncls=ncls, group=group):
            def grp_body(g, c):
                group(ci * ncls, g)
                return c
            return lax.fori_loop(0, ngrp, grp_body, carry)

        lax.fori_loop(0, dil // ncls, class_body, 0)

    for r in range(CLASSES):
        rows = pl.ds(r * cls_rows, cls_rows)
        onat[pl.ds(r, cls_rows, stride=CLASSES), :] = acc[rows, :] / den[rows, :]
    o_ref[...] = onat[...].astype(o_ref.dtype)


def _attn_a_call(qa, ka, va, bias):
    b, s, _ = qa.shape
    pad = WIN * DILATED_BRANCHES[-1][1] // CLASSES
    blk = pl.BlockSpec((None, s, LANES), lambda bb, p: (bb, 0, p))
    bias_spec = pl.BlockSpec((None,) + bias.shape[1:], lambda bb, p: (p, 0, 0, 0, 0))
    rows = pltpu.VMEM((s, LANES), F32)
    padded = pltpu.VMEM((pad + s, LANES), F32)
    return pl.pallas_call(
        functools.partial(_attn_a_kernel, seq=s, pad=pad),
        grid=(b, HEAD_PAIRS),
        in_specs=[blk, blk, blk, bias_spec],
        out_specs=blk,
        out_shape=jax.ShapeDtypeStruct((b, s, WIDTH_A), BF16),
        scratch_shapes=[rows, padded, padded, rows, rows, rows, rows, rows],
        compiler_params=pltpu.CompilerParams(
            dimension_semantics=("parallel", "parallel"), vmem_limit_bytes=VMEM_LIMIT),
        name="attn_a",
    )(qa, ka, va, bias)


def _t5_causal_bucket(dist):
    dist = np.asarray(dist, dtype=np.int64)
    max_exact = REL_BUCKETS // 2
    safe = np.maximum(dist, 1).astype(np.float32)
    large = max_exact + (np.log(safe / max_exact) / math.log(REL_MAX_DIST / max_exact)
                         * (REL_BUCKETS - max_exact)).astype(np.int64)
    large = np.minimum(large, REL_BUCKETS - 1)
    return np.where(dist < max_exact, dist, large).astype(np.int32)


def _bias_bucket_tables():
    i = np.arange(WIN)[:, None]
    c = np.arange(2 * WIN)[None, :]
    sub = WIN + i - c
    tables = []
    for _, dil in DILATED_BRANCHES:
        bucket = _t5_causal_bucket(np.clip(sub, 0, WIN) * dil)
        bucket = np.where((sub >= 0) & (sub <= WIN), bucket, REL_BUCKETS)
        nchunk = max(CLASSES // dil, 1)
        clen = WIN // nchunk
        bucket = bucket.reshape(clen, nchunk, 2, clen, nchunk)
        tables.append(np.transpose(bucket, (1, 0, 2, 4, 3)).reshape(WIN, 2 * WIN))
    return np.stack(tables).astype(np.int32)


def _bias_kernel(bias_ref, bucket_ref, o_ref):
    bucket = bucket_ref[...]
    first_blk_cols = lax.broadcasted_iota(jnp.int32, bucket.shape, 1) >= WIN
    for h in range(N_HEADS_A):
        t = jnp.full(bucket.shape, NEG, F32)
        for bkt in range(REL_BUCKETS):
            t = jnp.where(bucket == bkt, bias_ref[bkt, h], t)
        rows = slice((h % 2) * WIN, (h % 2 + 1) * WIN)
        o_ref[h // 2, 0, rows, :] = t
        o_ref[h // 2, 1, rows, :] = jnp.where(first_blk_cols, t, NEG)


def _attn_a_bias(rel_bias):
    buckets = jnp.asarray(_bias_bucket_tables())
    nbr = len(DILATED_BRANCHES)
    return pl.pallas_call(
        _bias_kernel,
        grid=(nbr,),
        in_specs=[pl.BlockSpec(memory_space=pltpu.SMEM),
                  pl.BlockSpec((None, WIN, 2 * WIN), lambda i: (i, 0, 0))],
        out_specs=pl.BlockSpec((HEAD_PAIRS, None, 2, 2 * WIN, 2 * WIN), lambda i: (0, i, 0, 0, 0)),
        out_shape=jax.ShapeDtypeStruct((HEAD_PAIRS, nbr, 2, 2 * WIN, 2 * WIN), F32),
        compiler_params=pltpu.CompilerParams(dimension_semantics=("parallel",)),
        name="bias_a",
    )(rel_bias.astype(F32) * LOG2E, buckets)


def _attn_b_kernel(qt_ref, k_ref, vt_ref, o_ref, acc0, acc1, m0, m1, s_even, s_odd, mb_even, mb_odd):
    tq = TQ_B
    nq = qt_ref.shape[0]
    heads = ((acc0, m0), (acc1, m1))
    bufs = ((s_even, mb_even), (s_odd, mb_odd))

    hq = tq // 2

    def causal(shape):
        return (lax.broadcasted_iota(jnp.int32, shape, 0)
                <= lax.broadcasted_iota(jnp.int32, shape, 1))

    def step(fold=None, make=None):
        for hh, (acc, m_s) in enumerate(heads):
            sl = slice(hh * LANES, (hh + 1) * LANES)
            if make is None:
                mats = [lambda: None, lambda: None]
            elif make[0] == "full":
                _, qi_a, kb_a, (s_a, mb_a) = make
                k_blk = k_ref[pl.ds(pl.multiple_of(kb_a * tq, tq), tq), sl]

                def mat(part, qi_a=qi_a, s_a=s_a, mb_a=mb_a, k_blk=k_blk):
                    half = slice(part * hq, (part + 1) * hq)
                    st = _dot(k_blk, qt_ref[qi_a, hh, :, half])
                    s_a[hh, :, half] = st
                    mb_a[hh, :, half] = jnp.max(st, axis=0, keepdims=True)
                mats = [functools.partial(mat, 0), functools.partial(mat, 1)]
            else:
                _, qi_a, s_a = make
                k0 = qi_a * tq

                def top_mat(qi_a=qi_a, s_a=s_a, k0=k0):
                    s_a[hh, :hq, :] = _dot(k_ref[pl.ds(k0, hq), sl], qt_ref[qi_a, hh])
                def bot_mat(qi_a=qi_a, s_a=s_a, k0=k0):
                    s_a[hh, hq:, :hq] = _dot(k_ref[pl.ds(k0 + hq, hq), sl], qt_ref[qi_a, hh, :, hq:])
                mats = [top_mat, bot_mat]
            if fold is None:
                for mat_piece in mats:
                    mat_piece()
                continue

            m_old = m_s[...]
            if fold[0] == "full":
                _, kb, (s_c, mb_c) = fold
                m_new = jnp.maximum(m_old, mb_c[hh])
                m_s[...] = m_new
                p0 = jnp.exp2(s_c[hh, :hq, :] - m_new).astype(BF16)
                mats[0]()
                p1 = jnp.exp2(s_c[hh, hq:, :] - m_new).astype(BF16)
                mats[1]()
                pv = _dot(vt_ref[kb, hh], jnp.concatenate([p0, p1], axis=0))
            else:
                _, kb, s_c = fold
                top = jnp.where(causal((hq, tq)), s_c[hh, :hq, :], NEG)
                bot = jnp.where(causal((hq, hq)), s_c[hh, hq:, :hq], NEG)
                m_top = jnp.max(top, axis=0, keepdims=True)
                m_bot = jnp.max(jnp.concatenate([jnp.full((hq, hq), NEG, F32), bot], axis=1),
                                axis=0, keepdims=True)
                m_new = jnp.maximum(m_old, jnp.maximum(m_top, m_bot))
                m_s[...] = m_new
                p0 = jnp.exp2(top - m_new).astype(BF16)
                mats[0]()
                p1 = jnp.exp2(bot - m_s[:, hq:]).astype(BF16)
                mats[1]()
                pv = _dot(vt_ref[kb, hh, :, :hq], p0) + jnp.concatenate(
                    [jnp.zeros((VT_ROWS, hq), F32), _dot(vt_ref[kb, hh, :, hq:], p1)], axis=1)
            acc[...] = jnp.exp2(m_old - m_new) * acc[...] + pv

    step(make=("diag", 0, bufs[0][0]))
    step0 = 0
    for qi in range(nq):
        cur, nxt = bufs[step0 % 2], bufs[(step0 + 1) % 2]
        for acc, m_s in heads:
            acc[...] = jnp.zeros_like(acc)
            m_s[...] = jnp.full(m_s.shape, NEG, F32)

        def pair_body(i, c, qi=qi, cur=cur, nxt=nxt):
            kb = 2 * i
            step(("full", kb, cur), ("full", qi, kb + 1, nxt))
            step(("full", kb + 1, nxt), ("full", qi, kb + 2, cur))
            return c

        npairs = qi // 2
        peel = qi % 2 == 0 and npairs > 0
        lax.fori_loop(0, npairs - 1 if peel else npairs, pair_body, 0)
        if peel:
            step(("full", qi - 2, cur), ("full", qi, qi - 1, nxt))
            step(("full", qi - 1, nxt), ("diag", qi, cur[0]))
        last, other = cur, nxt
        if qi % 2 == 1:
            step(("full", qi - 1, cur), ("diag", qi, nxt[0]))
            last, other = nxt, cur
        step(("diag", qi, last[0]), ("full", qi + 1, 0, other) if qi + 1 < nq else None)

        outs = []
        for acc, _ in heads:
            a = acc[...]
            outs.append(a[:V_HEAD_DIM] / a[V_HEAD_DIM:V_HEAD_DIM + 1])
        o_ref[pl.ds(qi * tq, tq), :] = jnp.concatenate(outs, axis=0).T.astype(o_ref.dtype)
        step0 += qi + 1


def _attn_b_call(qbt, kb, vbt):
    b, s, _ = kb.shape
    tq = TQ_B
    nkb = s // tq
    qbt = qbt.reshape(b, nkb, HEAD_PAIRS, 2, LANES, tq)
    vbt = vbt.reshape(b, nkb, HEAD_PAIRS, 2, VT_ROWS, tq)
    return pl.pallas_call(
        _attn_b_kernel,
        grid=(b, HEAD_PAIRS),
        in_specs=[pl.BlockSpec((None, nkb, None, 2, LANES, tq), lambda bb, p: (bb, 0, p, 0, 0, 0)),
                  pl.BlockSpec((None, s, 2 * LANES), lambda bb, p: (bb, 0, p)),
                  pl.BlockSpec((None, nkb, None, 2, VT_ROWS, tq),
                               lambda bb, p: (bb, 0, p, 0, 0, 0))],
        out_specs=pl.BlockSpec((None, s, LANES), lambda bb, p: (bb, 0, p)),
        out_shape=jax.ShapeDtypeStruct((b, s, WIDTH_B), BF16),
        scratch_shapes=[pltpu.VMEM((VT_ROWS, tq), F32)] * 2 + [pltpu.VMEM((1, tq), F32)] * 2
        + [pltpu.VMEM((2, tq, tq), F32)] * 2 + [pltpu.VMEM((2, 1, tq), F32)] * 2,
        compiler_params=pltpu.CompilerParams(
            dimension_semantics=("parallel", "parallel"), vmem_limit_bytes=VMEM_LIMIT),
        name="attn_b",
    )(qbt, kb, vbt)


def _ffn_kernel(x_ref, a_ref, b_ref, woa_ref, wob_ref, g_ref, w1_ref, w2_ref, o_ref):
    h = x_ref[...] + _dot(a_ref[...], woa_ref[...]) + _dot(b_ref[...], wob_ref[...])
    hn = (h * lax.rsqrt(jnp.mean(h * h, axis=-1, keepdims=True) + EPS) * g_ref[...]).astype(BF16)
    mlp = None
    for c in range(D_FF // FF_CHUNK):
        sl = slice(c * FF_CHUNK, (c + 1) * FF_CHUNK)
        hid = jnp.square(jnp.maximum(_dot(hn, w1_ref[:, sl]), 0.0)).astype(BF16)
        d = _dot(hid, w2_ref[sl, :])
        mlp = d if mlp is None else mlp + d
    o_ref[...] = h + mlp


def _ffn_call(x2, a2, b2, woa, wob, g, w1, w2):
    n = x2.shape[0]
    tm = TM_FFN
    row = lambda i: (i, 0)
    consts = (woa, wob, g, w1, w2)
    return pl.pallas_call(
        _ffn_kernel,
        grid=(n // tm,),
        in_specs=[pl.BlockSpec((tm, D_MODEL), row), pl.BlockSpec((tm, WIDTH_A), row),
                  pl.BlockSpec((tm, WIDTH_B), row)] + [_const_spec(c.shape) for c in consts],
        out_specs=pl.BlockSpec((tm, D_MODEL), row),
        out_shape=jax.ShapeDtypeStruct((n, D_MODEL), F32),
        compiler_params=pltpu.CompilerParams(
            dimension_semantics=("parallel",), vmem_limit_bytes=VMEM_LIMIT),
        name="ffn",
    )(x2, a2, b2, *consts)


def _pad_heads(w, n_heads, width):
    k = w.shape[0]
    w = w.reshape(k, n_heads, width)
    return jnp.pad(w, ((0, 0), (0, 0), (0, LANES - width))).reshape(k, n_heads * LANES)


def _lane_row(g, offset=0):
    return jnp.pad(g.astype(F32), (offset, LANES - offset - g.shape[0]))[None, :]


def kernel(x, positions, norm_mix_g, w_in, qnorm_a_g, knorm_a_g, rel_bias, cq_norm_g, ckv_norm_g,
           w_uq, w_ukv, qnorm_b_g, knorm_b_g, w_o, norm_ffn_g, w_ff1, w_ff2):
    b, s, d = x.shape
    n = b * s
    c0 = 3 * WIDTH_A
    c1 = c0 + Q_LORA_RANK
    c2 = c1 + KV_LORA_RANK
    w_qkv = w_in[:, :c0].astype(BF16)
    w_cq = w_in[:, c0:c1].astype(BF16)
    w_ckv = w_in[:, c1:c2].astype(BF16)
    w_kr = jnp.pad(w_in[:, c2:], ((0, 0), (QK_NOPE_DIM, LANES - QK_HEAD_DIM_B))).astype(BF16)
    w_uqt = w_uq.T.astype(BF16)
    w_ukv3 = w_ukv.reshape(KV_LORA_RANK, N_HEADS_B, QK_NOPE_DIM + V_HEAD_DIM)
    w_uk_p = _pad_heads(w_ukv3[:, :, :QK_NOPE_DIM].reshape(KV_LORA_RANK, -1),
                        N_HEADS_B, QK_NOPE_DIM).astype(BF16)
    ones_rows = VT_ROWS - V_HEAD_DIM
    w_uvt = jnp.transpose(w_ukv3[:, :, QK_NOPE_DIM:], (1, 2, 0))
    w_uvt = jnp.pad(w_uvt, ((0, 0), (0, ones_rows), (0, 0)))
    w_uvt = w_uvt.reshape(N_HEADS_B * VT_ROWS, KV_LORA_RANK).astype(BF16)
    inv_freq = 1.0 / (ROPE_THETA ** (jnp.arange(0, QK_ROPE_DIM, 2, dtype=F32) / QK_ROPE_DIM))
    ones_col = jnp.tile(jnp.concatenate([jnp.zeros((V_HEAD_DIM,), F32), jnp.ones((ones_rows,), F32)]),
                        N_HEADS_B)[:, None]
    g_mix = norm_mix_g.astype(F32)[None, :]
    head_sum = jnp.asarray(np.kron(np.eye(N_HEADS_A), np.ones((HEAD_DIM_A, HEAD_DIM_A))), BF16)
    consts_a = (
        g_mix, w_qkv, head_sum,
        jnp.tile(qnorm_a_g.astype(F32) * SCALE_A, N_HEADS_A)[None, :],
        jnp.tile(knorm_a_g.astype(F32), N_HEADS_A)[None, :],
    )
    tm = TM_PROJ
    gq_col = (qnorm_b_g.astype(F32) * SCALE_B)[:, None]
    consts_b = (
        g_mix, w_cq, w_ckv, w_kr,
        cq_norm_g.astype(F32)[None, :], ckv_norm_g.astype(F32)[None, :],
        w_uqt, w_uk_p, w_uvt,
        jnp.broadcast_to(gq_col, (QK_HEAD_DIM_B, tm)), _lane_row(knorm_b_g[:QK_NOPE_DIM]),
        _lane_row(knorm_b_g[QK_NOPE_DIM:], QK_NOPE_DIM),
        jnp.broadcast_to(inv_freq[:, None], (QK_ROPE_DIM // 2, tm)),
        ones_col,
    )
    x2 = x.reshape(n, d)
    qa, ka, va = _proj_a_call(x, consts_a)
    qbt, kb, vbt = _proj_b_call(x2, positions.reshape(n // tm, 1, tm), consts_b)

    out_a = _attn_a_call(qa, ka, va, _attn_a_bias(rel_bias))
    out_b = _attn_b_call(qbt, kb.reshape(b, s, -1), vbt)

    out = _ffn_call(x.reshape(n, d), out_a.reshape(n, -1), out_b.reshape(n, -1),
                    w_o[:WIDTH_A].astype(BF16), w_o[WIDTH_A:].astype(BF16),
                    norm_ffn_g.astype(F32)[None, :], w_ff1.astype(BF16), w_ff2.astype(BF16))
    return out.reshape(b, s, d)
```

```python
import functools
import math

import numpy as np
import jax
import jax.numpy as jnp
from jax import lax
from jax.experimental import pallas as pl
from jax.experimental.pallas import tpu as pltpu

F32 = jnp.float32
BF16 = jnp.bfloat16

D_MODEL = 1024
HEAD_DIM_A = 64
N_HEADS_A = 8
WIDTH_A = N_HEADS_A * HEAD_DIM_A
DILATED_BRANCHES = ((128, 1), (512, 4), (2048, 16))
N_HEADS_B = 8
QK_NOPE_DIM = 64
QK_ROPE_DIM = 32
V_HEAD_DIM = 64
QK_HEAD_DIM_B = QK_NOPE_DIM + QK_ROPE_DIM
Q_LORA_RANK = 768
KV_LORA_RANK = 256
WIDTH_B = N_HEADS_B * V_HEAD_DIM
ROPE_THETA = 10000.0
D_FF = 4 * D_MODEL
REL_BUCKETS = 32
REL_MAX_DIST = 2048
EPS = 1e-6

LANES = 128
HALF = LANES // 2
HEAD_PAIRS = N_HEADS_A // 2
WIN = 128
NEG = -1e30
VMEM_LIMIT = 56 * 1024 * 1024

TM_PROJ = 512
TM_PROJ_A = 1024
SUB_A = 256
TM_FFN = 512
TQ_B = 512
FF_CHUNK = 1024
GROUP_A = 32
CLASSES = 4

LOG2E = math.log2(math.e)
SCALE_A = LOG2E / math.sqrt(HEAD_DIM_A)
SCALE_B = LOG2E / math.sqrt(QK_HEAD_DIM_B)

BF16_ROWS = 16
VT_ROWS = V_HEAD_DIM + BF16_ROWS

assert HEAD_DIM_A == HALF and V_HEAD_DIM == HALF
assert TM_PROJ == TQ_B


def _nt_dot(a, b):
    return lax.dot_general(a, b, (((1,), (1,)), ((), ())), preferred_element_type=F32)


def _dot(a, b):
    return jnp.dot(a, b, preferred_element_type=F32)


def _const_spec(shape):
    nd = len(shape)
    return pl.BlockSpec(shape, lambda *_: (0,) * nd, pipeline_mode=pl.Buffered(1))


def _rms_bf16(x, g_row):
    return (x * lax.rsqrt(jnp.mean(x * x, axis=-1, keepdims=True) + EPS) * g_row).astype(BF16)


def _proj_a_kernel(x_ref, gmix_ref, wqkv_ref, hsum_ref, gqa_ref, gka_ref, qa_ref, ka_ref, va_ref,
                   xs_ref):
    tm, d = x_ref.shape

    def head_sumsq(y):
        return _dot((y * y).astype(BF16), hsum_ref[...])

    sub = SUB_A // CLASSES
    for h in range(tm // SUB_A):
        toks = slice(h * SUB_A, (h + 1) * SUB_A)
        for c in range(d // LANES):
            xs_ref[c, toks, :] = x_ref[toks, c * LANES:(c + 1) * LANES]
        x = jnp.concatenate(
            [jnp.concatenate([xs_ref[c, pl.ds(r + h * SUB_A, sub, stride=CLASSES), :]
                              for r in range(CLASSES)], axis=0)
             for c in range(d // LANES)], axis=1)
        xn = _rms_bf16(x, gmix_ref[...])
        q = _dot(xn, wqkv_ref[:, :WIDTH_A])
        k = _dot(xn, wqkv_ref[:, WIDTH_A:2 * WIDTH_A])
        ssq = head_sumsq(q)
        v = _dot(xn, wqkv_ref[:, 2 * WIDTH_A:]).astype(BF16)
        ssk = head_sumsq(k)
        q = (q * lax.rsqrt(ssq / HEAD_DIM_A + EPS) * gqa_ref[...]).astype(BF16)
        k = (k * lax.rsqrt(ssk / HEAD_DIM_A + EPS) * gka_ref[...]).astype(BF16)
        for r in range(CLASSES):
            src = slice(r * sub, (r + 1) * sub)
            dst = slice(h * sub, (h + 1) * sub)
            qa_ref[r, dst] = q[src]
            ka_ref[r, dst] = k[src]
            va_ref[r, dst] = v[src]


def _proj_a_call(x, consts):
    b, s, d = x.shape
    tm = TM_PROJ_A
    run = tm // CLASSES
    out = jax.ShapeDtypeStruct((b, CLASSES, s // CLASSES, WIDTH_A), BF16)
    outs = pl.pallas_call(
        _proj_a_kernel,
        grid=(b, s // tm),
        in_specs=[pl.BlockSpec((None, tm, d), lambda bb, i: (bb, i, 0))]
        + [_const_spec(c.shape) for c in consts],
        out_specs=[pl.BlockSpec((None, CLASSES, run, WIDTH_A), lambda bb, i: (bb, 0, i, 0))] * 3,
        out_shape=[out] * 3,
        scratch_shapes=[pltpu.VMEM((d // LANES, tm, LANES), F32)],
        compiler_params=pltpu.CompilerParams(
            dimension_semantics=("parallel", "parallel"), vmem_limit_bytes=VMEM_LIMIT),
        name="proj_a",
    )(x, *consts)
    return [o.reshape(b, s, WIDTH_A) for o in outs]


def _proj_b_kernel(x_ref, posr_ref, gmix_ref, wcq_ref, wckv_ref, wkr_ref,
                   gcq_ref, gckv_ref, wuqt_ref, wuk_ref, wuvt_ref,
                   gqbt_ref, gkbn_ref, gkbr_ref, invft_ref, ones_ref,
                   qt_ref, kb_ref, vt_ref):
    tm = x_ref.shape[0]
    half = QK_ROPE_DIM // 2
    r0, r1, r2 = QK_NOPE_DIM, QK_NOPE_DIM + half, QK_HEAD_DIM_B
    xn = _rms_bf16(x_ref[...], gmix_ref[...])
    cq = _dot(xn, wcq_ref[...])
    ckv = _dot(xn, wckv_ref[...])
    kr = _dot(xn, wkr_ref[...])
    ang_t = invft_ref[...] * posr_ref[...].astype(F32)
    cs_t = jnp.cos(ang_t)
    sn_t = jnp.sin(ang_t)

    ckvn = _rms_bf16(ckv, gckv_ref[...])
    kn = _dot(ckvn, wuk_ref[...])
    vt_ref[...] = (_nt_dot(wuvt_ref[...], ckvn) + ones_ref[...]).astype(BF16)

    cqn = _rms_bf16(cq, gcq_ref[...])
    qt = _nt_dot(wuqt_ref[...], cqn)

    zt = jnp.zeros((r0, tm), F32)
    zb = jnp.zeros((LANES - r2, tm), F32)
    cs = jnp.concatenate([zt, cs_t, cs_t, zb], axis=0).T
    sn = jnp.concatenate([zt, -sn_t, sn_t, zb], axis=0).T
    ss_kr = jnp.sum(kr * kr, axis=-1, keepdims=True)
    krg = kr * gkbr_ref[...]
    kr_rot = krg * cs + (pltpu.roll(krg, LANES - half, 1) + pltpu.roll(krg, half, 1)) * sn
    for h in range(N_HEADS_B):
        sl = slice(h * LANES, (h + 1) * LANES)
        kh = kn[:, sl]
        rk = lax.rsqrt((jnp.sum(kh * kh, axis=-1, keepdims=True) + ss_kr) / QK_HEAD_DIM_B + EPS)
        kb_ref[:, sl] = ((kh * gkbn_ref[...] + kr_rot) * rk).astype(BF16)

    zpad = jnp.zeros((LANES - r2, tm), BF16)
    for h in range(N_HEADS_B):
        qh = qt[h * r2:(h + 1) * r2]
        rq = lax.rsqrt(jnp.sum(qh * qh, axis=0, keepdims=True) / QK_HEAD_DIM_B + EPS)
        y = qh * rq * gqbt_ref[...]
        y1, y2 = y[r0:r1], y[r1:r2]
        out = jnp.concatenate([y[:r0], y1 * cs_t - y2 * sn_t, y2 * cs_t + y1 * sn_t], axis=0)
        qt_ref[h * LANES:(h + 1) * LANES, :] = jnp.concatenate([out.astype(BF16), zpad], axis=0)


def _proj_b_call(x2, pos_row, consts):
    n = x2.shape[0]
    tm = TM_PROJ
    row = lambda i: (i, 0)
    blk3 = lambda i: (i, 0, 0)
    wide = N_HEADS_B * LANES
    vt_rows = N_HEADS_B * VT_ROWS
    return pl.pallas_call(
        _proj_b_kernel,
        grid=(n // tm,),
        in_specs=[pl.BlockSpec((tm, D_MODEL), row),
                  pl.BlockSpec((None, 1, tm), blk3)] + [_const_spec(c.shape) for c in consts],
        out_specs=[pl.BlockSpec((None, wide, tm), blk3), pl.BlockSpec((tm, wide), row),
                   pl.BlockSpec((None, vt_rows, tm), blk3)],
        out_shape=[jax.ShapeDtypeStruct((n // tm, wide, tm), BF16),
                   jax.ShapeDtypeStruct((n, wide), BF16),
                   jax.ShapeDtypeStruct((n // tm, vt_rows, tm), BF16)],
        compiler_params=pltpu.CompilerParams(
            dimension_semantics=("parallel",), vmem_limit_bytes=VMEM_LIMIT),
        name="proj_b",
    )(x2, pos_row, *consts)


def _attn_a_kernel(q_ref, k_ref, v_ref, bias_ref, o_ref,
                   qf, kf, vf, acc, den, m0, m1, onat, *, seq, pad):
    lane = lax.broadcasted_iota(jnp.int32, (1, LANES), 1)
    first_head = lane < HALF
    qf[...] = q_ref[...].astype(F32)
    zeros = jnp.zeros((pad, LANES), F32)
    kf[pl.ds(0, pad), :] = zeros
    vf[pl.ds(0, pad), :] = zeros
    kf[pl.ds(pad, seq), :] = k_ref[...].astype(F32)
    vf[pl.ds(pad, seq), :] = v_ref[...].astype(F32)

    cls_rows = seq // CLASSES

    for bi, (window, dil) in enumerate(DILATED_BRANCHES):
        assert window // dil == WIN
        nchunk = max(CLASSES // dil, 1)
        clen = WIN // nchunk
        stride = max(dil // CLASSES, 1)
        step = stride * clen
        nblk = seq // (dil * WIN)
        ublk = min(GROUP_A, nblk)
        ncls = GROUP_A // ublk

        def run(start, clen=clen, stride=stride):
            return pl.ds(start, clen, stride=stride) if stride > 1 else pl.ds(start, clen)

        def gather(ref, starts, run=run):
            return jnp.concatenate([ref[run(st), :] for st in starts], axis=0)

        def group(r0, g, bi=bi, dil=dil, nchunk=nchunk, clen=clen, stride=stride, step=step,
                  ublk=ublk, ncls=ncls, run=run, gather=gather):
            plans = []
            for cc in range(ncls):
                rd = r0 + cc
                if stride == 1:
                    bases = [pl.multiple_of((dil * c + rd) * cls_rows + step * ublk * g, 8)
                             for c in range(nchunk)]
                else:
                    bases = [(rd % CLASSES) * cls_rows + rd // CLASSES + step * ublk * g]
                for u in range(ublk):
                    plans.append((bases, u))

            def scores(bases, u):
                qstarts = [b0 + step * u for b0 in bases]
                if stride == 1:
                    rows16 = [pl.multiple_of(st, BF16_ROWS) for st in qstarts]
                    prev16 = [pl.multiple_of(jnp.maximum(st - step, 0), BF16_ROWS) for st in qstarts]
                    q = gather(q_ref, rows16)
                    zero = jnp.zeros_like(q)
                    qq = jnp.concatenate([jnp.where(first_head, q, zero),
                                          jnp.where(first_head, zero, q)], axis=0)
                    kk = jnp.concatenate([gather(k_ref, prev16), gather(k_ref, rows16)], axis=0)
                    vv = jnp.concatenate([gather(v_ref, prev16), gather(v_ref, rows16)], axis=0)
                else:
                    kstarts = [pad + b0 + step * (u - 1) for b0 in bases]
                    q = gather(qf, qstarts)
                    qq = jnp.concatenate([jnp.where(first_head, q, 0.0),
                                          jnp.where(first_head, 0.0, q)], axis=0).astype(BF16)
                    kk = jnp.concatenate([gather(kf, kstarts),
                                          gather(kf, [k0 + step for k0 in kstarts])],
                                         axis=0).astype(BF16)
                    vv = jnp.concatenate([gather(vf, kstarts),
                                          gather(vf, [k0 + step for k0 in kstarts])],
                                         axis=0).astype(BF16)
                first_blk = jnp.where(g == 0, 1, 0) if u == 0 else 0
                s = _nt_dot(qq, kk) + bias_ref[bi, first_blk]
                return qstarts, s, jnp.concatenate([vv, jnp.ones_like(vv)], axis=1)

            def update(qstarts, s, vext):
                ps, alphas = [], []
                for hh, m_s in enumerate((m0, m1)):
                    sh = s[hh * WIN:(hh + 1) * WIN]
                    m_blk = jnp.max(sh, axis=-1, keepdims=True)
                    if bi == 0:
                        m_new = jnp.broadcast_to(m_blk, (WIN, LANES))
                    else:
                        m_old = gather(m_s, qstarts)
                        m_new = jnp.maximum(m_old, m_blk)
                        alphas.append(jnp.exp2(m_old - m_new))
                    ps.append(jnp.exp2(sh - jnp.concatenate([m_new, m_new], axis=1)).astype(BF16))
                    for c, st in enumerate(qstarts):
                        m_s[run(st), :] = m_new[c * clen:(c + 1) * clen]
                res = _dot(jnp.concatenate(ps, axis=0), vext)
                acc_new = jnp.where(first_head, res[:WIN, :LANES], res[WIN:, :LANES])
                den_new = jnp.where(first_head, res[:WIN, LANES:], res[WIN:, LANES:])
                if bi > 0:
                    alpha = jnp.where(first_head, alphas[0], alphas[1])
                    acc_new = alpha * gather(acc, qstarts) + acc_new
                    den_new = alpha * gather(den, qstarts) + den_new
                for c, st in enumerate(qstarts):
                    acc[run(st), :] = acc_new[c * clen:(c + 1) * clen]
                    den[run(st), :] = den_new[c * clen:(c + 1) * clen]

            pending = scores(*plans[0])
            for nxt in plans[1:]:
                ahead = scores(*nxt)
                update(*pending)
                pending = ahead
            update(*pending)

        def class_body(ci, carry, ngrp=nblk // ublk, ncls=ncls, group=group):
            def grp_body(g, c):
                group(ci * ncls, g)
                return c
            return lax.fori_loop(0, ngrp, grp_body, carry)

        lax.fori_loop(0, dil // ncls, class_body, 0)

    for r in range(CLASSES):
        rows = pl.ds(r * cls_rows, cls_rows)
        onat[pl.ds(r, cls_rows, stride=CLASSES), :] = acc[rows, :] / den[rows, :]
    o_ref[...] = onat[...].astype(o_ref.dtype)


def _attn_a_call(qa, ka, va, bias):
    b, s, _ = qa.shape
    pad = WIN * DILATED_BRANCHES[-1][1] // CLASSES
    blk = pl.BlockSpec((None, s, LANES), lambda bb, p: (bb, 0, p))
    bias_spec = pl.BlockSpec((None,) + bias.shape[1:], lambda bb, p: (p, 0, 0, 0, 0))
    rows = pltpu.VMEM((s, LANES), F32)
    padded = pltpu.VMEM((pad + s, LANES), F32)
    return pl.pallas_call(
        functools.partial(_attn_a_kernel, seq=s, pad=pad),
        grid=(b, HEAD_PAIRS),
        in_specs=[blk, blk, blk, bias_spec],
        out_specs=blk,
        out_shape=jax.ShapeDtypeStruct((b, s, WIDTH_A), BF16),
        scratch_shapes=[rows, padded, padded, rows, rows, rows, rows, rows],
        compiler_params=pltpu.CompilerParams(
            dimension_semantics=("parallel", "parallel"), vmem_limit_bytes=VMEM_LIMIT),
        name="attn_a",
    )(qa, ka, va, bias)


def _t5_causal_bucket(dist):
    dist = np.asarray(dist, dtype=np.int64)
    max_exact = REL_BUCKETS // 2
    safe = np.maximum(dist, 1).astype(np.float32)
    large = max_exact + (np.log(safe / max_exact) / math.log(REL_MAX_DIST / max_exact)
                         * (REL_BUCKETS - max_exact)).astype(np.int64)
    large = np.minimum(large, REL_BUCKETS - 1)
    return np.where(dist < max_exact, dist, large).astype(np.int32)


def _bias_bucket_tables():
    i = np.arange(WIN)[:, None]
    c = np.arange(2 * WIN)[None, :]
    sub = WIN + i - c
    tables = []
    for _, dil in DILATED_BRANCHES:
        bucket = _t5_causal_bucket(np.clip(sub, 0, WIN) * dil)
        bucket = np.where((sub >= 0) & (sub <= WIN), bucket, REL_BUCKETS)
        nchunk = max(CLASSES // dil, 1)
        clen = WIN // nchunk
        bucket = bucket.reshape(clen, nchunk, 2, clen, nchunk)
        tables.append(np.transpose(bucket, (1, 0, 2, 4, 3)).reshape(WIN, 2 * WIN))
    return np.stack(tables).astype(np.int32)


def _bias_kernel(bias_ref, bucket_ref, o_ref):
    bucket = bucket_ref[...]
    first_blk_cols = lax.broadcasted_iota(jnp.int32, bucket.shape, 1) >= WIN
    for h in range(N_HEADS_A):
        t = jnp.full(bucket.shape, NEG, F32)
        for bkt in range(REL_BUCKETS):
            t = jnp.where(bucket == bkt, bias_ref[bkt, h], t)
        rows = slice((h % 2) * WIN, (h % 2 + 1) * WIN)
        o_ref[h // 2, 0, rows, :] = t
        o_ref[h // 2, 1, rows, :] = jnp.where(first_blk_cols, t, NEG)


def _attn_a_bias(rel_bias):
    buckets = jnp.asarray(_bias_bucket_tables())
    nbr = len(DILATED_BRANCHES)
    return pl.pallas_call(
        _bias_kernel,
        grid=(nbr,),
        in_specs=[pl.BlockSpec(memory_space=pltpu.SMEM),
                  pl.BlockSpec((None, WIN, 2 * WIN), lambda i: (i, 0, 0))],
        out_specs=pl.BlockSpec((HEAD_PAIRS, None, 2, 2 * WIN, 2 * WIN), lambda i: (0, i, 0, 0, 0)),
        out_shape=jax.ShapeDtypeStruct((HEAD_PAIRS, nbr, 2, 2 * WIN, 2 * WIN), F32),
        compiler_params=pltpu.CompilerParams(dimension_semantics=("parallel",)),
        name="bias_a",
    )(rel_bias.astype(F32) * LOG2E, buckets)


def _attn_b_kernel(qt_ref, k_ref, vt_ref, o_ref, acc0, acc1, m0, m1, s_even, s_odd, mb_even, mb_odd):
    tq = TQ_B
    nq = qt_ref.shape[0]
    heads = ((acc0, m0), (acc1, m1))
    bufs = ((s_even, mb_even), (s_odd, mb_odd))

    hq = tq // 2

    def causal(shape):
        return (lax.broadcasted_iota(jnp.int32, shape, 0)
                <= lax.broadcasted_iota(jnp.int32, shape, 1))

    def step(fold=None, make=None):
        for hh, (acc, m_s) in enumerate(heads):
            sl = slice(hh * LANES, (hh + 1) * LANES)
            if make is None:
                mats = [lambda: None, lambda: None]
            elif make[0] == "full":
                _, qi_a, kb_a, (s_a, mb_a) = make
                k_blk = k_ref[pl.ds(pl.multiple_of(kb_a * tq, tq), tq), sl]

                def mat(part, qi_a=qi_a, s_a=s_a, mb_a=mb_a, k_blk=k_blk):
                    half = slice(part * hq, (part + 1) * hq)
                    st = _dot(k_blk, qt_ref[qi_a, hh, :, half])
                    s_a[hh, :, half] = st
                    mb_a[hh, :, half] = jnp.max(st, axis=0, keepdims=True)
                mats = [functools.partial(mat, 0), functools.partial(mat, 1)]
            else:
                _, qi_a, s_a = make
                k0 = qi_a * tq

                def top_mat(qi_a=qi_a, s_a=s_a, k0=k0):
                    s_a[hh, :hq, :] = _dot(k_ref[pl.ds(k0, hq), sl], qt_ref[qi_a, hh])
                def bot_mat(qi_a=qi_a, s_a=s_a, k0=k0):
                    s_a[hh, hq:, :hq] = _dot(k_ref[pl.ds(k0 + hq, hq), sl], qt_ref[qi_a, hh, :, hq:])
                mats = [top_mat, bot_mat]
            if fold is None:
                for mat_piece in mats:
                    mat_piece()
                continue

            m_old = m_s[...]
            if fold[0] == "full":
                _, kb, (s_c, mb_c) = fold
                m_new = jnp.maximum(m_old, mb_c[hh])
                m_s[...] = m_new
                p0 = jnp.exp2(s_c[hh, :hq, :] - m_new).astype(BF16)
                mats[0]()
                p1 = jnp.exp2(s_c[hh, hq:, :] - m_new).astype(BF16)
                mats[1]()
                pv = _dot(vt_ref[kb, hh], jnp.concatenate([p0, p1], axis=0))
            else:
                _, kb, s_c = fold
                top = jnp.where(causal((hq, tq)), s_c[hh, :hq, :], NEG)
                bot = jnp.where(causal((hq, hq)), s_c[hh, hq:, :hq], NEG)
                m_top = jnp.max(top, axis=0, keepdims=True)
                m_bot = jnp.max(jnp.concatenate([jnp.full((hq, hq), NEG, F32), bot], axis=1),
                                axis=0, keepdims=True)
                m_new = jnp.maximum(m_old, jnp.maximum(m_top, m_bot))
                m_s[...] = m_new
                p0 = jnp.exp2(top - m_new).astype(BF16)
                mats[0]()
                p1 = jnp.exp2(bot - m_s[:, hq:]).astype(BF16)
                mats[1]()
                pv = _dot(vt_ref[kb, hh, :, :hq], p0) + jnp.concatenate(
                    [jnp.zeros((VT_ROWS, hq), F32), _dot(vt_ref[kb, hh, :, hq:], p1)], axis=1)
            acc[...] = jnp.exp2(m_old - m_new) * acc[...] + pv

    step(make=("diag", 0, bufs[0][0]))
    step0 = 0
    for qi in range(nq):
        cur, nxt = bufs[step0 % 2], bufs[(step0 + 1) % 2]
        for acc, m_s in heads:
            acc[...] = jnp.zeros_like(acc)
            m_s[...] = jnp.full(m_s.shape, NEG, F32)

        def pair_body(i, c, qi=qi, cur=cur, nxt=nxt):
            kb = 2 * i
            step(("full", kb, cur), ("full", qi, kb + 1, nxt))
            step(("full", kb + 1, nxt), ("full", qi, kb + 2, cur))
            return c

        npairs = qi // 2
        peel = qi % 2 == 0 and npairs > 0
        lax.fori_loop(0, npairs - 1 if peel else npairs, pair_body, 0)
        if peel:
            step(("full", qi - 2, cur), ("full", qi, qi - 1, nxt))
            step(("full", qi - 1, nxt), ("diag", qi, cur[0]))
        last, other = cur, nxt
        if qi % 2 == 1:
            step(("full", qi - 1, cur), ("diag", qi, nxt[0]))
            last, other = nxt, cur
        step(("diag", qi, last[0]), ("full", qi + 1, 0, other) if qi + 1 < nq else None)

        outs = []
        for acc, _ in heads:
            a = acc[...]
            outs.append(a[:V_HEAD_DIM] / a[V_HEAD_DIM:V_HEAD_DIM + 1])
        o_ref[pl.ds(qi * tq, tq), :] = jnp.concatenate(outs, axis=0).T.astype(o_ref.dtype)
        step0 += qi + 1


def _attn_b_call(qbt, kb, vbt):
    b, s, _ = kb.shape
    tq = TQ_B
    nkb = s // tq
    qbt = qbt.reshape(b, nkb, HEAD_PAIRS, 2, LANES, tq)
    vbt = vbt.reshape(b, nkb, HEAD_PAIRS, 2, VT_ROWS, tq)
    return pl.pallas_call(
        _attn_b_kernel,
        grid=(b, HEAD_PAIRS),
        in_specs=[pl.BlockSpec((None, nkb, None, 2, LANES, tq), lambda bb, p: (bb, 0, p, 0, 0, 0)),
                  pl.BlockSpec((None, s, 2 * LANES), lambda bb, p: (bb, 0, p)),
                  pl.BlockSpec((None, nkb, None, 2, VT_ROWS, tq),
                               lambda bb, p: (bb, 0, p, 0, 0, 0))],
        out_specs=pl.BlockSpec((None, s, LANES), lambda bb, p: (bb, 0, p)),
        out_shape=jax.ShapeDtypeStruct((b, s, WIDTH_B), BF16),
        scratch_shapes=[pltpu.VMEM((VT_ROWS, tq), F32)] * 2 + [pltpu.VMEM((1, tq), F32)] * 2
        + [pltpu.VMEM((2, tq, tq), F32)] * 2 + [pltpu.VMEM((2, 1, tq), F32)] * 2,
        compiler_params=pltpu.CompilerParams(
            dimension_semantics=("parallel", "parallel"), vmem_limit_bytes=VMEM_LIMIT),
        name="attn_b",
    )(qbt, kb, vbt)


def _ffn_kernel(x_ref, a_ref, b_ref, woa_ref, wob_ref, g_ref, w1_ref, w2_ref, o_ref):
    h = x_ref[...] + _dot(a_ref[...], woa_ref[...]) + _dot(b_ref[...], wob_ref[...])
    hn = (h * lax.rsqrt(jnp.mean(h * h, axis=-1, keepdims=True) + EPS) * g_ref[...]).astype(BF16)
    mlp = None
    for c in range(D_FF // FF_CHUNK):
        sl = slice(c * FF_CHUNK, (c + 1) * FF_CHUNK)
        hid = jnp.square(jnp.maximum(_dot(hn, w1_ref[:, sl]), 0.0)).astype(BF16)
        d = _dot(hid, w2_ref[sl, :])
        mlp = d if mlp is None else mlp + d
    o_ref[...] = h + mlp


def _ffn_call(x2, a2, b2, woa, wob, g, w1, w2):
    n = x2.shape[0]
    tm = TM_FFN
    row = lambda i: (i, 0)
    consts = (woa, wob, g, w1, w2)
    return pl.pallas_call(
        _ffn_kernel,
        grid=(n // tm,),
        in_specs=[pl.BlockSpec((tm, D_MODEL), row), pl.BlockSpec((tm, WIDTH_A), row),
                  pl.BlockSpec((tm, WIDTH_B), row)] + [_const_spec(c.shape) for c in consts],
        out_specs=pl.BlockSpec((tm, D_MODEL), row),
        out_shape=jax.ShapeDtypeStruct((n, D_MODEL), F32),
        compiler_params=pltpu.CompilerParams(
            dimension_semantics=("parallel",), vmem_limit_bytes=VMEM_LIMIT),
        name="ffn",
    )(x2, a2, b2, *consts)


def _pad_heads(w, n_heads, width):
    k = w.shape[0]
    w = w.reshape(k, n_heads, width)
    return jnp.pad(w, ((0, 0), (0, 0), (0, LANES - width))).reshape(k, n_heads * LANES)


def _lane_row(g, offset=0):
    return jnp.pad(g.astype(F32), (offset, LANES - offset - g.shape[0]))[None, :]


def kernel(x, positions, norm_mix_g, w_in, qnorm_a_g, knorm_a_g, rel_bias, cq_norm_g, ckv_norm_g,
           w_uq, w_ukv, qnorm_b_g, knorm_b_g, w_o, norm_ffn_g, w_ff1, w_ff2):
    b, s, d = x.shape
    n = b * s
    c0 = 3 * WIDTH_A
    c1 = c0 + Q_LORA_RANK
    c2 = c1 + KV_LORA_RANK
    w_qkv = w_in[:, :c0].astype(BF16)
    w_cq = w_in[:, c0:c1].astype(BF16)
    w_ckv = w_in[:, c1:c2].astype(BF16)
    w_kr = jnp.pad(w_in[:, c2:], ((0, 0), (QK_NOPE_DIM, LANES - QK_HEAD_DIM_B))).astype(BF16)
    w_uqt = w_uq.T.astype(BF16)
    w_ukv3 = w_ukv.reshape(KV_LORA_RANK, N_HEADS_B, QK_NOPE_DIM + V_HEAD_DIM)
    w_uk_p = _pad_heads(w_ukv3[:, :, :QK_NOPE_DIM].reshape(KV_LORA_RANK, -1),
                        N_HEADS_B, QK_NOPE_DIM).astype(BF16)
    ones_rows = VT_ROWS - V_HEAD_DIM
    w_uvt = jnp.transpose(w_ukv3[:, :, QK_NOPE_DIM:], (1, 2, 0))
    w_uvt = jnp.pad(w_uvt, ((0, 0), (0, ones_rows), (0, 0)))
    w_uvt = w_uvt.reshape(N_HEADS_B * VT_ROWS, KV_LORA_RANK).astype(BF16)
    inv_freq = 1.0 / (ROPE_THETA ** (jnp.arange(0, QK_ROPE_DIM, 2, dtype=F32) / QK_ROPE_DIM))
    ones_col = jnp.tile(jnp.concatenate([jnp.zeros((V_HEAD_DIM,), F32), jnp.ones((ones_rows,), F32)]),
                        N_HEADS_B)[:, None]
    g_mix = norm_mix_g.astype(F32)[None, :]
    head_sum = jnp.asarray(np.kron(np.eye(N_HEADS_A), np.ones((HEAD_DIM_A, HEAD_DIM_A))), BF16)
    consts_a = (
        g_mix, w_qkv, head_sum,
        jnp.tile(qnorm_a_g.astype(F32) * SCALE_A, N_HEADS_A)[None, :],
        jnp.tile(knorm_a_g.astype(F32), N_HEADS_A)[None, :],
    )
    tm = TM_PROJ
    gq_col = (qnorm_b_g.astype(F32) * SCALE_B)[:, None]
    consts_b = (
        g_mix, w_cq, w_ckv, w_kr,
        cq_norm_g.astype(F32)[None, :], ckv_norm_g.astype(F32)[None, :],
        w_uqt, w_uk_p, w_uvt,
        jnp.broadcast_to(gq_col, (QK_HEAD_DIM_B, tm)), _lane_row(knorm_b_g[:QK_NOPE_DIM]),
        _lane_row(knorm_b_g[QK_NOPE_DIM:], QK_NOPE_DIM),
        jnp.broadcast_to(inv_freq[:, None], (QK_ROPE_DIM // 2, tm)),
        ones_col,
    )
    x2 = x.reshape(n, d)
    qa, ka, va = _proj_a_call(x, consts_a)
    qbt, kb, vbt = _proj_b_call(x2, positions.reshape(n // tm, 1, tm), consts_b)

    out_a = _attn_a_call(qa, ka, va, _attn_a_bias(rel_bias))
    out_b = _attn_b_call(qbt, kb.reshape(b, s, -1), vbt)

    out = _ffn_call(x.reshape(n, d), out_a.reshape(n, -1), out_b.reshape(n, -1),
                    w_o[:WIDTH_A].astype(BF16), w_o[WIDTH_A:].astype(BF16),
                    norm_ffn_g.astype(F32)[None, :], w_ff1.astype(BF16), w_ff2.astype(BF16))
    return out.reshape(b, s, d)
```

```python
import functools
import math

import numpy as np
import jax
import jax.numpy as jnp
from jax import lax
from jax.experimental import pallas as pl
from jax.experimental.pallas import tpu as pltpu

F32 = jnp.float32
BF16 = jnp.bfloat16

D_MODEL = 1024
HEAD_DIM_A = 64
N_HEADS_A = 8
WIDTH_A = N_HEADS_A * HEAD_DIM_A
DILATED_BRANCHES = ((128, 1), (512, 4), (2048, 16))
N_HEADS_B = 8
QK_NOPE_DIM = 64
QK_ROPE_DIM = 32
V_HEAD_DIM = 64
QK_HEAD_DIM_B = QK_NOPE_DIM + QK_ROPE_DIM
Q_LORA_RANK = 768
KV_LORA_RANK = 256
WIDTH_B = N_HEADS_B * V_HEAD_DIM
ROPE_THETA = 10000.0
D_FF = 4 * D_MODEL
REL_BUCKETS = 32
REL_MAX_DIST = 2048
EPS = 1e-6

LANES = 128
HALF = LANES // 2
HEAD_PAIRS = N_HEADS_A // 2
WIN = 128
NEG = -1e30
VMEM_LIMIT = 56 * 1024 * 1024

TM_PROJ = 512
TM_PROJ_A = 1024
SUB_A = 512
TM_FFN = 512
TQ_B = 512
FF_CHUNK = 1024
GROUP_A = 32
CLASSES = 4

LOG2E = math.log2(math.e)
SCALE_A = LOG2E / math.sqrt(HEAD_DIM_A)
SCALE_B = LOG2E / math.sqrt(QK_HEAD_DIM_B)

BF16_ROWS = 16
VT_ROWS = V_HEAD_DIM + BF16_ROWS

assert HEAD_DIM_A == HALF and V_HEAD_DIM == HALF
assert TM_PROJ == TQ_B


def _nt_dot(a, b):
    return lax.dot_general(a, b, (((1,), (1,)), ((), ())), preferred_element_type=F32)


def _dot(a, b):
    return jnp.dot(a, b, preferred_element_type=F32)


def _const_spec(shape):
    nd = len(shape)
    return pl.BlockSpec(shape, lambda *_: (0,) * nd, pipeline_mode=pl.Buffered(1))


def _rms_bf16(x, g_row):
    return (x * lax.rsqrt(jnp.mean(x * x, axis=-1, keepdims=True) + EPS) * g_row).astype(BF16)


def _proj_a_kernel(x_ref, gmix_ref, wqkv_ref, hsum_ref, gqa_ref, gka_ref, qa_ref, ka_ref, va_ref,
                   xs_ref):
    tm, d = x_ref.shape

    def head_sumsq(y):
        return _dot((y * y).astype(BF16), hsum_ref[...])

    sub = SUB_A // CLASSES
    for h in range(tm // SUB_A):
        toks = slice(h * SUB_A, (h + 1) * SUB_A)
        for c in range(d // LANES):
            xs_ref[c, toks, :] = x_ref[toks, c * LANES:(c + 1) * LANES]
        x = jnp.concatenate(
            [jnp.concatenate([xs_ref[c, pl.ds(r + h * SUB_A, sub, stride=CLASSES), :]
                              for r in range(CLASSES)], axis=0)
             for c in range(d // LANES)], axis=1)
        xn = _rms_bf16(x, gmix_ref[...])
        q = _dot(xn, wqkv_ref[:, :WIDTH_A])
        k = _dot(xn, wqkv_ref[:, WIDTH_A:2 * WIDTH_A])
        ssq = head_sumsq(q)
        v = _dot(xn, wqkv_ref[:, 2 * WIDTH_A:]).astype(BF16)
        ssk = head_sumsq(k)
        q = (q * lax.rsqrt(ssq / HEAD_DIM_A + EPS) * gqa_ref[...]).astype(BF16)
        k = (k * lax.rsqrt(ssk / HEAD_DIM_A + EPS) * gka_ref[...]).astype(BF16)
        for r in range(CLASSES):
            src = slice(r * sub, (r + 1) * sub)
            dst = slice(h * sub, (h + 1) * sub)
            qa_ref[r, dst] = q[src]
            ka_ref[r, dst] = k[src]
            va_ref[r, dst] = v[src]


def _proj_a_call(x, consts):
    b, s, d = x.shape
    tm = TM_PROJ_A
    run = tm // CLASSES
    out = jax.ShapeDtypeStruct((b, CLASSES, s // CLASSES, WIDTH_A), BF16)
    outs = pl.pallas_call(
        _proj_a_kernel,
        grid=(b, s // tm),
        in_specs=[pl.BlockSpec((None, tm, d), lambda bb, i: (bb, i, 0))]
        + [_const_spec(c.shape) for c in consts],
        out_specs=[pl.BlockSpec((None, CLASSES, run, WIDTH_A), lambda bb, i: (bb, 0, i, 0))] * 3,
        out_shape=[out] * 3,
        scratch_shapes=[pltpu.VMEM((d // LANES, tm, LANES), F32)],
        compiler_params=pltpu.CompilerParams(
            dimension_semantics=("parallel", "parallel"), vmem_limit_bytes=VMEM_LIMIT),
        name="proj_a",
    )(x, *consts)
    return [o.reshape(b, s, WIDTH_A) for o in outs]


def _proj_b_kernel(x_ref, posr_ref, gmix_ref, wcq_ref, wckv_ref, wkr_ref,
                   gcq_ref, gckv_ref, wuqt_ref, wuk_ref, wuvt_ref,
                   gqbt_ref, gkbn_ref, gkbr_ref, invft_ref, ones_ref,
                   qt_ref, kb_ref, vt_ref):
    tm = x_ref.shape[0]
    half = QK_ROPE_DIM // 2
    r0, r1, r2 = QK_NOPE_DIM, QK_NOPE_DIM + half, QK_HEAD_DIM_B
    xn = _rms_bf16(x_ref[...], gmix_ref[...])
    cq = _dot(xn, wcq_ref[...])
    ckv = _dot(xn, wckv_ref[...])
    kr = _dot(xn, wkr_ref[...])
    ang_t = invft_ref[...] * posr_ref[...].astype(F32)
    cs_t = jnp.cos(ang_t)
    sn_t = jnp.sin(ang_t)

    ckvn = _rms_bf16(ckv, gckv_ref[...])
    kn = _dot(ckvn, wuk_ref[...])
    vt_ref[...] = (_nt_dot(wuvt_ref[...], ckvn) + ones_ref[...]).astype(BF16)

    cqn = _rms_bf16(cq, gcq_ref[...])
    qt = _nt_dot(wuqt_ref[...], cqn)

    zt = jnp.zeros((r0, tm), F32)
    zb = jnp.zeros((LANES - r2, tm), F32)
    cs = jnp.concatenate([zt, cs_t, cs_t, zb], axis=0).T
    sn = jnp.concatenate([zt, -sn_t, sn_t, zb], axis=0).T
    ss_kr = jnp.sum(kr * kr, axis=-1, keepdims=True)
    krg = kr * gkbr_ref[...]
    kr_rot = krg * cs + (pltpu.roll(krg, LANES - half, 1) + pltpu.roll(krg, half, 1)) * sn
    for h in range(N_HEADS_B):
        sl = slice(h * LANES, (h + 1) * LANES)
        kh = kn[:, sl]
        rk = lax.rsqrt((jnp.sum(kh * kh, axis=-1, keepdims=True) + ss_kr) / QK_HEAD_DIM_B + EPS)
        kb_ref[:, sl] = ((kh * gkbn_ref[...] + kr_rot) * rk).astype(BF16)

    zpad = jnp.zeros((LANES - r2, tm), BF16)
    for h in range(N_HEADS_B):
        qh = qt[h * r2:(h + 1) * r2]
        rq = lax.rsqrt(jnp.sum(qh * qh, axis=0, keepdims=True) / QK_HEAD_DIM_B + EPS)
        y = qh * rq * gqbt_ref[...]
        y1, y2 = y[r0:r1], y[r1:r2]
        out = jnp.concatenate([y[:r0], y1 * cs_t - y2 * sn_t, y2 * cs_t + y1 * sn_t], axis=0)
        qt_ref[h * LANES:(h + 1) * LANES, :] = jnp.concatenate([out.astype(BF16), zpad], axis=0)


def _proj_b_call(x2, pos_row, consts):
    n = x2.shape[0]
    tm = TM_PROJ
    row = lambda i: (i, 0)
    blk3 = lambda i: (i, 0, 0)
    wide = N_HEADS_B * LANES
    vt_rows = N_HEADS_B * VT_ROWS
    return pl.pallas_call(
        _proj_b_kernel,
        grid=(n // tm,),
        in_specs=[pl.BlockSpec((tm, D_MODEL), row),
                  pl.BlockSpec((None, 1, tm), blk3)] + [_const_spec(c.shape) for c in consts],
        out_specs=[pl.BlockSpec((None, wide, tm), blk3), pl.BlockSpec((tm, wide), row),
                   pl.BlockSpec((None, vt_rows, tm), blk3)],
        out_shape=[jax.ShapeDtypeStruct((n // tm, wide, tm), BF16),
                   jax.ShapeDtypeStruct((n, wide), BF16),
                   jax.ShapeDtypeStruct((n // tm, vt_rows, tm), BF16)],
        compiler_params=pltpu.CompilerParams(
            dimension_semantics=("parallel",), vmem_limit_bytes=VMEM_LIMIT),
        name="proj_b",
    )(x2, pos_row, *consts)


def _attn_a_kernel(q_ref, k_ref, v_ref, bias_ref, o_ref,
                   qf, kf, vf, acc, den, m0, m1, onat, *, seq, pad):
    lane = lax.broadcasted_iota(jnp.int32, (1, LANES), 1)
    first_head = lane < HALF
    qf[...] = q_ref[...].astype(F32)
    zeros = jnp.zeros((pad, LANES), F32)
    kf[pl.ds(0, pad), :] = zeros
    vf[pl.ds(0, pad), :] = zeros
    kf[pl.ds(pad, seq), :] = k_ref[...].astype(F32)
    vf[pl.ds(pad, seq), :] = v_ref[...].astype(F32)

    cls_rows = seq // CLASSES

    for bi, (window, dil) in enumerate(DILATED_BRANCHES):
        assert window // dil == WIN
        nchunk = max(CLASSES // dil, 1)
        clen = WIN // nchunk
        stride = max(dil // CLASSES, 1)
        step = stride * clen
        nblk = seq // (dil * WIN)
        ublk = min(GROUP_A, nblk)
        ncls = GROUP_A // ublk

        def run(start, clen=clen, stride=stride):
            return pl.ds(start, clen, stride=stride) if stride > 1 else pl.ds(start, clen)

        def gather(ref, starts, run=run):
            return jnp.concatenate([ref[run(st), :] for st in starts], axis=0)

        def group(r0, g, bi=bi, dil=dil, nchunk=nchunk, clen=clen, stride=stride, step=step,
                  ublk=ublk, ncls=ncls, run=run, gather=gather):
            plans = []
            for cc in range(ncls):
                rd = r0 + cc
                if stride == 1:
                    bases = [pl.multiple_of((dil * c + rd) * cls_rows + step * ublk * g, 8)
                             for c in range(nchunk)]
                else:
                    bases = [(rd % CLASSES) * cls_rows + rd // CLASSES + step * ublk * g]
                for u in range(ublk):
                    plans.append((bases, u))

            def scores(bases, u):
                qstarts = [b0 + step * u for b0 in bases]
                if stride == 1:
                    rows16 = [pl.multiple_of(st, BF16_ROWS) for st in qstarts]
                    prev16 = [pl.multiple_of(jnp.maximum(st - step, 0), BF16_ROWS) for st in qstarts]
                    q = gather(q_ref, rows16)
                    zero = jnp.zeros_like(q)
                    qq = jnp.concatenate([jnp.where(first_head, q, zero),
                                          jnp.where(first_head, zero, q)], axis=0)
                    kk = jnp.concatenate([gather(k_ref, prev16), gather(k_ref, rows16)], axis=0)
                    vv = jnp.concatenate([gather(v_ref, prev16), gather(v_ref, rows16)], axis=0)
                else:
                    kstarts = [pad + b0 + step * (u - 1) for b0 in bases]
                    q = gather(qf, qstarts)
                    qq = jnp.concatenate([jnp.where(first_head, q, 0.0),
                                          jnp.where(first_head, 0.0, q)], axis=0).astype(BF16)
                    kk = jnp.concatenate([gather(kf, kstarts),
                                          gather(kf, [k0 + step for k0 in kstarts])],
                                         axis=0).astype(BF16)
                    vv = jnp.concatenate([gather(vf, kstarts),
                                          gather(vf, [k0 + step for k0 in kstarts])],
                                         axis=0).astype(BF16)
                first_blk = jnp.where(g == 0, 1, 0) if u == 0 else 0
                s = _nt_dot(qq, kk) + bias_ref[bi, first_blk]
                return qstarts, s, jnp.concatenate([vv, jnp.ones_like(vv)], axis=1)

            def update(qstarts, s, vext):
                ps, alphas = [], []
                for hh, m_s in enumerate((m0, m1)):
                    sh = s[hh * WIN:(hh + 1) * WIN]
                    m_blk = jnp.max(sh, axis=-1, keepdims=True)
                    if bi == 0:
                        m_new = jnp.broadcast_to(m_blk, (WIN, LANES))
                    else:
                        m_old = gather(m_s, qstarts)
                        m_new = jnp.maximum(m_old, m_blk)
                        alphas.append(jnp.exp2(m_old - m_new))
                    ps.append(jnp.exp2(sh - jnp.concatenate([m_new, m_new], axis=1)).astype(BF16))
                    for c, st in enumerate(qstarts):
                        m_s[run(st), :] = m_new[c * clen:(c + 1) * clen]
                res = _dot(jnp.concatenate(ps, axis=0), vext)
                acc_new = jnp.where(first_head, res[:WIN, :LANES], res[WIN:, :LANES])
                den_new = jnp.where(first_head, res[:WIN, LANES:], res[WIN:, LANES:])
                if bi > 0:
                    alpha = jnp.where(first_head, alphas[0], alphas[1])
                    acc_new = alpha * gather(acc, qstarts) + acc_new
                    den_new = alpha * gather(den, qstarts) + den_new
                for c, st in enumerate(qstarts):
                    acc[run(st), :] = acc_new[c * clen:(c + 1) * clen]
                    den[run(st), :] = den_new[c * clen:(c + 1) * clen]

            pending = scores(*plans[0])
            for nxt in plans[1:]:
                ahead = scores(*nxt)
                update(*pending)
                pending = ahead
            update(*pending)

        def class_body(ci, carry, ngrp=nblk // ublk, ncls=ncls, group=group):
            def grp_body(g, c):
                group(ci * ncls, g)
                return c
            return lax.fori_loop(0, ngrp, grp_body, carry)

        lax.fori_loop(0, dil // ncls, class_body, 0)

    for r in range(CLASSES):
        rows = pl.ds(r * cls_rows, cls_rows)
        onat[pl.ds(r, cls_rows, stride=CLASSES), :] = acc[rows, :] / den[rows, :]
    o_ref[...] = onat[...].astype(o_ref.dtype)


def _attn_a_call(qa, ka, va, bias):
    b, s, _ = qa.shape
    pad = WIN * DILATED_BRANCHES[-1][1] // CLASSES
    blk = pl.BlockSpec((None, s, LANES), lambda bb, p: (bb, 0, p))
    bias_spec = pl.BlockSpec((None,) + bias.shape[1:], lambda bb, p: (p, 0, 0, 0, 0))
    rows = pltpu.VMEM((s, LANES), F32)
    padded = pltpu.VMEM((pad + s, LANES), F32)
    return pl.pallas_call(
        functools.partial(_attn_a_kernel, seq=s, pad=pad),
        grid=(b, HEAD_PAIRS),
        in_specs=[blk, blk, blk, bias_spec],
        out_specs=blk,
        out_shape=jax.ShapeDtypeStruct((b, s, WIDTH_A), BF16),
        scratch_shapes=[rows, padded, padded, rows, rows, rows, rows, rows],
        compiler_params=pltpu.CompilerParams(
            dimension_semantics=("parallel", "parallel"), vmem_limit_bytes=VMEM_LIMIT),
        name="attn_a",
    )(qa, ka, va, bias)


def _t5_causal_bucket(dist):
    dist = np.asarray(dist, dtype=np.int64)
    max_exact = REL_BUCKETS // 2
    safe = np.maximum(dist, 1).astype(np.float32)
    large = max_exact + (np.log(safe / max_exact) / math.log(REL_MAX_DIST / max_exact)
                         * (REL_BUCKETS - max_exact)).astype(np.int64)
    large = np.minimum(large, REL_BUCKETS - 1)
    return np.where(dist < max_exact, dist, large).astype(np.int32)


def _bias_bucket_tables():
    i = np.arange(WIN)[:, None]
    c = np.arange(2 * WIN)[None, :]
    sub = WIN + i - c
    tables = []
    for _, dil in DILATED_BRANCHES:
        bucket = _t5_causal_bucket(np.clip(sub, 0, WIN) * dil)
        bucket = np.where((sub >= 0) & (sub <= WIN), bucket, REL_BUCKETS)
        nchunk = max(CLASSES // dil, 1)
        clen = WIN // nchunk
        bucket = bucket.reshape(clen, nchunk, 2, clen, nchunk)
        tables.append(np.transpose(bucket, (1, 0, 2, 4, 3)).reshape(WIN, 2 * WIN))
    return np.stack(tables).astype(np.int32)


def _bias_kernel(bias_ref, bucket_ref, o_ref):
    bucket = bucket_ref[...]
    first_blk_cols = lax.broadcasted_iota(jnp.int32, bucket.shape, 1) >= WIN
    for h in range(N_HEADS_A):
        t = jnp.full(bucket.shape, NEG, F32)
        for bkt in range(REL_BUCKETS):
            t = jnp.where(bucket == bkt, bias_ref[bkt, h], t)
        rows = slice((h % 2) * WIN, (h % 2 + 1) * WIN)
        o_ref[h // 2, 0, rows, :] = t
        o_ref[h // 2, 1, rows, :] = jnp.where(first_blk_cols, t, NEG)


def _attn_a_bias(rel_bias):
    buckets = jnp.asarray(_bias_bucket_tables())
    nbr = len(DILATED_BRANCHES)
    return pl.pallas_call(
        _bias_kernel,
        grid=(nbr,),
        in_specs=[pl.BlockSpec(memory_space=pltpu.SMEM),
                  pl.BlockSpec((None, WIN, 2 * WIN), lambda i: (i, 0, 0))],
        out_specs=pl.BlockSpec((HEAD_PAIRS, None, 2, 2 * WIN, 2 * WIN), lambda i: (0, i, 0, 0, 0)),
        out_shape=jax.ShapeDtypeStruct((HEAD_PAIRS, nbr, 2, 2 * WIN, 2 * WIN), F32),
        compiler_params=pltpu.CompilerParams(dimension_semantics=("parallel",)),
        name="bias_a",
    )(rel_bias.astype(F32) * LOG2E, buckets)


def _attn_b_kernel(qt_ref, k_ref, vt_ref, o_ref, acc0, acc1, m0, m1, s_even, s_odd, mb_even, mb_odd):
    tq = TQ_B
    nq = qt_ref.shape[0]
    heads = ((acc0, m0), (acc1, m1))
    bufs = ((s_even, mb_even), (s_odd, mb_odd))

    hq = tq // 2

    def causal(shape):
        return (lax.broadcasted_iota(jnp.int32, shape, 0)
                <= lax.broadcasted_iota(jnp.int32, shape, 1))

    def step(fold=None, make=None):
        for hh, (acc, m_s) in enumerate(heads):
            sl = slice(hh * LANES, (hh + 1) * LANES)
            if make is None:
                mats = [lambda: None, lambda: None]
            elif make[0] == "full":
                _, qi_a, kb_a, (s_a, mb_a) = make
                k_blk = k_ref[pl.ds(pl.multiple_of(kb_a * tq, tq), tq), sl]

                def mat(part, qi_a=qi_a, s_a=s_a, mb_a=mb_a, k_blk=k_blk):
                    half = slice(part * hq, (part + 1) * hq)
                    st = _dot(k_blk, qt_ref[qi_a, hh, :, half])
                    s_a[hh, :, half] = st
                    mb_a[hh, :, half] = jnp.max(st, axis=0, keepdims=True)
                mats = [functools.partial(mat, 0), functools.partial(mat, 1)]
            else:
                _, qi_a, s_a = make
                k0 = qi_a * tq

                def top_mat(qi_a=qi_a, s_a=s_a, k0=k0):
                    s_a[hh, :hq, :] = _dot(k_ref[pl.ds(k0, hq), sl], qt_ref[qi_a, hh])
                def bot_mat(qi_a=qi_a, s_a=s_a, k0=k0):
                    s_a[hh, hq:, :hq] = _dot(k_ref[pl.ds(k0 + hq, hq), sl], qt_ref[qi_a, hh, :, hq:])
                mats = [top_mat, bot_mat]
            if fold is None:
                for mat_piece in mats:
                    mat_piece()
                continue

            m_old = m_s[...]
            if fold[0] == "full":
                _, kb, (s_c, mb_c) = fold
                m_new = jnp.maximum(m_old, mb_c[hh])
                m_s[...] = m_new
                p0 = jnp.exp2(s_c[hh, :hq, :] - m_new).astype(BF16)
                mats[0]()
                p1 = jnp.exp2(s_c[hh, hq:, :] - m_new).astype(BF16)
                mats[1]()
                pv = _dot(vt_ref[kb, hh], jnp.concatenate([p0, p1], axis=0))
            else:
                _, kb, s_c = fold
                top = jnp.where(causal((hq, tq)), s_c[hh, :hq, :], NEG)
                bot = jnp.where(causal((hq, hq)), s_c[hh, hq:, :hq], NEG)
                m_top = jnp.max(top, axis=0, keepdims=True)
                m_bot = jnp.max(jnp.concatenate([jnp.full((hq, hq), NEG, F32), bot], axis=1),
                                axis=0, keepdims=True)
                m_new = jnp.maximum(m_old, jnp.maximum(m_top, m_bot))
                m_s[...] = m_new
                p0 = jnp.exp2(top - m_new).astype(BF16)
                mats[0]()
                p1 = jnp.exp2(bot - m_s[:, hq:]).astype(BF16)
                mats[1]()
                pv = _dot(vt_ref[kb, hh, :, :hq], p0) + jnp.concatenate(
                    [jnp.zeros((VT_ROWS, hq), F32), _dot(vt_ref[kb, hh, :, hq:], p1)], axis=1)
            acc[...] = jnp.exp2(m_old - m_new) * acc[...] + pv

    step(make=("diag", 0, bufs[0][0]))
    step0 = 0
    for qi in range(nq):
        cur, nxt = bufs[step0 % 2], bufs[(step0 + 1) % 2]
        for acc, m_s in heads:
            acc[...] = jnp.zeros_like(acc)
            m_s[...] = jnp.full(m_s.shape, NEG, F32)

        def pair_body(i, c, qi=qi, cur=cur, nxt=nxt):
            kb = 2 * i
            step(("full", kb, cur), ("full", qi, kb + 1, nxt))
            step(("full", kb + 1, nxt), ("full", qi, kb + 2, cur))
            return c

        npairs = qi // 2
        peel = qi % 2 == 0 and npairs > 0
        lax.fori_loop(0, npairs - 1 if peel else npairs, pair_body, 0)
        if peel:
            step(("full", qi - 2, cur), ("full", qi, qi - 1, nxt))
            step(("full", qi - 1, nxt), ("diag", qi, cur[0]))
        last, other = cur, nxt
        if qi % 2 == 1:
            step(("full", qi - 1, cur), ("diag", qi, nxt[0]))
            last, other = nxt, cur
        step(("diag", qi, last[0]), ("full", qi + 1, 0, other) if qi + 1 < nq else None)

        outs = []
        for acc, _ in heads:
            a = acc[...]
            outs.append(a[:V_HEAD_DIM] / a[V_HEAD_DIM:V_HEAD_DIM + 1])
        o_ref[pl.ds(qi * tq, tq), :] = jnp.concatenate(outs, axis=0).T.astype(o_ref.dtype)
        step0 += qi + 1


def _attn_b_call(qbt, kb, vbt):
    b, s, _ = kb.shape
    tq = TQ_B
    nkb = s // tq
    qbt = qbt.reshape(b, nkb, HEAD_PAIRS, 2, LANES, tq)
    vbt = vbt.reshape(b, nkb, HEAD_PAIRS, 2, VT_ROWS, tq)
    return pl.pallas_call(
        _attn_b_kernel,
        grid=(b, HEAD_PAIRS),
        in_specs=[pl.BlockSpec((None, nkb, None, 2, LANES, tq), lambda bb, p: (bb, 0, p, 0, 0, 0)),
                  pl.BlockSpec((None, s, 2 * LANES), lambda bb, p: (bb, 0, p)),
                  pl.BlockSpec((None, nkb, None, 2, VT_ROWS, tq),
                               lambda bb, p: (bb, 0, p, 0, 0, 0))],
        out_specs=pl.BlockSpec((None, s, LANES), lambda bb, p: (bb, 0, p)),
        out_shape=jax.ShapeDtypeStruct((b, s, WIDTH_B), BF16),
        scratch_shapes=[pltpu.VMEM((VT_ROWS, tq), F32)] * 2 + [pltpu.VMEM((1, tq), F32)] * 2
        + [pltpu.VMEM((2, tq, tq), F32)] * 2 + [pltpu.VMEM((2, 1, tq), F32)] * 2,
        compiler_params=pltpu.CompilerParams(
            dimension_semantics=("parallel", "parallel"), vmem_limit_bytes=VMEM_LIMIT),
        name="attn_b",
    )(qbt, kb, vbt)


def _ffn_kernel(x_ref, a_ref, b_ref, woa_ref, wob_ref, g_ref, w1_ref, w2_ref, o_ref):
    h = x_ref[...] + _dot(a_ref[...], woa_ref[...]) + _dot(b_ref[...], wob_ref[...])
    hn = (h * lax.rsqrt(jnp.mean(h * h, axis=-1, keepdims=True) + EPS) * g_ref[...]).astype(BF16)
    mlp = None
    for c in range(D_FF // FF_CHUNK):
        sl = slice(c * FF_CHUNK, (c + 1) * FF_CHUNK)
        hid = jnp.square(jnp.maximum(_dot(hn, w1_ref[:, sl]), 0.0)).astype(BF16)
        d = _dot(hid, w2_ref[sl, :])
        mlp = d if mlp is None else mlp + d
    o_ref[...] = h + mlp


def _ffn_call(x2, a2, b2, woa, wob, g, w1, w2):
    n = x2.shape[0]
    tm = TM_FFN
    row = lambda i: (i, 0)
    consts = (woa, wob, g, w1, w2)
    return pl.pallas_call(
        _ffn_kernel,
        grid=(n // tm,),
        in_specs=[pl.BlockSpec((tm, D_MODEL), row), pl.BlockSpec((tm, WIDTH_A), row),
                  pl.BlockSpec((tm, WIDTH_B), row)] + [_const_spec(c.shape) for c in consts],
        out_specs=pl.BlockSpec((tm, D_MODEL), row),
        out_shape=jax.ShapeDtypeStruct((n, D_MODEL), F32),
        compiler_params=pltpu.CompilerParams(
            dimension_semantics=("parallel",), vmem_limit_bytes=VMEM_LIMIT),
        name="ffn",
    )(x2, a2, b2, *consts)


def _pad_heads(w, n_heads, width):
    k = w.shape[0]
    w = w.reshape(k, n_heads, width)
    return jnp.pad(w, ((0, 0), (0, 0), (0, LANES - width))).reshape(k, n_heads * LANES)


def _lane_row(g, offset=0):
    return jnp.pad(g.astype(F32), (offset, LANES - offset - g.shape[0]))[None, :]


def kernel(x, positions, norm_mix_g, w_in, qnorm_a_g, knorm_a_g, rel_bias, cq_norm_g, ckv_norm_g,
           w_uq, w_ukv, qnorm_b_g, knorm_b_g, w_o, norm_ffn_g, w_ff1, w_ff2):
    b, s, d = x.shape
    n = b * s
    c0 = 3 * WIDTH_A
    c1 = c0 + Q_LORA_RANK
    c2 = c1 + KV_LORA_RANK
    w_qkv = w_in[:, :c0].astype(BF16)
    w_cq = w_in[:, c0:c1].astype(BF16)
    w_ckv = w_in[:, c1:c2].astype(BF16)
    w_kr = jnp.pad(w_in[:, c2:], ((0, 0), (QK_NOPE_DIM, LANES - QK_HEAD_DIM_B))).astype(BF16)
    w_uqt = w_uq.T.astype(BF16)
    w_ukv3 = w_ukv.reshape(KV_LORA_RANK, N_HEADS_B, QK_NOPE_DIM + V_HEAD_DIM)
    w_uk_p = _pad_heads(w_ukv3[:, :, :QK_NOPE_DIM].reshape(KV_LORA_RANK, -1),
                        N_HEADS_B, QK_NOPE_DIM).astype(BF16)
    ones_rows = VT_ROWS - V_HEAD_DIM
    w_uvt = jnp.transpose(w_ukv3[:, :, QK_NOPE_DIM:], (1, 2, 0))
    w_uvt = jnp.pad(w_uvt, ((0, 0), (0, ones_rows), (0, 0)))
    w_uvt = w_uvt.reshape(N_HEADS_B * VT_ROWS, KV_LORA_RANK).astype(BF16)
    inv_freq = 1.0 / (ROPE_THETA ** (jnp.arange(0, QK_ROPE_DIM, 2, dtype=F32) / QK_ROPE_DIM))
    ones_col = jnp.tile(jnp.concatenate([jnp.zeros((V_HEAD_DIM,), F32), jnp.ones((ones_rows,), F32)]),
                        N_HEADS_B)[:, None]
    g_mix = norm_mix_g.astype(F32)[None, :]
    head_sum = jnp.asarray(np.kron(np.eye(N_HEADS_A), np.ones((HEAD_DIM_A, HEAD_DIM_A))), BF16)
    consts_a = (
        g_mix, w_qkv, head_sum,
        jnp.tile(qnorm_a_g.astype(F32) * SCALE_A, N_HEADS_A)[None, :],
        jnp.tile(knorm_a_g.astype(F32), N_HEADS_A)[None, :],
    )
    tm = TM_PROJ
    gq_col = (qnorm_b_g.astype(F32) * SCALE_B)[:, None]
    consts_b = (
        g_mix, w_cq, w_ckv, w_kr,
        cq_norm_g.astype(F32)[None, :], ckv_norm_g.astype(F32)[None, :],
        w_uqt, w_uk_p, w_uvt,
        jnp.broadcast_to(gq_col, (QK_HEAD_DIM_B, tm)), _lane_row(knorm_b_g[:QK_NOPE_DIM]),
        _lane_row(knorm_b_g[QK_NOPE_DIM:], QK_NOPE_DIM),
        jnp.broadcast_to(inv_freq[:, None], (QK_ROPE_DIM // 2, tm)),
        ones_col,
    )
    x2 = x.reshape(n, d)
    qa, ka, va = _proj_a_call(x, consts_a)
    qbt, kb, vbt = _proj_b_call(x2, positions.reshape(n // tm, 1, tm), consts_b)

    out_a = _attn_a_call(qa, ka, va, _attn_a_bias(rel_bias))
    out_b = _attn_b_call(qbt, kb.reshape(b, s, -1), vbt)

    out = _ffn_call(x.reshape(n, d), out_a.reshape(n, -1), out_b.reshape(n, -1),
                    w_o[:WIDTH_A].astype(BF16), w_o[WIDTH_A:].astype(BF16),
                    norm_ffn_g.astype(F32)[None, :], w_ff1.astype(BF16), w_ff2.astype(BF16))
    return out.reshape(b, s, d)
```

```python
import functools
import math

import numpy as np
import jax
import jax.numpy as jnp
from jax import lax
from jax.experimental import pallas as pl
from jax.experimental.pallas import tpu as pltpu

F32 = jnp.float32
BF16 = jnp.bfloat16

D_MODEL = 1024
HEAD_DIM_A = 64
N_HEADS_A = 8
WIDTH_A = N_HEADS_A * HEAD_DIM_A
DILATED_BRANCHES = ((128, 1), (512, 4), (2048, 16))
N_HEADS_B = 8
QK_NOPE_DIM = 64
QK_ROPE_DIM = 32
V_HEAD_DIM = 64
QK_HEAD_DIM_B = QK_NOPE_DIM + QK_ROPE_DIM
Q_LORA_RANK = 768
KV_LORA_RANK = 256
WIDTH_B = N_HEADS_B * V_HEAD_DIM
ROPE_THETA = 10000.0
D_FF = 4 * D_MODEL
REL_BUCKETS = 32
REL_MAX_DIST = 2048
EPS = 1e-6

LANES = 128
HALF = LANES // 2
HEAD_PAIRS = N_HEADS_A // 2
WIN = 128
NEG = -1e30
VMEM_LIMIT = 56 * 1024 * 1024

TM_PROJ = 512
TM_PROJ_A = 2048
SUB_A = 512
TM_FFN = 512
TQ_B = 512
FF_CHUNK = 1024
GROUP_A = 32
CLASSES = 4

LOG2E = math.log2(math.e)
SCALE_A = LOG2E / math.sqrt(HEAD_DIM_A)
SCALE_B = LOG2E / math.sqrt(QK_HEAD_DIM_B)

BF16_ROWS = 16
VT_ROWS = V_HEAD_DIM + BF16_ROWS

assert HEAD_DIM_A == HALF and V_HEAD_DIM == HALF
assert TM_PROJ == TQ_B


def _nt_dot(a, b):
    return lax.dot_general(a, b, (((1,), (1,)), ((), ())), preferred_element_type=F32)


def _dot(a, b):
    return jnp.dot(a, b, preferred_element_type=F32)


def _const_spec(shape):
    nd = len(shape)
    return pl.BlockSpec(shape, lambda *_: (0,) * nd, pipeline_mode=pl.Buffered(1))


def _rms_bf16(x, g_row):
    return (x * lax.rsqrt(jnp.mean(x * x, axis=-1, keepdims=True) + EPS) * g_row).astype(BF16)


def _proj_a_kernel(x_ref, gmix_ref, wqkv_ref, hsum_ref, gqa_ref, gka_ref, qa_ref, ka_ref, va_ref,
                   xs_ref):
    tm, d = x_ref.shape

    def head_sumsq(y):
        return _dot((y * y).astype(BF16), hsum_ref[...])

    sub = SUB_A // CLASSES
    for h in range(tm // SUB_A):
        toks = slice(h * SUB_A, (h + 1) * SUB_A)
        for c in range(d // LANES):
            xs_ref[c, toks, :] = x_ref[toks, c * LANES:(c + 1) * LANES]
        x = jnp.concatenate(
            [jnp.concatenate([xs_ref[c, pl.ds(r + h * SUB_A, sub, stride=CLASSES), :]
                              for r in range(CLASSES)], axis=0)
             for c in range(d // LANES)], axis=1)
        xn = _rms_bf16(x, gmix_ref[...])
        q = _dot(xn, wqkv_ref[:, :WIDTH_A])
        k = _dot(xn, wqkv_ref[:, WIDTH_A:2 * WIDTH_A])
        ssq = head_sumsq(q)
        v = _dot(xn, wqkv_ref[:, 2 * WIDTH_A:]).astype(BF16)
        ssk = head_sumsq(k)
        q = (q * lax.rsqrt(ssq / HEAD_DIM_A + EPS) * gqa_ref[...]).astype(BF16)
        k = (k * lax.rsqrt(ssk / HEAD_DIM_A + EPS) * gka_ref[...]).astype(BF16)
        for r in range(CLASSES):
            src = slice(r * sub, (r + 1) * sub)
            dst = slice(h * sub, (h + 1) * sub)
            qa_ref[r, dst] = q[src]
            ka_ref[r, dst] = k[src]
            va_ref[r, dst] = v[src]


def _proj_a_call(x, consts):
    b, s, d = x.shape
    tm = TM_PROJ_A
    run = tm // CLASSES
    out = jax.ShapeDtypeStruct((b, CLASSES, s // CLASSES, WIDTH_A), BF16)
    outs = pl.pallas_call(
        _proj_a_kernel,
        grid=(b, s // tm),
        in_specs=[pl.BlockSpec((None, tm, d), lambda bb, i: (bb, i, 0))]
        + [_const_spec(c.shape) for c in consts],
        out_specs=[pl.BlockSpec((None, CLASSES, run, WIDTH_A), lambda bb, i: (bb, 0, i, 0))] * 3,
        out_shape=[out] * 3,
        scratch_shapes=[pltpu.VMEM((d // LANES, tm, LANES), F32)],
        compiler_params=pltpu.CompilerParams(
            dimension_semantics=("parallel", "parallel"), vmem_limit_bytes=VMEM_LIMIT),
        name="proj_a",
    )(x, *consts)
    return [o.reshape(b, s, WIDTH_A) for o in outs]


def _proj_b_kernel(x_ref, posr_ref, gmix_ref, wcq_ref, wckv_ref, wkr_ref,
                   gcq_ref, gckv_ref, wuqt_ref, wuk_ref, wuvt_ref,
                   gqbt_ref, gkbn_ref, gkbr_ref, invft_ref, ones_ref,
                   qt_ref, kb_ref, vt_ref):
    tm = x_ref.shape[0]
    half = QK_ROPE_DIM // 2
    r0, r1, r2 = QK_NOPE_DIM, QK_NOPE_DIM + half, QK_HEAD_DIM_B
    xn = _rms_bf16(x_ref[...], gmix_ref[...])
    cq = _dot(xn, wcq_ref[...])
    ckv = _dot(xn, wckv_ref[...])
    kr = _dot(xn, wkr_ref[...])
    ang_t = invft_ref[...] * posr_ref[...].astype(F32)
    cs_t = jnp.cos(ang_t)
    sn_t = jnp.sin(ang_t)

    ckvn = _rms_bf16(ckv, gckv_ref[...])
    kn = _dot(ckvn, wuk_ref[...])
    vt_ref[...] = (_nt_dot(wuvt_ref[...], ckvn) + ones_ref[...]).astype(BF16)

    cqn = _rms_bf16(cq, gcq_ref[...])
    qt = _nt_dot(wuqt_ref[...], cqn)

    zt = jnp.zeros((r0, tm), F32)
    zb = jnp.zeros((LANES - r2, tm), F32)
    cs = jnp.concatenate([zt, cs_t, cs_t, zb], axis=0).T
    sn = jnp.concatenate([zt, -sn_t, sn_t, zb], axis=0).T
    ss_kr = jnp.sum(kr * kr, axis=-1, keepdims=True)
    krg = kr * gkbr_ref[...]
    kr_rot = krg * cs + (pltpu.roll(krg, LANES - half, 1) + pltpu.roll(krg, half, 1)) * sn
    for h in range(N_HEADS_B):
        sl = slice(h * LANES, (h + 1) * LANES)
        kh = kn[:, sl]
        rk = lax.rsqrt((jnp.sum(kh * kh, axis=-1, keepdims=True) + ss_kr) / QK_HEAD_DIM_B + EPS)
        kb_ref[:, sl] = ((kh * gkbn_ref[...] + kr_rot) * rk).astype(BF16)

    zpad = jnp.zeros((LANES - r2, tm), BF16)
    for h in range(N_HEADS_B):
        qh = qt[h * r2:(h + 1) * r2]
        rq = lax.rsqrt(jnp.sum(qh * qh, axis=0, keepdims=True) / QK_HEAD_DIM_B + EPS)
        y = qh * rq * gqbt_ref[...]
        y1, y2 = y[r0:r1], y[r1:r2]
        out = jnp.concatenate([y[:r0], y1 * cs_t - y2 * sn_t, y2 * cs_t + y1 * sn_t], axis=0)
        qt_ref[h * LANES:(h + 1) * LANES, :] = jnp.concatenate([out.astype(BF16), zpad], axis=0)


def _proj_b_call(x2, pos_row, consts):
    n = x2.shape[0]
    tm = TM_PROJ
    row = lambda i: (i, 0)
    blk3 = lambda i: (i, 0, 0)
    wide = N_HEADS_B * LANES
    vt_rows = N_HEADS_B * VT_ROWS
    return pl.pallas_call(
        _proj_b_kernel,
        grid=(n // tm,),
        in_specs=[pl.BlockSpec((tm, D_MODEL), row),
                  pl.BlockSpec((None, 1, tm), blk3)] + [_const_spec(c.shape) for c in consts],
        out_specs=[pl.BlockSpec((None, wide, tm), blk3), pl.BlockSpec((tm, wide), row),
                   pl.BlockSpec((None, vt_rows, tm), blk3)],
        out_shape=[jax.ShapeDtypeStruct((n // tm, wide, tm), BF16),
                   jax.ShapeDtypeStruct((n, wide), BF16),
                   jax.ShapeDtypeStruct((n // tm, vt_rows, tm), BF16)],
        compiler_params=pltpu.CompilerParams(
            dimension_semantics=("parallel",), vmem_limit_bytes=VMEM_LIMIT),
        name="proj_b",
    )(x2, pos_row, *consts)


def _attn_a_kernel(q_ref, k_ref, v_ref, bias_ref, o_ref,
                   qf, kf, vf, acc, den, m0, m1, onat, *, seq, pad):
    lane = lax.broadcasted_iota(jnp.int32, (1, LANES), 1)
    first_head = lane < HALF
    qf[...] = q_ref[...].astype(F32)
    zeros = jnp.zeros((pad, LANES), F32)
    kf[pl.ds(0, pad), :] = zeros
    vf[pl.ds(0, pad), :] = zeros
    kf[pl.ds(pad, seq), :] = k_ref[...].astype(F32)
    vf[pl.ds(pad, seq), :] = v_ref[...].astype(F32)

    cls_rows = seq // CLASSES

    for bi, (window, dil) in enumerate(DILATED_BRANCHES):
        assert window // dil == WIN
        nchunk = max(CLASSES // dil, 1)
        clen = WIN // nchunk
        stride = max(dil // CLASSES, 1)
        step = stride * clen
        nblk = seq // (dil * WIN)
        ublk = min(GROUP_A, nblk)
        ncls = GROUP_A // ublk

        def run(start, clen=clen, stride=stride):
            return pl.ds(start, clen, stride=stride) if stride > 1 else pl.ds(start, clen)

        def gather(ref, starts, run=run):
            return jnp.concatenate([ref[run(st), :] for st in starts], axis=0)

        def group(r0, g, bi=bi, dil=dil, nchunk=nchunk, clen=clen, stride=stride, step=step,
                  ublk=ublk, ncls=ncls, run=run, gather=gather):
            plans = []
            for cc in range(ncls):
                rd = r0 + cc
                if stride == 1:
                    bases = [pl.multiple_of((dil * c + rd) * cls_rows + step * ublk * g, 8)
                             for c in range(nchunk)]
                else:
                    bases = [(rd % CLASSES) * cls_rows + rd // CLASSES + step * ublk * g]
                for u in range(ublk):
                    plans.append((bases, u))

            def scores(bases, u):
                qstarts = [b0 + step * u for b0 in bases]
                if stride == 1:
                    rows16 = [pl.multiple_of(st, BF16_ROWS) for st in qstarts]
                    prev16 = [pl.multiple_of(jnp.maximum(st - step, 0), BF16_ROWS) for st in qstarts]
                    q = gather(q_ref, rows16)
                    zero = jnp.zeros_like(q)
                    qq = jnp.concatenate([jnp.where(first_head, q, zero),
                                          jnp.where(first_head, zero, q)], axis=0)
                    kk = jnp.concatenate([gather(k_ref, prev16), gather(k_ref, rows16)], axis=0)
                    vv = jnp.concatenate([gather(v_ref, prev16), gather(v_ref, rows16)], axis=0)
                else:
                    kstarts = [pad + b0 + step * (u - 1) for b0 in bases]
                    q = gather(qf, qstarts)
                    qq = jnp.concatenate([jnp.where(first_head, q, 0.0),
                                          jnp.where(first_head, 0.0, q)], axis=0).astype(BF16)
                    kk = jnp.concatenate([gather(kf, kstarts),
                                          gather(kf, [k0 + step for k0 in kstarts])],
                                         axis=0).astype(BF16)
                    vv = jnp.concatenate([gather(vf, kstarts),
                                          gather(vf, [k0 + step for k0 in kstarts])],
                                         axis=0).astype(BF16)
                first_blk = jnp.where(g == 0, 1, 0) if u == 0 else 0
                s = _nt_dot(qq, kk) + bias_ref[bi, first_blk]
                return qstarts, s, jnp.concatenate([vv, jnp.ones_like(vv)], axis=1)

            def update(qstarts, s, vext):
                ps, alphas = [], []
                for hh, m_s in enumerate((m0, m1)):
                    sh = s[hh * WIN:(hh + 1) * WIN]
                    m_blk = jnp.max(sh, axis=-1, keepdims=True)
                    if bi == 0:
                        m_new = jnp.broadcast_to(m_blk, (WIN, LANES))
                    else:
                        m_old = gather(m_s, qstarts)
                        m_new = jnp.maximum(m_old, m_blk)
                        alphas.append(jnp.exp2(m_old - m_new))
                    ps.append(jnp.exp2(sh - jnp.concatenate([m_new, m_new], axis=1)).astype(BF16))
                    for c, st in enumerate(qstarts):
                        m_s[run(st), :] = m_new[c * clen:(c + 1) * clen]
                res = _dot(jnp.concatenate(ps, axis=0), vext)
                acc_new = jnp.where(first_head, res[:WIN, :LANES], res[WIN:, :LANES])
                den_new = jnp.where(first_head, res[:WIN, LANES:], res[WIN:, LANES:])
                if bi > 0:
                    alpha = jnp.where(first_head, alphas[0], alphas[1])
                    acc_new = alpha * gather(acc, qstarts) + acc_new
                    den_new = alpha * gather(den, qstarts) + den_new
                for c, st in enumerate(qstarts):
                    acc[run(st), :] = acc_new[c * clen:(c + 1) * clen]
                    den[run(st), :] = den_new[c * clen:(c + 1) * clen]

            pending = scores(*plans[0])
            for nxt in plans[1:]:
                ahead = scores(*nxt)
                update(*pending)
                pending = ahead
            update(*pending)

        def class_body(ci, carry, ngrp=nblk // ublk, ncls=ncls, group=group):
            def grp_body(g, c):
                group(ci * ncls, g)
                return c
            return lax.fori_loop(0, ngrp, grp_body, carry)

        lax.fori_loop(0, dil // ncls, class_body, 0)

    for r in range(CLASSES):
        rows = pl.ds(r * cls_rows, cls_rows)
        onat[pl.ds(r, cls_rows, stride=CLASSES), :] = acc[rows, :] / den[rows, :]
    o_ref[...] = onat[...].astype(o_ref.dtype)


def _attn_a_call(qa, ka, va, bias):
    b, s, _ = qa.shape
    pad = WIN * DILATED_BRANCHES[-1][1] // CLASSES
    blk = pl.BlockSpec((None, s, LANES), lambda bb, p: (bb, 0, p))
    bias_spec = pl.BlockSpec((None,) + bias.shape[1:], lambda bb, p: (p, 0, 0, 0, 0))
    rows = pltpu.VMEM((s, LANES), F32)
    padded = pltpu.VMEM((pad + s, LANES), F32)
    return pl.pallas_call(
        functools.partial(_attn_a_kernel, seq=s, pad=pad),
        grid=(b, HEAD_PAIRS),
        in_specs=[blk, blk, blk, bias_spec],
        out_specs=blk,
        out_shape=jax.ShapeDtypeStruct((b, s, WIDTH_A), BF16),
        scratch_shapes=[rows, padded, padded, rows, rows, rows, rows, rows],
        compiler_params=pltpu.CompilerParams(
            dimension_semantics=("parallel", "parallel"), vmem_limit_bytes=VMEM_LIMIT),
        name="attn_a",
    )(qa, ka, va, bias)


def _t5_causal_bucket(dist):
    dist = np.asarray(dist, dtype=np.int64)
    max_exact = REL_BUCKETS // 2
    safe = np.maximum(dist, 1).astype(np.float32)
    large = max_exact + (np.log(safe / max_exact) / math.log(REL_MAX_DIST / max_exact)
                         * (REL_BUCKETS - max_exact)).astype(np.int64)
    large = np.minimum(large, REL_BUCKETS - 1)
    return np.where(dist < max_exact, dist, large).astype(np.int32)


def _bias_bucket_tables():
    i = np.arange(WIN)[:, None]
    c = np.arange(2 * WIN)[None, :]
    sub = WIN + i - c
    tables = []
    for _, dil in DILATED_BRANCHES:
        bucket = _t5_causal_bucket(np.clip(sub, 0, WIN) * dil)
        bucket = np.where((sub >= 0) & (sub <= WIN), bucket, REL_BUCKETS)
        nchunk = max(CLASSES // dil, 1)
        clen = WIN // nchunk
        bucket = bucket.reshape(clen, nchunk, 2, clen, nchunk)
        tables.append(np.transpose(bucket, (1, 0, 2, 4, 3)).reshape(WIN, 2 * WIN))
    return np.stack(tables).astype(np.int32)


def _bias_kernel(bias_ref, bucket_ref, o_ref):
    bucket = bucket_ref[...]
    first_blk_cols = lax.broadcasted_iota(jnp.int32, bucket.shape, 1) >= WIN
    for h in range(N_HEADS_A):
        t = jnp.full(bucket.shape, NEG, F32)
        for bkt in range(REL_BUCKETS):
            t = jnp.where(bucket == bkt, bias_ref[bkt, h], t)
        rows = slice((h % 2) * WIN, (h % 2 + 1) * WIN)
        o_ref[h // 2, 0, rows, :] = t
        o_ref[h // 2, 1, rows, :] = jnp.where(first_blk_cols, t, NEG)


def _attn_a_bias(rel_bias):
    buckets = jnp.asarray(_bias_bucket_tables())
    nbr = len(DILATED_BRANCHES)
    return pl.pallas_call(
        _bias_kernel,
        grid=(nbr,),
        in_specs=[pl.BlockSpec(memory_space=pltpu.SMEM),
                  pl.BlockSpec((None, WIN, 2 * WIN), lambda i: (i, 0, 0))],
        out_specs=pl.BlockSpec((HEAD_PAIRS, None, 2, 2 * WIN, 2 * WIN), lambda i: (0, i, 0, 0, 0)),
        out_shape=jax.ShapeDtypeStruct((HEAD_PAIRS, nbr, 2, 2 * WIN, 2 * WIN), F32),
        compiler_params=pltpu.CompilerParams(dimension_semantics=("parallel",)),
        name="bias_a",
    )(rel_bias.astype(F32) * LOG2E, buckets)


def _attn_b_kernel(qt_ref, k_ref, vt_ref, o_ref, acc0, acc1, m0, m1, s_even, s_odd, mb_even, mb_odd):
    tq = TQ_B
    nq = qt_ref.shape[0]
    heads = ((acc0, m0), (acc1, m1))
    bufs = ((s_even, mb_even), (s_odd, mb_odd))

    hq = tq // 2

    def causal(shape):
        return (lax.broadcasted_iota(jnp.int32, shape, 0)
                <= lax.broadcasted_iota(jnp.int32, shape, 1))

    def step(fold=None, make=None):
        for hh, (acc, m_s) in enumerate(heads):
            sl = slice(hh * LANES, (hh + 1) * LANES)
            if make is None:
                mats = [lambda: None, lambda: None]
            elif make[0] == "full":
                _, qi_a, kb_a, (s_a, mb_a) = make
                k_blk = k_ref[pl.ds(pl.multiple_of(kb_a * tq, tq), tq), sl]

                def mat(part, qi_a=qi_a, s_a=s_a, mb_a=mb_a, k_blk=k_blk):
                    half = slice(part * hq, (part + 1) * hq)
                    st = _dot(k_blk, qt_ref[qi_a, hh, :, half])
                    s_a[hh, :, half] = st
                    mb_a[hh, :, half] = jnp.max(st, axis=0, keepdims=True)
                mats = [functools.partial(mat, 0), functools.partial(mat, 1)]
            else:
                _, qi_a, s_a = make
                k0 = qi_a * tq

                def top_mat(qi_a=qi_a, s_a=s_a, k0=k0):
                    s_a[hh, :hq, :] = _dot(k_ref[pl.ds(k0, hq), sl], qt_ref[qi_a, hh])
                def bot_mat(qi_a=qi_a, s_a=s_a, k0=k0):
                    s_a[hh, hq:, :hq] = _dot(k_ref[pl.ds(k0 + hq, hq), sl], qt_ref[qi_a, hh, :, hq:])
                mats = [top_mat, bot_mat]
            if fold is None:
                for mat_piece in mats:
                    mat_piece()
                continue

            m_old = m_s[...]
            if fold[0] == "full":
                _, kb, (s_c, mb_c) = fold
                m_new = jnp.maximum(m_old, mb_c[hh])
                m_s[...] = m_new
                p0 = jnp.exp2(s_c[hh, :hq, :] - m_new).astype(BF16)
                mats[0]()
                p1 = jnp.exp2(s_c[hh, hq:, :] - m_new).astype(BF16)
                mats[1]()
                pv = _dot(vt_ref[kb, hh], jnp.concatenate([p0, p1], axis=0))
            else:
                _, kb, s_c = fold
                top = jnp.where(causal((hq, tq)), s_c[hh, :hq, :], NEG)
                bot = jnp.where(causal((hq, hq)), s_c[hh, hq:, :hq], NEG)
                m_top = jnp.max(top, axis=0, keepdims=True)
                m_bot = jnp.max(jnp.concatenate([jnp.full((hq, hq), NEG, F32), bot], axis=1),
                                axis=0, keepdims=True)
                m_new = jnp.maximum(m_old, jnp.maximum(m_top, m_bot))
                m_s[...] = m_new
                p0 = jnp.exp2(top - m_new).astype(BF16)
                mats[0]()
                p1 = jnp.exp2(bot - m_s[:, hq:]).astype(BF16)
                mats[1]()
                pv = _dot(vt_ref[kb, hh, :, :hq], p0) + jnp.concatenate(
                    [jnp.zeros((VT_ROWS, hq), F32), _dot(vt_ref[kb, hh, :, hq:], p1)], axis=1)
            acc[...] = jnp.exp2(m_old - m_new) * acc[...] + pv

    step(make=("diag", 0, bufs[0][0]))
    step0 = 0
    for qi in range(nq):
        cur, nxt = bufs[step0 % 2], bufs[(step0 + 1) % 2]
        for acc, m_s in heads:
            acc[...] = jnp.zeros_like(acc)
            m_s[...] = jnp.full(m_s.shape, NEG, F32)

        def pair_body(i, c, qi=qi, cur=cur, nxt=nxt):
            kb = 2 * i
            step(("full", kb, cur), ("full", qi, kb + 1, nxt))
            step(("full", kb + 1, nxt), ("full", qi, kb + 2, cur))
            return c

        npairs = qi // 2
        peel = qi % 2 == 0 and npairs > 0
        lax.fori_loop(0, npairs - 1 if peel else npairs, pair_body, 0)
        if peel:
            step(("full", qi - 2, cur), ("full", qi, qi - 1, nxt))
            step(("full", qi - 1, nxt), ("diag", qi, cur[0]))
        last, other = cur, nxt
        if qi % 2 == 1:
            step(("full", qi - 1, cur), ("diag", qi, nxt[0]))
            last, other = nxt, cur
        step(("diag", qi, last[0]), ("full", qi + 1, 0, other) if qi + 1 < nq else None)

        outs = []
        for acc, _ in heads:
            a = acc[...]
            outs.append(a[:V_HEAD_DIM] / a[V_HEAD_DIM:V_HEAD_DIM + 1])
        o_ref[pl.ds(qi * tq, tq), :] = jnp.concatenate(outs, axis=0).T.astype(o_ref.dtype)
        step0 += qi + 1


def _attn_b_call(qbt, kb, vbt):
    b, s, _ = kb.shape
    tq = TQ_B
    nkb = s // tq
    qbt = qbt.reshape(b, nkb, HEAD_PAIRS, 2, LANES, tq)
    vbt = vbt.reshape(b, nkb, HEAD_PAIRS, 2, VT_ROWS, tq)
    return pl.pallas_call(
        _attn_b_kernel,
        grid=(b, HEAD_PAIRS),
        in_specs=[pl.BlockSpec((None, nkb, None, 2, LANES, tq), lambda bb, p: (bb, 0, p, 0, 0, 0)),
                  pl.BlockSpec((None, s, 2 * LANES), lambda bb, p: (bb, 0, p)),
                  pl.BlockSpec((None, nkb, None, 2, VT_ROWS, tq),
                               lambda bb, p: (bb, 0, p, 0, 0, 0))],
        out_specs=pl.BlockSpec((None, s, LANES), lambda bb, p: (bb, 0, p)),
        out_shape=jax.ShapeDtypeStruct((b, s, WIDTH_B), BF16),
        scratch_shapes=[pltpu.VMEM((VT_ROWS, tq), F32)] * 2 + [pltpu.VMEM((1, tq), F32)] * 2
        + [pltpu.VMEM((2, tq, tq), F32)] * 2 + [pltpu.VMEM((2, 1, tq), F32)] * 2,
        compiler_params=pltpu.CompilerParams(
            dimension_semantics=("parallel", "parallel"), vmem_limit_bytes=VMEM_LIMIT),
        name="attn_b",
    )(qbt, kb, vbt)


def _ffn_kernel(x_ref, a_ref, b_ref, woa_ref, wob_ref, g_ref, w1_ref, w2_ref, o_ref):
    h = x_ref[...] + _dot(a_ref[...], woa_ref[...]) + _dot(b_ref[...], wob_ref[...])
    hn = (h * lax.rsqrt(jnp.mean(h * h, axis=-1, keepdims=True) + EPS) * g_ref[...]).astype(BF16)
    mlp = None
    for c in range(D_FF // FF_CHUNK):
        sl = slice(c * FF_CHUNK, (c + 1) * FF_CHUNK)
        hid = jnp.square(jnp.maximum(_dot(hn, w1_ref[:, sl]), 0.0)).astype(BF16)
        d = _dot(hid, w2_ref[sl, :])
        mlp = d if mlp is None else mlp + d
    o_ref[...] = h + mlp


def _ffn_call(x2, a2, b2, woa, wob, g, w1, w2):
    n = x2.shape[0]
    tm = TM_FFN
    row = lambda i: (i, 0)
    consts = (woa, wob, g, w1, w2)
    return pl.pallas_call(
        _ffn_kernel,
        grid=(n // tm,),
        in_specs=[pl.BlockSpec((tm, D_MODEL), row), pl.BlockSpec((tm, WIDTH_A), row),
                  pl.BlockSpec((tm, WIDTH_B), row)] + [_const_spec(c.shape) for c in consts],
        out_specs=pl.BlockSpec((tm, D_MODEL), row),
        out_shape=jax.ShapeDtypeStruct((n, D_MODEL), F32),
        compiler_params=pltpu.CompilerParams(
            dimension_semantics=("parallel",), vmem_limit_bytes=VMEM_LIMIT),
        name="ffn",
    )(x2, a2, b2, *consts)


def _pad_heads(w, n_heads, width):
    k = w.shape[0]
    w = w.reshape(k, n_heads, width)
    return jnp.pad(w, ((0, 0), (0, 0), (0, LANES - width))).reshape(k, n_heads * LANES)


def _lane_row(g, offset=0):
    return jnp.pad(g.astype(F32), (offset, LANES - offset - g.shape[0]))[None, :]


def kernel(x, positions, norm_mix_g, w_in, qnorm_a_g, knorm_a_g, rel_bias, cq_norm_g, ckv_norm_g,
           w_uq, w_ukv, qnorm_b_g, knorm_b_g, w_o, norm_ffn_g, w_ff1, w_ff2):
    b, s, d = x.shape
    n = b * s
    c0 = 3 * WIDTH_A
    c1 = c0 + Q_LORA_RANK
    c2 = c1 + KV_LORA_RANK
    w_qkv = w_in[:, :c0].astype(BF16)
    w_cq = w_in[:, c0:c1].astype(BF16)
    w_ckv = w_in[:, c1:c2].astype(BF16)
    w_kr = jnp.pad(w_in[:, c2:], ((0, 0), (QK_NOPE_DIM, LANES - QK_HEAD_DIM_B))).astype(BF16)
    w_uqt = w_uq.T.astype(BF16)
    w_ukv3 = w_ukv.reshape(KV_LORA_RANK, N_HEADS_B, QK_NOPE_DIM + V_HEAD_DIM)
    w_uk_p = _pad_heads(w_ukv3[:, :, :QK_NOPE_DIM].reshape(KV_LORA_RANK, -1),
                        N_HEADS_B, QK_NOPE_DIM).astype(BF16)
    ones_rows = VT_ROWS - V_HEAD_DIM
    w_uvt = jnp.transpose(w_ukv3[:, :, QK_NOPE_DIM:], (1, 2, 0))
    w_uvt = jnp.pad(w_uvt, ((0, 0), (0, ones_rows), (0, 0)))
    w_uvt = w_uvt.reshape(N_HEADS_B * VT_ROWS, KV_LORA_RANK).astype(BF16)
    inv_freq = 1.0 / (ROPE_THETA ** (jnp.arange(0, QK_ROPE_DIM, 2, dtype=F32) / QK_ROPE_DIM))
    ones_col = jnp.tile(jnp.concatenate([jnp.zeros((V_HEAD_DIM,), F32), jnp.ones((ones_rows,), F32)]),
                        N_HEADS_B)[:, None]
    g_mix = norm_mix_g.astype(F32)[None, :]
    head_sum = jnp.asarray(np.kron(np.eye(N_HEADS_A), np.ones((HEAD_DIM_A, HEAD_DIM_A))), BF16)
    consts_a = (
        g_mix, w_qkv, head_sum,
        jnp.tile(qnorm_a_g.astype(F32) * SCALE_A, N_HEADS_A)[None, :],
        jnp.tile(knorm_a_g.astype(F32), N_HEADS_A)[None, :],
    )
    tm = TM_PROJ
    gq_col = (qnorm_b_g.astype(F32) * SCALE_B)[:, None]
    consts_b = (
        g_mix, w_cq, w_ckv, w_kr,
        cq_norm_g.astype(F32)[None, :], ckv_norm_g.astype(F32)[None, :],
        w_uqt, w_uk_p, w_uvt,
        jnp.broadcast_to(gq_col, (QK_HEAD_DIM_B, tm)), _lane_row(knorm_b_g[:QK_NOPE_DIM]),
        _lane_row(knorm_b_g[QK_NOPE_DIM:], QK_NOPE_DIM),
        jnp.broadcast_to(inv_freq[:, None], (QK_ROPE_DIM // 2, tm)),
        ones_col,
    )
    x2 = x.reshape(n, d)
    qa, ka, va = _proj_a_call(x, consts_a)
    qbt, kb, vbt = _proj_b_call(x2, positions.reshape(n // tm, 1, tm), consts_b)

    out_a = _attn_a_call(qa, ka, va, _attn_a_bias(rel_bias))
    out_b = _attn_b_call(qbt, kb.reshape(b, s, -1), vbt)

    out = _ffn_call(x.reshape(n, d), out_a.reshape(n, -1), out_b.reshape(n, -1),
                    w_o[:WIDTH_A].astype(BF16), w_o[WIDTH_A:].astype(BF16),
                    norm_ffn_g.astype(F32)[None, :], w_ff1.astype(BF16), w_ff2.astype(BF16))
    return out.reshape(b, s, d)
```
